```python
import jax, jax.numpy as jnp
from jax import lax
import numpy as np

D_MODEL = 2048
BATCH = 8
SEQ = 2048
DEPTH = 2

CTX_LEN = 256
GRID_W = 64
EPS = 1e-6

CHUNK = 128
GMLP_GROUPS = 8
W_A = D_MODEL // 2
GMLP_GW = W_A // GMLP_GROUPS

W_B = D_MODEL // 2
F_GROUPS = 4
F_GW = W_B // F_GROUPS

HEAD_DIM = 128
NA_HEADS = 8
W_C = NA_HEADS * HEAD_DIM
NA_KH = 8
NA_KW = 16
NA_BAND = 2 * NA_KW

OFF_U = 0
OFF_VG = OFF_U + W_A
OFF_GA = OFF_VG + W_A
OFF_F = OFF_GA + W_A
OFF_GF = OFF_F + W_B
OFF_Q = OFF_GF + W_B
OFF_K = OFF_Q + W_C
OFF_V = OFF_K + W_C
OFF_GN = OFF_V + W_C
OFF_MERGE = OFF_GN + W_C
W_IN = OFF_MERGE + 3 * D_MODEL

kernel_name = "hybrid_gmlp_fnet_natten_prefix_block"


def rms_norm(x, g):
    xf = x.astype(jnp.float32)
    y = xf * lax.rsqrt(jnp.mean(xf * xf, axis=-1, keepdims=True) + EPS)
    return (y * g.astype(jnp.float32)).astype(x.dtype)


def layer_norm(x, g, b):
    xf = x.astype(jnp.float32)
    mu = jnp.mean(xf, axis=-1, keepdims=True)
    xc = xf - mu
    y = xc * lax.rsqrt(jnp.mean(xc * xc, axis=-1, keepdims=True) + EPS)
    return (y * g.astype(jnp.float32) + b.astype(jnp.float32)).astype(x.dtype)


def heads(t):
    return t.reshape(t.shape[:-1] + (NA_HEADS, HEAD_DIM))


def chunk_spatial_gating(u, v, ln_g, ln_b, ws, bs):
    B, N, _ = u.shape
    u = jax.nn.gelu(u)
    v = layer_norm(jax.nn.gelu(v), ln_g, ln_b)
    vg = v.reshape(B, N // CHUNK, CHUNK, GMLP_GROUPS, GMLP_GW)
    sv = jnp.einsum('gpq,bnqgc->bnpgc', ws, vg) + bs.T[:, :, None]
    return u * sv.reshape(B, N, W_A)


def fourier_mix(xf):
    B, N, _ = xf.shape
    xg = xf.reshape(B, N, F_GROUPS, F_GW).astype(jnp.float32)
    y = jnp.fft.fft2(xg, axes=(1, 3), norm="ortho").real
    return y.reshape(B, N, W_B).astype(xf.dtype)


def context_self_attention(qc, kc, vc):
    s = jnp.einsum('bqhd,bkhd->bhqk', qc, kc, preferred_element_type=jnp.float32) * (HEAD_DIM ** -0.5)
    p = jax.nn.softmax(s, axis=-1).astype(vc.dtype)
    o = jnp.einsum('bhqk,bkhd->bqhd', p, vc)
    return o.reshape(o.shape[:2] + (W_C,))


def neighbourhood_attention(q, k, v, kc, vc, rpb):
    B, S, H, Dh = q.shape
    rows = S // GRID_W
    kh = min(NA_KH, rows)
    r = jnp.arange(rows)
    row_start = jnp.clip(r - kh // 2, 0, rows - kh)
    row_idx = row_start[:, None] + jnp.arange(kh)[None, :]
    dr = row_idx - r[:, None]
    qg = q.reshape(B, rows, GRID_W, H, Dh)
    kg = k.reshape(B, rows, GRID_W, H, Dh)
    vg = v.reshape(B, rows, GRID_W, H, Dh)
    scale = HEAD_DIM ** -0.5
    outs = []
    for j in range(GRID_W // NA_KW):
        q0 = j * NA_KW
        band0 = min(max(q0 - NA_KW // 2, 0), GRID_W - NA_BAND)
        cols_q = q0 + np.arange(NA_KW)
        col_start = np.clip(cols_q - NA_KW // 2, 0, GRID_W - NA_KW)
        cols_k = band0 + np.arange(NA_BAND)
        col_mask = jnp.asarray((cols_k[None, :] >= col_start[:, None])
                               & (cols_k[None, :] < col_start[:, None] + NA_KW))
        dc = jnp.asarray(cols_k[None, :] - cols_q[:, None])
        qb = qg[:, :, q0:q0 + NA_KW]
        kb = kg[:, :, band0:band0 + NA_BAND][:, row_idx]
        vb = vg[:, :, band0:band0 + NA_BAND][:, row_idx]
        s_loc = jnp.einsum('brqhd,brkmhd->bhrqkm', qb, kb,
                           preferred_element_type=jnp.float32) * scale
        bias = rpb[:, dr[:, None, :, None] + (NA_KH - 1),
                   dc[None, :, None, :] + (NA_KW - 1)]
        s_loc = jnp.where(col_mask[None, None, None, :, None, :],
                          s_loc + bias.astype(jnp.float32)[None], -jnp.inf)
        s_loc = s_loc.reshape(B, H, rows, NA_KW, kh * NA_BAND)
        s_ctx = jnp.einsum('brqhd,bkhd->bhrqk', qb, kc,
                           preferred_element_type=jnp.float32) * scale
        p = jax.nn.softmax(jnp.concatenate([s_loc, s_ctx], axis=-1), axis=-1).astype(v.dtype)
        p_loc = p[..., :kh * NA_BAND].reshape(B, H, rows, NA_KW, kh, NA_BAND)
        p_ctx = p[..., kh * NA_BAND:]
        o = (jnp.einsum('bhrqkm,brkmhd->brqhd', p_loc, vb)
             + jnp.einsum('bhrqk,bkhd->brqhd', p_ctx, vc))
        outs.append(o)
    o = jnp.concatenate(outs, axis=2)
    return o.reshape(B, S, W_C)


def merge_branches(z, attn, ln_g, ln_b, ws, bs, w_pa, w_pf, w_pn, w_out):
    a = chunk_spatial_gating(z[..., OFF_U:OFF_VG], z[..., OFF_VG:OFF_GA], ln_g, ln_b, ws, bs) \
        * jax.nn.silu(z[..., OFF_GA:OFF_F])
    f = fourier_mix(z[..., OFF_F:OFF_GF]) * jax.nn.silu(z[..., OFF_GF:OFF_Q])
    n = attn * jax.nn.silu(z[..., OFF_GN:OFF_MERGE])
    g = jax.nn.sigmoid(z[..., OFF_MERGE:])
    g_a, g_f, g_n = g[..., :D_MODEL], g[..., D_MODEL:2 * D_MODEL], g[..., 2 * D_MODEL:]
    y = g_a * (a @ w_pa) + g_f * (f @ w_pf) + g_n * (n @ w_pn)
    return y @ w_out


def setup_inputs(seed: int = 0) -> dict:
    key = jax.random.key(seed)
    ks = jax.random.split(key, 20)
    L, D = DEPTH, D_MODEL

    def nrm(k, shape, s):
        return jax.random.normal(k, shape, jnp.float32) * s

    return {
        "x": nrm(ks[0], (BATCH, SEQ, D), 1.0),
        "c": nrm(ks[1], (BATCH, D), 1.0),
        "ctx": nrm(ks[2], (BATCH, CTX_LEN, D), 1.0),
        "c_ctx": nrm(ks[3], (D,), 1.0),
        "norm_g": 1.0 + nrm(ks[4], (L, D), 0.02),
        "w_ada": nrm(ks[5], (L, D, 3 * D), 0.5 * D ** -0.5),
        "b_ada": nrm(ks[6], (L, 3 * D), 0.01),
        "w_in": nrm(ks[7], (L, D, W_IN), D ** -0.5),
        "gmlp_ln_g": 1.0 + nrm(ks[8], (L, W_A), 0.02),
        "gmlp_ln_b": nrm(ks[9], (L, W_A), 0.02),
        "gmlp_ws": nrm(ks[10], (L, GMLP_GROUPS, CHUNK, CHUNK), CHUNK ** -0.5),
        "gmlp_bs": 1.0 + nrm(ks[11], (L, GMLP_GROUPS, CHUNK), 0.02),
        "q_norm_g": 1.0 + nrm(ks[12], (L, HEAD_DIM), 0.02),
        "k_norm_g": 1.0 + nrm(ks[13], (L, HEAD_DIM), 0.02),
        "rpb": nrm(ks[14], (L, NA_HEADS, 2 * NA_KH - 1, 2 * NA_KW - 1), 0.1),
        "w_pa": nrm(ks[15], (L, W_A, D), W_A ** -0.5),
        "w_pf": nrm(ks[16], (L, W_B, D), W_B ** -0.5),
        "w_pn": nrm(ks[17], (L, W_C, D), W_C ** -0.5),
        "w_out": nrm(ks[18], (L, D, D), D ** -0.5),
    }


def reference(x, c, ctx, c_ctx, norm_g, w_ada, b_ada, w_in, gmlp_ln_g, gmlp_ln_b, gmlp_ws,
              gmlp_bs, q_norm_g, k_norm_g, rpb, w_pa, w_pf, w_pn, w_out):
    D = D_MODEL
    silu_c = jax.nn.silu(c)
    silu_cc = jax.nn.silu(c_ctx)
    for l in range(DEPTH):
        last = l == DEPTH - 1
        mod = silu_c @ w_ada[l] + b_ada[l]
        shift, scale, gate = mod[:, :D], mod[:, D:2 * D], mod[:, 2 * D:]
        mod_c = silu_cc @ w_ada[l] + b_ada[l]
        shift_c, scale_c, gate_c = mod_c[:D], mod_c[D:2 * D], mod_c[2 * D:]

        h = rms_norm(x, norm_g[l]) * (1.0 + scale[:, None, :]) + shift[:, None, :]
        hc = rms_norm(ctx, norm_g[l]) * (1.0 + scale_c) + shift_c

        if last:
            zkv = hc @ w_in[l][:, OFF_K:OFF_GN]
            kc = rms_norm(heads(zkv[..., :W_C]), k_norm_g[l])
            vc = heads(zkv[..., W_C:])
        else:
            zc = hc @ w_in[l]
            qc = rms_norm(heads(zc[..., OFF_Q:OFF_K]), q_norm_g[l])
            kc = rms_norm(heads(zc[..., OFF_K:OFF_V]), k_norm_g[l])
            vc = heads(zc[..., OFF_V:OFF_GN])
            attn_c = context_self_attention(qc, kc, vc)
            out_c = merge_branches(zc, attn_c, gmlp_ln_g[l], gmlp_ln_b[l], gmlp_ws[l], gmlp_bs[l],
                                   w_pa[l], w_pf[l], w_pn[l], w_out[l])

        z = h @ w_in[l]
        q = rms_norm(heads(z[..., OFF_Q:OFF_K]), q_norm_g[l])
        k = rms_norm(heads(z[..., OFF_K:OFF_V]), k_norm_g[l])
        v = heads(z[..., OFF_V:OFF_GN])
        attn = neighbourhood_attention(q, k, v, kc, vc, rpb[l])
        out = merge_branches(z, attn, gmlp_ln_g[l], gmlp_ln_b[l], gmlp_ws[l], gmlp_bs[l],
                             w_pa[l], w_pf[l], w_pn[l], w_out[l])
        x = x + gate[:, None, :] * out
        if not last:
            ctx = ctx + gate_c * out_c
    return x
```

```python
import functools

import jax
import jax.numpy as jnp
import numpy as np
from jax import lax
from jax.experimental import pallas as pl
from jax.experimental.pallas import tpu as pltpu

F32 = jnp.float32
BF16 = jnp.bfloat16

D_MODEL = 2048
CTX_LEN = 256
GRID_W = 64
EPS = 1e-6
CHUNK = 128
GMLP_GROUPS = 8
W_A = D_MODEL // 2
W_B = D_MODEL // 2
F_GROUPS = 4
F_GW = W_B // F_GROUPS
HEAD_DIM = 128
NA_HEADS = 8
W_C = NA_HEADS * HEAD_DIM
NA_KH = 8
NA_KW = 16

OFF_U = 0
OFF_VG = OFF_U + W_A
OFF_GA = OFF_VG + W_A
OFF_F = OFF_GA + W_A
OFF_GF = OFF_F + W_B
OFF_Q = OFF_GF + W_B
OFF_K = OFF_Q + W_C
OFF_V = OFF_K + W_C
OFF_GN = OFF_V + W_C
OFF_MERGE = OFF_GN + W_C
W_IN = OFF_MERGE + 3 * D_MODEL

COL = 1024
MOD_ROWS = 16
VMEM_LIMIT = 56 * 1024 * 1024
Z_DTYPE = jnp.bfloat16

Q_ROWS = 4
WIN_ROWS = 12
NEG = -1e30
ATT_SCALE = HEAD_DIM ** -0.5


def _sigmoid(t):
    return 1.0 / (1.0 + jnp.exp(-t))


def _silu(t):
    return t * _sigmoid(t)


def _gelu(t):
    return 0.5 * t * (1.0 + jnp.tanh(np.sqrt(2.0 / np.pi) * (t + 0.044715 * (t * t * t))))


def _dot(a, b):
    return jnp.dot(a, b, preferred_element_type=F32)


def _dot_nt(a, b):
    return lax.dot_general(a, b, (((1,), (1,)), ((), ())), preferred_element_type=F32)


def _params(*sem):
    return pltpu.CompilerParams(dimension_semantics=sem, vmem_limit_bytes=VMEM_LIMIT)


def _mod_kernel(cc_ref, w_ref, b_ref, o_ref):
    s = _silu(cc_ref[...])
    s_hi = s.astype(BF16)
    s_lo = (s - s_hi.astype(F32)).astype(BF16)
    w = w_ref[0]
    w_hi = w.astype(BF16)
    w_lo = (w - w_hi.astype(F32)).astype(BF16)
    o_ref[0] = _dot(s_hi, w_hi) + _dot(s_lo, w_hi) + _dot(s_hi, w_lo) + b_ref[0]


def _modulation(cc, w_ada, b_ada):
    L, D, N = w_ada.shape
    tn = 1024
    return pl.pallas_call(
        _mod_kernel,
        grid=(L, N // tn),
        in_specs=[
            pl.BlockSpec((MOD_ROWS, D), lambda l, j: (0, 0)),
            pl.BlockSpec((1, D, tn), lambda l, j: (l, 0, j)),
            pl.BlockSpec((1, 1, tn), lambda l, j: (l, 0, j)),
        ],
        out_specs=pl.BlockSpec((1, MOD_ROWS, tn), lambda l, j: (l, 0, j)),
        out_shape=jax.ShapeDtypeStruct((L, MOD_ROWS, N), F32),
        compiler_params=_params("arbitrary", "arbitrary"),
        name="adaln_mod",
    )(cc, w_ada, b_ada.reshape(L, 1, N))


def _in_proj_kernel(x_ref, g_ref, sh_ref, sc_ref, w_ref, o_ref, h_scr, *, col0):
    j = pl.program_id(1)

    @pl.when(j == 0)
    def _():
        x = x_ref[...]
        y = x * lax.rsqrt(jnp.mean(x * x, axis=-1, keepdims=True) + EPS) * g_ref[...]
        h_scr[...] = (y * (1.0 + sc_ref[0]) + sh_ref[0]).astype(BF16)

    c = j + col0

    def project(act):
        o_ref[...] = act(_dot(h_scr[...], w_ref[...])).astype(o_ref.dtype)

    @pl.when(c < OFF_GA // COL)
    def _():
        project(_gelu)

    @pl.when((c == OFF_GA // COL) | (c == OFF_GF // COL) | (c == OFF_GN // COL))
    def _():
        project(_silu)

    @pl.when(c >= OFF_MERGE // COL)
    def _():
        project(_sigmoid)

    @pl.when((c == OFF_F // COL) | ((c >= OFF_Q // COL) & (c < OFF_GN // COL)))
    def _():
        project(lambda z: z)


def _in_proj(x2, g, mod3, w, *, rows_per_mod, mod_row0, col0, ncol):
    T, D = x2.shape
    tm = min(1024, T)
    mrow = lambda i: mod_row0 + i // rows_per_mod
    return pl.pallas_call(
        functools.partial(_in_proj_kernel, col0=col0),
        grid=(T // tm, ncol),
        in_specs=[
            pl.BlockSpec((tm, D), lambda i, j: (i, 0), pipeline_mode=pl.Buffered(1)),
            pl.BlockSpec((1, D), lambda i, j: (0, 0)),
            pl.BlockSpec((1, 1, D), lambda i, j: (mrow(i), 0, 0)),
            pl.BlockSpec((1, 1, D), lambda i, j: (mrow(i), 0, 1)),
            pl.BlockSpec((D, COL), lambda i, j: (0, j + col0)),
        ],
        out_specs=pl.BlockSpec((tm, COL), lambda i, j: (i, j)),
        out_shape=jax.ShapeDtypeStruct((T, ncol * COL), Z_DTYPE),
        scratch_shapes=[pltpu.VMEM((tm, D), BF16)],
        compiler_params=_params("arbitrary", "arbitrary"),
        name="in_proj",
    )(x2, g, mod3, mod3, w)


def _gmlp_kernel(u_ref, vg_ref, ga_ref, lng_ref, lnb_ref, ws_ref, bst_ref, o_ref, vn_scr):
    v = vg_ref[...].astype(F32)
    mu = jnp.mean(v, axis=-1, keepdims=True)
    vc = v - mu
    vn = vc * lax.rsqrt(jnp.mean(vc * vc, axis=-1, keepdims=True) + EPS)
    vn_scr[...] = (vn * lng_ref[...] + lnb_ref[...]).astype(BF16)
    tm = u_ref.shape[0]
    for ch in range(tm // CHUNK):
        rows = slice(ch * CHUNK, (ch + 1) * CHUNK)
        for g in range(GMLP_GROUPS):
            cols = slice(g * CHUNK, (g + 1) * CHUNK)
            sv = _dot(ws_ref[g], vn_scr[rows, cols]) + bst_ref[:, g:g + 1]
            a = u_ref[rows, cols].astype(F32) * sv * ga_ref[rows, cols].astype(F32)
            o_ref[rows, cols] = a.astype(o_ref.dtype)


def _gmlp(z, lng, lnb, ws, bst):
    T = z.shape[0]
    tm = 512
    blk = lambda c: pl.BlockSpec((tm, W_A), lambda i: (i, c))
    return pl.pallas_call(
        _gmlp_kernel,
        grid=(T // tm,),
        in_specs=[
            blk(OFF_U // W_A), blk(OFF_VG // W_A), blk(OFF_GA // W_A),
            pl.BlockSpec((1, W_A), lambda i: (0, 0)),
            pl.BlockSpec((1, W_A), lambda i: (0, 0)),
            pl.BlockSpec((GMLP_GROUPS, CHUNK, CHUNK), lambda i: (0, 0, 0)),
            pl.BlockSpec((CHUNK, GMLP_GROUPS), lambda i: (0, 0)),
        ],
        out_specs=pl.BlockSpec((tm, W_A), lambda i: (i, 0)),
        out_shape=jax.ShapeDtypeStruct((T, W_A), BF16),
        scratch_shapes=[pltpu.VMEM((tm, W_A), BF16)],
        compiler_params=_params("arbitrary"),
        name="gmlp",
    )(z, z, z, lng, lnb, ws, bst)


def _dft_consts(n):
    k = np.arange(n, dtype=np.int64)
    ang = 2.0 * np.pi * ((k[:, None] * k[None, :]) % n).astype(np.float64) / n
    csn = np.concatenate([np.cos(ang), np.sin(ang)], axis=0)
    c = np.arange(F_GW, dtype=np.int64)
    angc = 2.0 * np.pi * ((c[:, None] * c[None, :]) % F_GW).astype(np.float64) / F_GW
    return tuple(jnp.asarray(t, dtype=F32).astype(BF16) for t in (csn, np.cos(angc), np.sin(angc)))


def _fourier_kernel(f_ref, gf_ref, csn_ref, cc_ref, sc_ref, o_ref):
    n = f_ref.shape[0]
    t = _dot(csn_ref[...], f_ref[...].astype(BF16))
    y = _dot(t[:n].astype(BF16), cc_ref[...]) - _dot(t[n:].astype(BF16), sc_ref[...])
    o_ref[...] = (y * ((n * F_GW) ** -0.5) * gf_ref[...].astype(F32)).astype(o_ref.dtype)


def _fourier(z, n, consts):
    T = z.shape[0]
    csn, cc, sc = consts
    const = lambda shape: pl.BlockSpec(shape, lambda b, g: (0, 0), pipeline_mode=pl.Buffered(1))
    return pl.pallas_call(
        _fourier_kernel,
        grid=(T // n, F_GROUPS),
        in_specs=[
            pl.BlockSpec((n, F_GW), lambda b, g: (b, OFF_F // F_GW + g)),
            pl.BlockSpec((n, F_GW), lambda b, g: (b, OFF_GF // F_GW + g)),
            const((2 * n, n)), const((F_GW, F_GW)), const((F_GW, F_GW)),
        ],
        out_specs=pl.BlockSpec((n, F_GW), lambda b, g: (b, g)),
        out_shape=jax.ShapeDtypeStruct((T, W_B), BF16),
        compiler_params=_params("arbitrary", "arbitrary"),
        name="fourier",
    )(z, z, csn, cc, sc)


def _head_rms(t, g):
    t = t.astype(F32)
    return t * lax.rsqrt(jnp.mean(t * t, axis=-1, keepdims=True) + EPS) * g


def _block_geometry(rows):
    geo = []
    nblk = rows // Q_ROWS
    for blk in range(nblk):
        base = min(max(Q_ROWS * blk - NA_KH // 2, 0), rows - WIN_ROWS)
        typ = 0 if blk == 0 else (2 if blk == nblk - 1 else 1)
        geo.append((base, typ))
    return geo


def _build_bias_tables(rpb_ref, tab_scr, head, geo, rows):
    n_dr, n_dc = 2 * NA_KH - 1, 2 * NA_KW - 1
    shape = (GRID_W, 2 * GRID_W)
    lane = lax.broadcasted_iota(jnp.int32, shape, 1)
    cq = lax.broadcasted_iota(jnp.int32, shape, 0)
    kc = lane & (GRID_W - 1)
    dc = kc - cq + (NA_KW - 1)
    cs = jnp.clip(cq - NA_KW // 2, 0, GRID_W - NA_KW)
    col_ok = (kc >= cs) & (kc < cs + NA_KW)
    neg = jnp.full(shape, NEG, F32)
    row_bias = []
    for d in range(n_dr):
        t = neg
        for b in range(n_dc):
            t = jnp.where(dc == b, rpb_ref[(head * n_dr + d) * n_dc + b], t)
        row_bias.append(jnp.where(col_ok, t, NEG))
    nblk = len(geo)
    for typ, blk in enumerate((0, 1, nblk - 1)):
        base = geo[blk][0]
        for i in range(Q_ROWS):
            r = Q_ROWS * blk + i
            rs = min(max(r - NA_KH // 2, 0), rows - NA_KH)
            for wp in range(WIN_ROWS // 2):
                halves = []
                for kr in (base + 2 * wp, base + 2 * wp + 1):
                    halves.append(row_bias[kr - r + NA_KH - 1] if rs <= kr < rs + NA_KH else neg)
                tab_scr[typ, i * GRID_W:(i + 1) * GRID_W, wp * 2 * GRID_W:(wp + 1) * 2 * GRID_W] = (
                    jnp.where(lane < GRID_W, halves[0], halves[1]))


def _nbr_attn_kernel(rpb_ref, q_ref, k_ref, v_ref, gn_ref, kc_ref, vc_ref, qg_ref, kg_ref, o_ref,
                     qn_scr, kn_scr, kcn_scr, vcb_scr, tab_ref, *, geo, rows):
    @pl.when(pl.program_id(1) == 0)
    def _():
        _build_bias_tables(rpb_ref, tab_ref, pl.program_id(0), geo, rows)

    qn_scr[...] = (_head_rms(q_ref[...], qg_ref[...]) * ATT_SCALE).astype(BF16)
    kn_scr[...] = _head_rms(k_ref[...], kg_ref[...]).astype(BF16)
    kcn_scr[...] = _head_rms(kc_ref[...], kg_ref[...]).astype(BF16)
    vcb_scr[...] = vc_ref[...].astype(BF16)
    nq = Q_ROWS * GRID_W
    nk = WIN_ROWS * GRID_W
    for blk, (base, typ) in enumerate(geo):
        qs = slice(blk * nq, (blk + 1) * nq)
        ks = slice(base * GRID_W, base * GRID_W + nk)
        q = qn_scr[qs, :]
        s_loc = _dot_nt(q, kn_scr[ks, :]) + tab_ref[typ]
        s_ctx = _dot_nt(q, kcn_scr[...])
        m = jnp.maximum(jnp.max(s_loc, axis=-1, keepdims=True), jnp.max(s_ctx, axis=-1, keepdims=True))
        p_loc = jnp.exp(s_loc - m)
        p_ctx = jnp.exp(s_ctx - m)
        den = jnp.sum(p_loc, axis=-1, keepdims=True) + jnp.sum(p_ctx, axis=-1, keepdims=True)
        o = _dot(p_loc.astype(BF16), v_ref[ks, :].astype(BF16)) + _dot(p_ctx.astype(BF16), vcb_scr[...])
        o_ref[qs, :] = (o / den * gn_ref[qs, :].astype(F32)).astype(o_ref.dtype)


def _nbr_attention(z, zc, ck, cv, qg, kg, rpb_flat, seq):
    T = z.shape[0]
    rows = seq // GRID_W
    geo = _block_geometry(rows)
    hd = HEAD_DIM
    lat = lambda off: pl.BlockSpec((seq, hd), lambda h, b: (b, off // hd + h))
    return pl.pallas_call(
        functools.partial(_nbr_attn_kernel, geo=geo, rows=rows),
        grid=(NA_HEADS, T // seq),
        in_specs=[
            pl.BlockSpec(memory_space=pltpu.SMEM),
            lat(OFF_Q), lat(OFF_K), lat(OFF_V), lat(OFF_GN),
            pl.BlockSpec((CTX_LEN, hd), lambda h, b: (b, ck + h)),
            pl.BlockSpec((CTX_LEN, hd), lambda h, b: (b, cv + h)),
            pl.BlockSpec((1, hd), lambda h, b: (0, 0)),
            pl.BlockSpec((1, hd), lambda h, b: (0, 0)),
        ],
        out_specs=pl.BlockSpec((seq, hd), lambda h, b: (b, h)),
        out_shape=jax.ShapeDtypeStruct((T, W_C), BF16),
        scratch_shapes=[pltpu.VMEM((seq, hd), BF16), pltpu.VMEM((seq, hd), BF16),
                        pltpu.VMEM((CTX_LEN, hd), BF16), pltpu.VMEM((CTX_LEN, hd), BF16),
                        pltpu.VMEM((3, Q_ROWS * GRID_W, WIN_ROWS * GRID_W), F32)],
        compiler_params=_params("arbitrary", "arbitrary"),
        name="nbr_attn",
    )(rpb_flat, z, z, z, z, zc, zc, qg, kg)


def _ctx_attn_kernel(q_ref, k_ref, v_ref, gn_ref, qg_ref, kg_ref, o_ref):
    q = (_head_rms(q_ref[...], qg_ref[...]) * ATT_SCALE).astype(BF16)
    k = _head_rms(k_ref[...], kg_ref[...]).astype(BF16)
    s = _dot_nt(q, k)
    p = jnp.exp(s - jnp.max(s, axis=-1, keepdims=True))
    o = _dot(p.astype(BF16), v_ref[...].astype(BF16)) / jnp.sum(p, axis=-1, keepdims=True)
    o_ref[...] = (o * gn_ref[...].astype(F32)).astype(o_ref.dtype)


def _ctx_attention(zc, qg, kg):
    T = zc.shape[0]
    hd = HEAD_DIM
    blk = lambda off: pl.BlockSpec((CTX_LEN, hd), lambda b, h: (b, off // hd + h))
    vec = pl.BlockSpec((1, hd), lambda b, h: (0, 0))
    return pl.pallas_call(
        _ctx_attn_kernel,
        grid=(T // CTX_LEN, NA_HEADS),
        in_specs=[blk(OFF_Q), blk(OFF_K), blk(OFF_V), blk(OFF_GN), vec, vec],
        out_specs=pl.BlockSpec((CTX_LEN, hd), lambda b, h: (b, h)),
        out_shape=jax.ShapeDtypeStruct((T, W_C), BF16),
        compiler_params=_params("arbitrary", "arbitrary"),
        name="ctx_attn",
    )(zc, zc, zc, zc, qg, kg)


def _merge_kernel(a_ref, f_ref, n_ref, ga0, ga1, gf0, gf1, gn0, gn1, x_ref, gate_ref,
                  wpa_ref, wpf_ref, wpn_ref, wout_ref, o_ref, y_scr):
    a = a_ref[...]
    f = f_ref[...]
    n = n_ref[...]
    for half, (ga, gf, gn) in enumerate(((ga0, gf0, gn0), (ga1, gf1, gn1))):
        cols = slice(half * COL, (half + 1) * COL)
        y = (ga[...].astype(F32) * _dot(a, wpa_ref[:, cols])
             + gf[...].astype(F32) * _dot(f, wpf_ref[:, cols])
             + gn[...].astype(F32) * _dot(n, wpn_ref[:, cols]))
        y_scr[:, cols] = y.astype(BF16)
    o_ref[...] = x_ref[...] + gate_ref[0] * _dot(y_scr[...], wout_ref[...])


def _merge(a, f, n, z, x2, mod3, wpa, wpf, wpn, wout, *, rows_per_mod, mod_row0):
    T, D = x2.shape
    tm = 256
    mrow = lambda i: mod_row0 + i // rows_per_mod
    act = pl.BlockSpec((tm, COL), lambda i: (i, 0))
    gate_blk = lambda c: pl.BlockSpec((tm, COL), lambda i: (i, OFF_MERGE // COL + c))
    const = lambda shape: pl.BlockSpec(shape, lambda i: (0, 0), pipeline_mode=pl.Buffered(1))
    return pl.pallas_call(
        _merge_kernel,
        grid=(T // tm,),
        in_specs=[
            act, act, act,
            gate_blk(0), gate_blk(1), gate_blk(2), gate_blk(3), gate_blk(4), gate_blk(5),
            pl.BlockSpec((tm, D), lambda i: (i, 0)),
            pl.BlockSpec((1, 1, D), lambda i: (mrow(i), 0, 2)),
            const((W_A, D)), const((W_B, D)), const((W_C, D)), const((D, D)),
        ],
        out_specs=pl.BlockSpec((tm, D), lambda i: (i, 0)),
        out_shape=jax.ShapeDtypeStruct((T, D), F32),
        scratch_shapes=[pltpu.VMEM((tm, D), BF16)],
        compiler_params=_params("arbitrary"),
        name="merge_out",
    )(a, f, n, z, z, z, z, z, z, x2, mod3, wpa, wpf, wpn, wout)


def kernel(x, c, ctx, c_ctx, norm_g, w_ada, b_ada, w_in, gmlp_ln_g, gmlp_ln_b, gmlp_ws, gmlp_bs,
           q_norm_g, k_norm_g, rpb, w_pa, w_pf, w_pn, w_out):
    B, S, D = x.shape
    C = ctx.shape[1]
    L = w_in.shape[0]
    assert D == D_MODEL and C == CTX_LEN and B < MOD_ROWS
    assert S % (GRID_W * Q_ROWS) == 0 and S // GRID_W >= WIN_ROWS

    cc = jnp.concatenate([c, c_ctx[None], jnp.zeros((MOD_ROWS - B - 1, D), F32)], axis=0)
    mod = _modulation(cc, w_ada, b_ada)

    dft_lat = _dft_consts(S)
    dft_ctx = _dft_consts(C)
    x2 = x.reshape(B * S, D)
    c2 = ctx.reshape(B * C, D)
    for l in range(L):
        last = l == L - 1
        mod3 = mod[l].reshape(MOD_ROWS, 1, 3 * D)
        g = norm_g[l].reshape(1, D)
        w = w_in[l].astype(BF16)
        wpa, wpf, wpn, wout = (t[l].astype(BF16) for t in (w_pa, w_pf, w_pn, w_out))
        lng = gmlp_ln_g[l].reshape(1, W_A)
        lnb = gmlp_ln_b[l].reshape(1, W_A)
        ws = gmlp_ws[l].astype(BF16)
        bst = gmlp_bs[l].T
        qg = q_norm_g[l].reshape(1, HEAD_DIM)
        kg = k_norm_g[l].reshape(1, HEAD_DIM)
        rpb_flat = rpb[l].reshape(-1)

        if last:
            zc = _in_proj(c2, g, mod3, w, rows_per_mod=1 << 30, mod_row0=B,
                          col0=OFF_K // COL, ncol=2)
            ck, cv = 0, W_C // HEAD_DIM
        else:
            zc = _in_proj(c2, g, mod3, w, rows_per_mod=1 << 30, mod_row0=B, col0=0, ncol=W_IN // COL)
            ck, cv = OFF_K // HEAD_DIM, OFF_V // HEAD_DIM
        z = _in_proj(x2, g, mod3, w, rows_per_mod=S // 1024, mod_row0=0, col0=0, ncol=W_IN // COL)

        n = _nbr_attention(z, zc, ck, cv, qg, kg, rpb_flat, S)
        a = _gmlp(z, lng, lnb, ws, bst)
        f = _fourier(z, S, dft_lat)
        x_new = _merge(a, f, n, z, x2, mod3, wpa, wpf, wpn, wout, rows_per_mod=S // 256, mod_row0=0)

        if not last:
            nc = _ctx_attention(zc, qg, kg)
            ac = _gmlp(zc, lng, lnb, ws, bst)
            fc = _fourier(zc, C, dft_ctx)
            c2 = _merge(ac, fc, nc, zc, c2, mod3, wpa, wpf, wpn, wout,
                        rows_per_mod=1 << 30, mod_row0=B)
        x2 = x_new
    return x2.reshape(B, S, D)
```

```python
import functools

import jax
import jax.numpy as jnp
import numpy as np
from jax import lax
from jax.experimental import pallas as pl
from jax.experimental.pallas import tpu as pltpu

F32 = jnp.float32
BF16 = jnp.bfloat16

D_MODEL = 2048
CTX_LEN = 256
GRID_W = 64
EPS = 1e-6
CHUNK = 128
GMLP_GROUPS = 8
W_A = D_MODEL // 2
W_B = D_MODEL // 2
F_GROUPS = 4
F_GW = W_B // F_GROUPS
HEAD_DIM = 128
NA_HEADS = 8
W_C = NA_HEADS * HEAD_DIM
NA_KH = 8
NA_KW = 16

OFF_U = 0
OFF_VG = OFF_U + W_A
OFF_GA = OFF_VG + W_A
OFF_F = OFF_GA + W_A
OFF_GF = OFF_F + W_B
OFF_Q = OFF_GF + W_B
OFF_K = OFF_Q + W_C
OFF_V = OFF_K + W_C
OFF_GN = OFF_V + W_C
OFF_MERGE = OFF_GN + W_C
W_IN = OFF_MERGE + 3 * D_MODEL

COL = 1024
MOD_ROWS = 16
VMEM_LIMIT = 56 * 1024 * 1024
Z_DTYPE = jnp.bfloat16

Q_ROWS = 4
WIN_ROWS = 12
NEG = -1e30
ATT_SCALE = HEAD_DIM ** -0.5


def _sigmoid(t):
    return 1.0 / (1.0 + jnp.exp(-t))


def _silu(t):
    return t * _sigmoid(t)


def _gelu(t):
    return 0.5 * t * (1.0 + jnp.tanh(np.sqrt(2.0 / np.pi) * (t + 0.044715 * (t * t * t))))


def _dot(a, b):
    return jnp.dot(a, b, preferred_element_type=F32)


def _dot_nt(a, b):
    return lax.dot_general(a, b, (((1,), (1,)), ((), ())), preferred_element_type=F32)


def _params(*sem):
    return pltpu.CompilerParams(dimension_semantics=sem, vmem_limit_bytes=VMEM_LIMIT)


def _mod_kernel(cc_ref, w_ref, b_ref, o_ref):
    s = _silu(cc_ref[...])
    s_hi = s.astype(BF16)
    s_lo = (s - s_hi.astype(F32)).astype(BF16)
    w = w_ref[0]
    w_hi = w.astype(BF16)
    w_lo = (w - w_hi.astype(F32)).astype(BF16)
    o_ref[0] = _dot(s_hi, w_hi) + _dot(s_lo, w_hi) + _dot(s_hi, w_lo) + b_ref[0]


def _modulation(cc, w_ada, b_ada):
    L, D, N = w_ada.shape
    tn = 1024
    return pl.pallas_call(
        _mod_kernel,
        grid=(L, N // tn),
        in_specs=[
            pl.BlockSpec((MOD_ROWS, D), lambda l, j: (0, 0)),
            pl.BlockSpec((1, D, tn), lambda l, j: (l, 0, j)),
            pl.BlockSpec((1, 1, tn), lambda l, j: (l, 0, j)),
        ],
        out_specs=pl.BlockSpec((1, MOD_ROWS, tn), lambda l, j: (l, 0, j)),
        out_shape=jax.ShapeDtypeStruct((L, MOD_ROWS, N), F32),
        compiler_params=_params("arbitrary", "arbitrary"),
        name="adaln_mod",
    )(cc, w_ada, b_ada.reshape(L, 1, N))


def _in_proj_kernel(x_ref, g_ref, sh_ref, sc_ref, w_ref, o_ref, h_scr, *, col0):
    j = pl.program_id(1)

    @pl.when(j == 0)
    def _():
        x = x_ref[...]
        y = x * lax.rsqrt(jnp.mean(x * x, axis=-1, keepdims=True) + EPS) * g_ref[...]
        h_scr[...] = (y * (1.0 + sc_ref[0]) + sh_ref[0]).astype(BF16)

    c = j + col0

    def project(act):
        o_ref[...] = act(_dot(h_scr[...], w_ref[0].astype(BF16))).astype(o_ref.dtype)

    @pl.when(c < OFF_GA // COL)
    def _():
        project(_gelu)

    @pl.when((c == OFF_GA // COL) | (c == OFF_GF // COL) | (c == OFF_GN // COL))
    def _():
        project(_silu)

    @pl.when(c >= OFF_MERGE // COL)
    def _():
        project(_sigmoid)

    @pl.when((c == OFF_F // COL) | ((c >= OFF_Q // COL) & (c < OFF_GN // COL)))
    def _():
        project(lambda z: z)


def _in_proj(x2, g, mod3, w_in, layer, *, rows_per_mod, mod_row0, col0, ncol):
    T, D = x2.shape
    tm = min(1024, T)
    mrow = lambda i: mod_row0 + i // rows_per_mod
    return pl.pallas_call(
        functools.partial(_in_proj_kernel, col0=col0),
        grid=(T // tm, ncol),
        in_specs=[
            pl.BlockSpec((tm, D), lambda i, j: (i, 0), pipeline_mode=pl.Buffered(1)),
            pl.BlockSpec((1, D), lambda i, j: (0, 0)),
            pl.BlockSpec((1, 1, D), lambda i, j: (mrow(i), 0, 0)),
            pl.BlockSpec((1, 1, D), lambda i, j: (mrow(i), 0, 1)),
            pl.BlockSpec((1, D, COL), lambda i, j: (layer, 0, j + col0)),
        ],
        out_specs=pl.BlockSpec((tm, COL), lambda i, j: (i, j)),
        out_shape=jax.ShapeDtypeStruct((T, ncol * COL), Z_DTYPE),
        scratch_shapes=[pltpu.VMEM((tm, D), BF16)],
        compiler_params=_params("arbitrary", "arbitrary"),
        name="in_proj",
    )(x2, g, mod3, mod3, w_in)


def _gmlp_kernel(u_ref, vg_ref, ga_ref, lng_ref, lnb_ref, ws_ref, bst_ref, o_ref, vn_scr):
    v = vg_ref[...].astype(F32)
    mu = jnp.mean(v, axis=-1, keepdims=True)
    vc = v - mu
    vn = vc * lax.rsqrt(jnp.mean(vc * vc, axis=-1, keepdims=True) + EPS)
    vn_scr[...] = (vn * lng_ref[...] + lnb_ref[...]).astype(BF16)
    tm = u_ref.shape[0]
    for ch in range(tm // CHUNK):
        rows = slice(ch * CHUNK, (ch + 1) * CHUNK)
        for g in range(GMLP_GROUPS):
            cols = slice(g * CHUNK, (g + 1) * CHUNK)
            sv = _dot(ws_ref[g], vn_scr[rows, cols]) + bst_ref[:, g:g + 1]
            a = u_ref[rows, cols].astype(F32) * sv * ga_ref[rows, cols].astype(F32)
            o_ref[rows, cols] = a.astype(o_ref.dtype)


def _gmlp(z, lng, lnb, ws, bst):
    T = z.shape[0]
    tm = 512
    blk = lambda c: pl.BlockSpec((tm, W_A), lambda i: (i, c))
    return pl.pallas_call(
        _gmlp_kernel,
        grid=(T // tm,),
        in_specs=[
            blk(OFF_U // W_A), blk(OFF_VG // W_A), blk(OFF_GA // W_A),
            pl.BlockSpec((1, W_A), lambda i: (0, 0)),
            pl.BlockSpec((1, W_A), lambda i: (0, 0)),
            pl.BlockSpec((GMLP_GROUPS, CHUNK, CHUNK), lambda i: (0, 0, 0)),
            pl.BlockSpec((CHUNK, GMLP_GROUPS), lambda i: (0, 0)),
        ],
        out_specs=pl.BlockSpec((tm, W_A), lambda i: (i, 0)),
        out_shape=jax.ShapeDtypeStruct((T, W_A), BF16),
        scratch_shapes=[pltpu.VMEM((tm, W_A), BF16)],
        compiler_params=_params("arbitrary"),
        name="gmlp",
    )(z, z, z, lng, lnb, ws, bst)


def _dft_consts(n):
    k = np.arange(n, dtype=np.int64)
    ang = 2.0 * np.pi * ((k[:, None] * k[None, :]) % n).astype(np.float64) / n
    csn = np.concatenate([np.cos(ang), np.sin(ang)], axis=0)
    c = np.arange(F_GW, dtype=np.int64)
    angc = 2.0 * np.pi * ((c[:, None] * c[None, :]) % F_GW).astype(np.float64) / F_GW
    return tuple(jnp.asarray(t, dtype=F32).astype(BF16) for t in (csn, np.cos(angc), np.sin(angc)))


def _fourier_kernel(f_ref, gf_ref, csn_ref, cc_ref, sc_ref, o_ref):
    n = f_ref.shape[0]
    t = _dot(csn_ref[...], f_ref[...].astype(BF16))
    y = _dot(t[:n].astype(BF16), cc_ref[...]) - _dot(t[n:].astype(BF16), sc_ref[...])
    o_ref[...] = (y * ((n * F_GW) ** -0.5) * gf_ref[...].astype(F32)).astype(o_ref.dtype)


def _fourier(z, n, consts):
    T = z.shape[0]
    csn, cc, sc = consts
    const = lambda shape: pl.BlockSpec(shape, lambda b, g: (0, 0), pipeline_mode=pl.Buffered(1))
    return pl.pallas_call(
        _fourier_kernel,
        grid=(T // n, F_GROUPS),
        in_specs=[
            pl.BlockSpec((n, F_GW), lambda b, g: (b, OFF_F // F_GW + g)),
            pl.BlockSpec((n, F_GW), lambda b, g: (b, OFF_GF // F_GW + g)),
            const((2 * n, n)), const((F_GW, F_GW)), const((F_GW, F_GW)),
        ],
        out_specs=pl.BlockSpec((n, F_GW), lambda b, g: (b, g)),
        out_shape=jax.ShapeDtypeStruct((T, W_B), BF16),
        compiler_params=_params("arbitrary", "arbitrary"),
        name="fourier",
    )(z, z, csn, cc, sc)


def _head_rms(t, g):
    t = t.astype(F32)
    return t * lax.rsqrt(jnp.mean(t * t, axis=-1, keepdims=True) + EPS) * g


def _block_geometry(rows):
    geo = []
    nblk = rows // Q_ROWS
    for blk in range(nblk):
        base = min(max(Q_ROWS * blk - NA_KH // 2, 0), rows - WIN_ROWS)
        typ = 0 if blk == 0 else (2 if blk == nblk - 1 else 1)
        geo.append((base, typ))
    return geo


def _build_bias_tables(rpb_ref, tab_scr, head, geo, rows):
    n_dr, n_dc = 2 * NA_KH - 1, 2 * NA_KW - 1
    shape = (GRID_W, 2 * GRID_W)
    lane = lax.broadcasted_iota(jnp.int32, shape, 1)
    cq = lax.broadcasted_iota(jnp.int32, shape, 0)
    kc = lane & (GRID_W - 1)
    dc = kc - cq + (NA_KW - 1)
    cs = jnp.clip(cq - NA_KW // 2, 0, GRID_W - NA_KW)
    col_ok = (kc >= cs) & (kc < cs + NA_KW)
    neg = jnp.full(shape, NEG, F32)
    row_bias = []
    for d in range(n_dr):
        t = neg
        for b in range(n_dc):
            t = jnp.where(dc == b, rpb_ref[(head * n_dr + d) * n_dc + b], t)
        row_bias.append(jnp.where(col_ok, t, NEG))
    nblk = len(geo)
    for typ, blk in enumerate((0, 1, nblk - 1)):
        base = geo[blk][0]
        for i in range(Q_ROWS):
            r = Q_ROWS * blk + i
            rs = min(max(r - NA_KH // 2, 0), rows - NA_KH)
            for wp in range(WIN_ROWS // 2):
                halves = []
                for kr in (base + 2 * wp, base + 2 * wp + 1):
                    halves.append(row_bias[kr - r + NA_KH - 1] if rs <= kr < rs + NA_KH else neg)
                tab_scr[typ, i * GRID_W:(i + 1) * GRID_W, wp * 2 * GRID_W:(wp + 1) * 2 * GRID_W] = (
                    jnp.where(lane < GRID_W, halves[0], halves[1]))


def _nbr_attn_kernel(rpb_ref, q_ref, k_ref, v_ref, gn_ref, kc_ref, vc_ref, qg_ref, kg_ref, o_ref,
                     qn_scr, kn_scr, kcn_scr, vcb_scr, tab_ref, *, geo, rows):
    @pl.when(pl.program_id(1) == 0)
    def _():
        _build_bias_tables(rpb_ref, tab_ref, pl.program_id(0), geo, rows)

    qn_scr[...] = (_head_rms(q_ref[...], qg_ref[...]) * ATT_SCALE).astype(BF16)
    kn_scr[...] = _head_rms(k_ref[...], kg_ref[...]).astype(BF16)
    kcn_scr[...] = _head_rms(kc_ref[...], kg_ref[...]).astype(BF16)
    vcb_scr[...] = vc_ref[...].astype(BF16)
    nq = Q_ROWS * GRID_W
    nk = WIN_ROWS * GRID_W
    for blk, (base, typ) in enumerate(geo):
        qs = slice(blk * nq, (blk + 1) * nq)
        ks = slice(base * GRID_W, base * GRID_W + nk)
        q = qn_scr[qs, :]
        s_loc = _dot_nt(q, kn_scr[ks, :]) + tab_ref[typ]
        s_ctx = _dot_nt(q, kcn_scr[...])
        m = jnp.maximum(jnp.max(s_loc, axis=-1, keepdims=True), jnp.max(s_ctx, axis=-1, keepdims=True))
        p_loc = jnp.exp(s_loc - m)
        p_ctx = jnp.exp(s_ctx - m)
        den = jnp.sum(p_loc, axis=-1, keepdims=True) + jnp.sum(p_ctx, axis=-1, keepdims=True)
        o = _dot(p_loc.astype(BF16), v_ref[ks, :].astype(BF16)) + _dot(p_ctx.astype(BF16), vcb_scr[...])
        o_ref[qs, :] = (o / den * gn_ref[qs, :].astype(F32)).astype(o_ref.dtype)


def _nbr_attention(z, zc, ck, cv, qg, kg, rpb_flat, seq):
    T = z.shape[0]
    rows = seq // GRID_W
    geo = _block_geometry(rows)
    hd = HEAD_DIM
    lat = lambda off: pl.BlockSpec((seq, hd), lambda h, b: (b, off // hd + h))
    return pl.pallas_call(
        functools.partial(_nbr_attn_kernel, geo=geo, rows=rows),
        grid=(NA_HEADS, T // seq),
        in_specs=[
            pl.BlockSpec(memory_space=pltpu.SMEM),
            lat(OFF_Q), lat(OFF_K), lat(OFF_V), lat(OFF_GN),
            pl.BlockSpec((CTX_LEN, hd), lambda h, b: (b, ck + h)),
            pl.BlockSpec((CTX_LEN, hd), lambda h, b: (b, cv + h)),
            pl.BlockSpec((1, hd), lambda h, b: (0, 0)),
            pl.BlockSpec((1, hd), lambda h, b: (0, 0)),
        ],
        out_specs=pl.BlockSpec((seq, hd), lambda h, b: (b, h)),
        out_shape=jax.ShapeDtypeStruct((T, W_C), BF16),
        scratch_shapes=[pltpu.VMEM((seq, hd), BF16), pltpu.VMEM((seq, hd), BF16),
                        pltpu.VMEM((CTX_LEN, hd), BF16), pltpu.VMEM((CTX_LEN, hd), BF16),
                        pltpu.VMEM((3, Q_ROWS * GRID_W, WIN_ROWS * GRID_W), F32)],
        compiler_params=_params("arbitrary", "arbitrary"),
        name="nbr_attn",
    )(rpb_flat, z, z, z, z, zc, zc, qg, kg)


def _ctx_attn_kernel(q_ref, k_ref, v_ref, gn_ref, qg_ref, kg_ref, o_ref):
    q = (_head_rms(q_ref[...], qg_ref[...]) * ATT_SCALE).astype(BF16)
    k = _head_rms(k_ref[...], kg_ref[...]).astype(BF16)
    s = _dot_nt(q, k)
    p = jnp.exp(s - jnp.max(s, axis=-1, keepdims=True))
    o = _dot(p.astype(BF16), v_ref[...].astype(BF16)) / jnp.sum(p, axis=-1, keepdims=True)
    o_ref[...] = (o * gn_ref[...].astype(F32)).astype(o_ref.dtype)


def _ctx_attention(zc, qg, kg):
    T = zc.shape[0]
    hd = HEAD_DIM
    blk = lambda off: pl.BlockSpec((CTX_LEN, hd), lambda b, h: (b, off // hd + h))
    vec = pl.BlockSpec((1, hd), lambda b, h: (0, 0))
    return pl.pallas_call(
        _ctx_attn_kernel,
        grid=(T // CTX_LEN, NA_HEADS),
        in_specs=[blk(OFF_Q), blk(OFF_K), blk(OFF_V), blk(OFF_GN), vec, vec],
        out_specs=pl.BlockSpec((CTX_LEN, hd), lambda b, h: (b, h)),
        out_shape=jax.ShapeDtypeStruct((T, W_C), BF16),
        compiler_params=_params("arbitrary", "arbitrary"),
        name="ctx_attn",
    )(zc, zc, zc, zc, qg, kg)


def _merge_kernel(a_ref, f_ref, n_ref, ga0, ga1, gf0, gf1, gn0, gn1, x_ref, gate_ref,
                  wpa_ref, wpf_ref, wpn_ref, wout_ref, o_ref, y_scr):
    a = a_ref[...]
    f = f_ref[...]
    n = n_ref[...]
    for half, (ga, gf, gn) in enumerate(((ga0, gf0, gn0), (ga1, gf1, gn1))):
        cols = slice(half * COL, (half + 1) * COL)
        y = (ga[...].astype(F32) * _dot(a, wpa_ref[:, cols])
             + gf[...].astype(F32) * _dot(f, wpf_ref[:, cols])
             + gn[...].astype(F32) * _dot(n, wpn_ref[:, cols]))
        y_scr[:, cols] = y.astype(BF16)
    o_ref[...] = x_ref[...] + gate_ref[0] * _dot(y_scr[...], wout_ref[...])


def _merge(a, f, n, z, x2, mod3, wpa, wpf, wpn, wout, *, rows_per_mod, mod_row0):
    T, D = x2.shape
    tm = 256
    mrow = lambda i: mod_row0 + i // rows_per_mod
    act = pl.BlockSpec((tm, COL), lambda i: (i, 0))
    gate_blk = lambda c: pl.BlockSpec((tm, COL), lambda i: (i, OFF_MERGE // COL + c))
    const = lambda shape: pl.BlockSpec(shape, lambda i: (0, 0), pipeline_mode=pl.Buffered(1))
    return pl.pallas_call(
        _merge_kernel,
        grid=(T // tm,),
        in_specs=[
            act, act, act,
            gate_blk(0), gate_blk(1), gate_blk(2), gate_blk(3), gate_blk(4), gate_blk(5),
            pl.BlockSpec((tm, D), lambda i: (i, 0)),
            pl.BlockSpec((1, 1, D), lambda i: (mrow(i), 0, 2)),
            const((W_A, D)), const((W_B, D)), const((W_C, D)), const((D, D)),
        ],
        out_specs=pl.BlockSpec((tm, D), lambda i: (i, 0)),
        out_shape=jax.ShapeDtypeStruct((T, D), F32),
        scratch_shapes=[pltpu.VMEM((tm, D), BF16)],
        compiler_params=_params("arbitrary"),
        name="merge_out",
    )(a, f, n, z, z, z, z, z, z, x2, mod3, wpa, wpf, wpn, wout)


def kernel(x, c, ctx, c_ctx, norm_g, w_ada, b_ada, w_in, gmlp_ln_g, gmlp_ln_b, gmlp_ws, gmlp_bs,
           q_norm_g, k_norm_g, rpb, w_pa, w_pf, w_pn, w_out):
    B, S, D = x.shape
    C = ctx.shape[1]
    L = w_in.shape[0]
    assert D == D_MODEL and C == CTX_LEN and B < MOD_ROWS
    assert S % (GRID_W * Q_ROWS) == 0 and S // GRID_W >= WIN_ROWS

    cc = jnp.concatenate([c, c_ctx[None], jnp.zeros((MOD_ROWS - B - 1, D), F32)], axis=0)
    mod = _modulation(cc, w_ada, b_ada)

    dft_lat = _dft_consts(S)
    dft_ctx = _dft_consts(C)
    x2 = x.reshape(B * S, D)
    c2 = ctx.reshape(B * C, D)
    for l in range(L):
        last = l == L - 1
        mod3 = mod[l].reshape(MOD_ROWS, 1, 3 * D)
        g = norm_g[l].reshape(1, D)
        wpa, wpf, wpn, wout = (t[l].astype(BF16) for t in (w_pa, w_pf, w_pn, w_out))
        lng = gmlp_ln_g[l].reshape(1, W_A)
        lnb = gmlp_ln_b[l].reshape(1, W_A)
        ws = gmlp_ws[l].astype(BF16)
        bst = gmlp_bs[l].T
        qg = q_norm_g[l].reshape(1, HEAD_DIM)
        kg = k_norm_g[l].reshape(1, HEAD_DIM)
        rpb_flat = rpb[l].reshape(-1)

        if last:
            zc = _in_proj(c2, g, mod3, w_in, l, rows_per_mod=1 << 30, mod_row0=B,
                          col0=OFF_K // COL, ncol=2)
            ck, cv = 0, W_C // HEAD_DIM
        else:
            zc = _in_proj(c2, g, mod3, w_in, l, rows_per_mod=1 << 30, mod_row0=B, col0=0,
                          ncol=W_IN // COL)
            ck, cv = OFF_K // HEAD_DIM, OFF_V // HEAD_DIM
        z = _in_proj(x2, g, mod3, w_in, l, rows_per_mod=S // 1024, mod_row0=0, col0=0,
                     ncol=W_IN // COL)

        n = _nbr_attention(z, zc, ck, cv, qg, kg, rpb_flat, S)
        a = _gmlp(z, lng, lnb, ws, bst)
        f = _fourier(z, S, dft_lat)
        x_new = _merge(a, f, n, z, x2, mod3, wpa, wpf, wpn, wout, rows_per_mod=S // 256, mod_row0=0)

        if not last:
            nc = _ctx_attention(zc, qg, kg)
            ac = _gmlp(zc, lng, lnb, ws, bst)
            fc = _fourier(zc, C, dft_ctx)
            c2 = _merge(ac, fc, nc, zc, c2, mod3, wpa, wpf, wpn, wout,
                        rows_per_mod=1 << 30, mod_row0=B)
        x2 = x_new
    return x2.reshape(B, S, D)
```

```python
import functools

import jax
import jax.numpy as jnp
import numpy as np
from jax import lax
from jax.experimental import pallas as pl
from jax.experimental.pallas import tpu as pltpu

F32 = jnp.float32
BF16 = jnp.bfloat16

D_MODEL = 2048
CTX_LEN = 256
GRID_W = 64
EPS = 1e-6
CHUNK = 128
GMLP_GROUPS = 8
W_A = D_MODEL // 2
W_B = D_MODEL // 2
F_GROUPS = 4
F_GW = W_B // F_GROUPS
HEAD_DIM = 128
NA_HEADS = 8
W_C = NA_HEADS * HEAD_DIM
NA_KH = 8
NA_KW = 16

OFF_U = 0
OFF_VG = OFF_U + W_A
OFF_GA = OFF_VG + W_A
OFF_F = OFF_GA + W_A
OFF_GF = OFF_F + W_B
OFF_Q = OFF_GF + W_B
OFF_K = OFF_Q + W_C
OFF_V = OFF_K + W_C
OFF_GN = OFF_V + W_C
OFF_MERGE = OFF_GN + W_C
W_IN = OFF_MERGE + 3 * D_MODEL

COL = 1024
MOD_ROWS = 16
VMEM_LIMIT = 56 * 1024 * 1024
Z_DTYPE = jnp.bfloat16

Q_ROWS = 4
WIN_ROWS = 12
NEG = -1e30
LOG2E = float(np.log2(np.e))
ATT_SCALE = HEAD_DIM ** -0.5 * LOG2E


GELU_C1 = float(np.sqrt(2.0 / np.pi))
GELU_C2 = GELU_C1 * 0.044715


def _sigmoid(t):
    return 0.5 * jnp.tanh(0.5 * t) + 0.5


def _silu(t):
    h = 0.5 * t
    return h * jnp.tanh(h) + h


def _gelu(t):
    h = 0.5 * t
    return h * jnp.tanh(t * (GELU_C1 + GELU_C2 * (t * t))) + h


def _dot(a, b):
    return jnp.dot(a, b, preferred_element_type=F32)


def _dot_nt(a, b):
    return lax.dot_general(a, b, (((1,), (1,)), ((), ())), preferred_element_type=F32)


def _params(*sem):
    return pltpu.CompilerParams(dimension_semantics=sem, vmem_limit_bytes=VMEM_LIMIT)


def _mod_kernel(cc_ref, w_ref, b_ref, o_ref):
    s = _silu(cc_ref[...])
    s_hi = s.astype(BF16)
    s_lo = (s - s_hi.astype(F32)).astype(BF16)
    w = w_ref[0]
    w_hi = w.astype(BF16)
    w_lo = (w - w_hi.astype(F32)).astype(BF16)
    o_ref[0] = _dot(s_hi, w_hi) + _dot(s_lo, w_hi) + _dot(s_hi, w_lo) + b_ref[0]


def _modulation(cc, w_ada, b_ada):
    L, D, N = w_ada.shape
    tn = 1024
    return pl.pallas_call(
        _mod_kernel,
        grid=(L, N // tn),
        in_specs=[
            pl.BlockSpec((MOD_ROWS, D), lambda l, j: (0, 0)),
            pl.BlockSpec((1, D, tn), lambda l, j: (l, 0, j)),
            pl.BlockSpec((1, 1, tn), lambda l, j: (l, 0, j)),
        ],
        out_specs=pl.BlockSpec((1, MOD_ROWS, tn), lambda l, j: (l, 0, j)),
        out_shape=jax.ShapeDtypeStruct((L, MOD_ROWS, N), F32),
        compiler_params=_params("arbitrary", "arbitrary"),
        name="adaln_mod",
    )(cc, w_ada, b_ada.reshape(L, 1, N))


def _in_proj_kernel(x_ref, g_ref, sh_ref, sc_ref, w_ref, o_ref, h_scr, *, col0):
    j = pl.program_id(1)

    @pl.when(j == 0)
    def _():
        x = x_ref[...]
        y = x * lax.rsqrt(jnp.mean(x * x, axis=-1, keepdims=True) + EPS) * g_ref[...]
        h_scr[...] = (y * (1.0 + sc_ref[0]) + sh_ref[0]).astype(BF16)

    c = j + col0

    def project(act):
        o_ref[...] = act(_dot(h_scr[...], w_ref[0].astype(BF16))).astype(o_ref.dtype)

    @pl.when(c < OFF_GA // COL)
    def _():
        project(_gelu)

    @pl.when((c == OFF_GA // COL) | (c == OFF_GF // COL) | (c == OFF_GN // COL))
    def _():
        project(_silu)

    @pl.when(c >= OFF_MERGE // COL)
    def _():
        project(_sigmoid)

    @pl.when((c == OFF_F // COL) | ((c >= OFF_Q // COL) & (c < OFF_GN // COL)))
    def _():
        project(lambda z: z)


def _in_proj(x2, g, mod3, w_in, layer, *, rows_per_mod, mod_row0, col0, ncol):
    T, D = x2.shape
    tm = min(1024, T)
    mrow = lambda i: mod_row0 + i // rows_per_mod
    return pl.pallas_call(
        functools.partial(_in_proj_kernel, col0=col0),
        grid=(T // tm, ncol),
        in_specs=[
            pl.BlockSpec((tm, D), lambda i, j: (i, 0), pipeline_mode=pl.Buffered(1)),
            pl.BlockSpec((1, D), lambda i, j: (0, 0)),
            pl.BlockSpec((1, 1, D), lambda i, j: (mrow(i), 0, 0)),
            pl.BlockSpec((1, 1, D), lambda i, j: (mrow(i), 0, 1)),
            pl.BlockSpec((1, D, COL), lambda i, j: (layer, 0, j + col0)),
        ],
        out_specs=pl.BlockSpec((tm, COL), lambda i, j: (i, j)),
        out_shape=jax.ShapeDtypeStruct((T, ncol * COL), Z_DTYPE),
        scratch_shapes=[pltpu.VMEM((tm, D), BF16)],
        compiler_params=_params("arbitrary", "arbitrary"),
        name="in_proj",
    )(x2, g, mod3, mod3, w_in)


def _gmlp_kernel(u_ref, vg_ref, ga_ref, lng_ref, lnb_ref, ws_ref, bst_ref, o_ref, vn_scr):
    v = vg_ref[...].astype(F32)
    mu = jnp.mean(v, axis=-1, keepdims=True)
    vc = v - mu
    vn = vc * lax.rsqrt(jnp.mean(vc * vc, axis=-1, keepdims=True) + EPS)
    vn_scr[...] = (vn * lng_ref[...] + lnb_ref[...]).astype(BF16)
    tm = u_ref.shape[0]
    for ch in range(tm // CHUNK):
        rows = slice(ch * CHUNK, (ch + 1) * CHUNK)
        for g in range(GMLP_GROUPS):
            cols = slice(g * CHUNK, (g + 1) * CHUNK)
            sv = _dot(ws_ref[g], vn_scr[rows, cols]) + bst_ref[:, g:g + 1]
            a = u_ref[rows, cols].astype(F32) * sv * ga_ref[rows, cols].astype(F32)
            o_ref[rows, cols] = a.astype(o_ref.dtype)


def _gmlp(z, lng, lnb, ws, bst):
    T = z.shape[0]
    tm = 512
    blk = lambda c: pl.BlockSpec((tm, W_A), lambda i: (i, c))
    return pl.pallas_call(
        _gmlp_kernel,
        grid=(T // tm,),
        in_specs=[
            blk(OFF_U // W_A), blk(OFF_VG // W_A), blk(OFF_GA // W_A),
            pl.BlockSpec((1, W_A), lambda i: (0, 0)),
            pl.BlockSpec((1, W_A), lambda i: (0, 0)),
            pl.BlockSpec((GMLP_GROUPS, CHUNK, CHUNK), lambda i: (0, 0, 0)),
            pl.BlockSpec((CHUNK, GMLP_GROUPS), lambda i: (0, 0)),
        ],
        out_specs=pl.BlockSpec((tm, W_A), lambda i: (i, 0)),
        out_shape=jax.ShapeDtypeStruct((T, W_A), BF16),
        scratch_shapes=[pltpu.VMEM((tm, W_A), BF16)],
        compiler_params=_params("arbitrary"),
        name="gmlp",
    )(z, z, z, lng, lnb, ws, bst)


def _dft_consts(n):
    k = np.arange(n, dtype=np.int64)
    ang = 2.0 * np.pi * ((k[:, None] * k[None, :]) % n).astype(np.float64) / n
    csn = np.concatenate([np.cos(ang), np.sin(ang)], axis=0)
    c = np.arange(F_GW, dtype=np.int64)
    angc = 2.0 * np.pi * ((c[:, None] * c[None, :]) % F_GW).astype(np.float64) / F_GW
    return tuple(jnp.asarray(t, dtype=F32).astype(BF16) for t in (csn, np.cos(angc), np.sin(angc)))


def _fourier_kernel(f_ref, gf_ref, csn_ref, cc_ref, sc_ref, o_ref):
    n = f_ref.shape[0]
    t = _dot(csn_ref[...], f_ref[...].astype(BF16))
    y = _dot(t[:n].astype(BF16), cc_ref[...]) - _dot(t[n:].astype(BF16), sc_ref[...])
    o_ref[...] = (y * ((n * F_GW) ** -0.5) * gf_ref[...].astype(F32)).astype(o_ref.dtype)


def _fourier(z, n, consts):
    T = z.shape[0]
    csn, cc, sc = consts
    const = lambda shape: pl.BlockSpec(shape, lambda b, g: (0, 0), pipeline_mode=pl.Buffered(1))
    return pl.pallas_call(
        _fourier_kernel,
        grid=(T // n, F_GROUPS),
        in_specs=[
            pl.BlockSpec((n, F_GW), lambda b, g: (b, OFF_F // F_GW + g)),
            pl.BlockSpec((n, F_GW), lambda b, g: (b, OFF_GF // F_GW + g)),
            const((2 * n, n)), const((F_GW, F_GW)), const((F_GW, F_GW)),
        ],
        out_specs=pl.BlockSpec((n, F_GW), lambda b, g: (b, g)),
        out_shape=jax.ShapeDtypeStruct((T, W_B), BF16),
        compiler_params=_params("arbitrary", "arbitrary"),
        name="fourier",
    )(z, z, csn, cc, sc)


def _head_rms(t, g):
    t = t.astype(F32)
    return t * lax.rsqrt(jnp.mean(t * t, axis=-1, keepdims=True) + EPS) * g


def _block_geometry(rows):
    geo = []
    nblk = rows // Q_ROWS
    for blk in range(nblk):
        base = min(max(Q_ROWS * blk - NA_KH // 2, 0), rows - WIN_ROWS)
        typ = 0 if blk == 0 else (2 if blk == nblk - 1 else 1)
        geo.append((base, typ))
    return geo


def _build_bias_tables(rpb_ref, tab_scr, head, geo, rows):
    n_dr, n_dc = 2 * NA_KH - 1, 2 * NA_KW - 1
    shape = (GRID_W, 2 * GRID_W)
    lane = lax.broadcasted_iota(jnp.int32, shape, 1)
    cq = lax.broadcasted_iota(jnp.int32, shape, 0)
    kc = lane & (GRID_W - 1)
    dc = kc - cq + (NA_KW - 1)
    cs = jnp.clip(cq - NA_KW // 2, 0, GRID_W - NA_KW)
    col_ok = (kc >= cs) & (kc < cs + NA_KW)
    neg = jnp.full(shape, NEG, F32)
    row_bias = []
    for d in range(n_dr):
        t = neg
        for b in range(n_dc):
            t = jnp.where(dc == b, rpb_ref[(head * n_dr + d) * n_dc + b] * LOG2E, t)
        row_bias.append(jnp.where(col_ok, t, NEG))
    nblk = len(geo)
    for typ, blk in enumerate((0, 1, nblk - 1)):
        base = geo[blk][0]
        for i in range(Q_ROWS):
            r = Q_ROWS * blk + i
            rs = min(max(r - NA_KH // 2, 0), rows - NA_KH)
            for wp in range(WIN_ROWS // 2):
                halves = []
                for kr in (base + 2 * wp, base + 2 * wp + 1):
                    halves.append(row_bias[kr - r + NA_KH - 1] if rs <= kr < rs + NA_KH else neg)
                tab_scr[typ, i * GRID_W:(i + 1) * GRID_W, wp * 2 * GRID_W:(wp + 1) * 2 * GRID_W] = (
                    jnp.where(lane < GRID_W, halves[0], halves[1]))


def _nbr_attn_kernel(rpb_ref, q_ref, k_ref, v_ref, gn_ref, kc_ref, vc_ref, qg_ref, kg_ref, o_ref,
                     qn_scr, kn_scr, kcn_scr, vcb_scr, tab_ref, *, geo, rows):
    @pl.when(pl.program_id(1) == 0)
    def _():
        _build_bias_tables(rpb_ref, tab_ref, pl.program_id(0), geo, rows)

    qn_scr[...] = (_head_rms(q_ref[...], qg_ref[...]) * ATT_SCALE).astype(BF16)
    kn_scr[...] = _head_rms(k_ref[...], kg_ref[...]).astype(BF16)
    kcn_scr[...] = _head_rms(kc_ref[...], kg_ref[...]).astype(BF16)
    vcb_scr[...] = vc_ref[...].astype(BF16)
    nq = Q_ROWS * GRID_W
    nk = WIN_ROWS * GRID_W
    for blk, (base, typ) in enumerate(geo):
        qs = slice(blk * nq, (blk + 1) * nq)
        ks = slice(base * GRID_W, base * GRID_W + nk)
        q = qn_scr[qs, :]
        s_loc = _dot_nt(q, kn_scr[ks, :]) + tab_ref[typ]
        s_ctx = _dot_nt(q, kcn_scr[...])
        m = jnp.maximum(jnp.max(s_loc, axis=-1, keepdims=True), jnp.max(s_ctx, axis=-1, keepdims=True))
        p_loc = jnp.exp2(s_loc - m)
        p_ctx = jnp.exp2(s_ctx - m)
        den = jnp.sum(p_loc, axis=-1, keepdims=True) + jnp.sum(p_ctx, axis=-1, keepdims=True)
        o = _dot(p_loc.astype(BF16), v_ref[ks, :].astype(BF16)) + _dot(p_ctx.astype(BF16), vcb_scr[...])
        o_ref[qs, :] = (o / den * gn_ref[qs, :].astype(F32)).astype(o_ref.dtype)


def _nbr_attention(z, zc, ck, cv, qg, kg, rpb_flat, seq):
    T = z.shape[0]
    rows = seq // GRID_W
    geo = _block_geometry(rows)
    hd = HEAD_DIM
    lat = lambda off: pl.BlockSpec((seq, hd), lambda h, b: (b, off // hd + h))
    return pl.pallas_call(
        functools.partial(_nbr_attn_kernel, geo=geo, rows=rows),
        grid=(NA_HEADS, T // seq),
        in_specs=[
            pl.BlockSpec(memory_space=pltpu.SMEM),
            lat(OFF_Q), lat(OFF_K), lat(OFF_V), lat(OFF_GN),
            pl.BlockSpec((CTX_LEN, hd), lambda h, b: (b, ck + h)),
            pl.BlockSpec((CTX_LEN, hd), lambda h, b: (b, cv + h)),
            pl.BlockSpec((1, hd), lambda h, b: (0, 0)),
            pl.BlockSpec((1, hd), lambda h, b: (0, 0)),
        ],
        out_specs=pl.BlockSpec((seq, hd), lambda h, b: (b, h)),
        out_shape=jax.ShapeDtypeStruct((T, W_C), BF16),
        scratch_shapes=[pltpu.VMEM((seq, hd), BF16), pltpu.VMEM((seq, hd), BF16),
                        pltpu.VMEM((CTX_LEN, hd), BF16), pltpu.VMEM((CTX_LEN, hd), BF16),
                        pltpu.VMEM((3, Q_ROWS * GRID_W, WIN_ROWS * GRID_W), F32)],
        compiler_params=_params("arbitrary", "arbitrary"),
        name="nbr_attn",
    )(rpb_flat, z, z, z, z, zc, zc, qg, kg)


def _ctx_attn_kernel(q_ref, k_ref, v_ref, gn_ref, qg_ref, kg_ref, o_ref):
    q = (_head_rms(q_ref[...], qg_ref[...]) * ATT_SCALE).astype(BF16)
    k = _head_rms(k_ref[...], kg_ref[...]).astype(BF16)
    s = _dot_nt(q, k)
    p = jnp.exp2(s - jnp.max(s, axis=-1, keepdims=True))
    o = _dot(p.astype(BF16), v_ref[...].astype(BF16)) / jnp.sum(p, axis=-1, keepdims=True)
    o_ref[...] = (o * gn_ref[...].astype(F32)).astype(o_ref.dtype)


def _ctx_attention(zc, qg, kg):
    T = zc.shape[0]
    hd = HEAD_DIM
    blk = lambda off: pl.BlockSpec((CTX_LEN, hd), lambda b, h: (b, off // hd + h))
    vec = pl.BlockSpec((1, hd), lambda b, h: (0, 0))
    return pl.pallas_call(
        _ctx_attn_kernel,
        grid=(T // CTX_LEN, NA_HEADS),
        in_specs=[blk(OFF_Q), blk(OFF_K), blk(OFF_V), blk(OFF_GN), vec, vec],
        out_specs=pl.BlockSpec((CTX_LEN, hd), lambda b, h: (b, h)),
        out_shape=jax.ShapeDtypeStruct((T, W_C), BF16),
        compiler_params=_params("arbitrary", "arbitrary"),
        name="ctx_attn",
    )(zc, zc, zc, zc, qg, kg)


def _merge_kernel(a_ref, f_ref, n_ref, ga0, ga1, gf0, gf1, gn0, gn1, x_ref, gate_ref,
                  wpa_ref, wpf_ref, wpn_ref, wout_ref, o_ref, y_scr):
    a = a_ref[...]
    f = f_ref[...]
    n = n_ref[...]
    for half, (ga, gf, gn) in enumerate(((ga0, gf0, gn0), (ga1, gf1, gn1))):
        cols = slice(half * COL, (half + 1) * COL)
        y = (ga[...].astype(F32) * _dot(a, wpa_ref[:, cols])
             + gf[...].astype(F32) * _dot(f, wpf_ref[:, cols])
             + gn[...].astype(F32) * _dot(n, wpn_ref[:, cols]))
        y_scr[:, cols] = y.astype(BF16)
    o_ref[...] = x_ref[...] + gate_ref[0] * _dot(y_scr[...], wout_ref[...])


def _merge(a, f, n, z, x2, mod3, wpa, wpf, wpn, wout, *, rows_per_mod, mod_row0):
    T, D = x2.shape
    tm = 256
    mrow = lambda i: mod_row0 + i // rows_per_mod
    act = pl.BlockSpec((tm, COL), lambda i: (i, 0))
    gate_blk = lambda c: pl.BlockSpec((tm, COL), lambda i: (i, OFF_MERGE // COL + c))
    const = lambda shape: pl.BlockSpec(shape, lambda i: (0, 0), pipeline_mode=pl.Buffered(1))
    return pl.pallas_call(
        _merge_kernel,
        grid=(T // tm,),
        in_specs=[
            act, act, act,
            gate_blk(0), gate_blk(1), gate_blk(2), gate_blk(3), gate_blk(4), gate_blk(5),
            pl.BlockSpec((tm, D), lambda i: (i, 0)),
            pl.BlockSpec((1, 1, D), lambda i: (mrow(i), 0, 2)),
            const((W_A, D)), const((W_B, D)), const((W_C, D)), const((D, D)),
        ],
        out_specs=pl.BlockSpec((tm, D), lambda i: (i, 0)),
        out_shape=jax.ShapeDtypeStruct((T, D), F32),
        scratch_shapes=[pltpu.VMEM((tm, D), BF16)],
        compiler_params=_params("arbitrary"),
        name="merge_out",
    )(a, f, n, z, z, z, z, z, z, x2, mod3, wpa, wpf, wpn, wout)


def kernel(x, c, ctx, c_ctx, norm_g, w_ada, b_ada, w_in, gmlp_ln_g, gmlp_ln_b, gmlp_ws, gmlp_bs,
           q_norm_g, k_norm_g, rpb, w_pa, w_pf, w_pn, w_out):
    B, S, D = x.shape
    C = ctx.shape[1]
    L = w_in.shape[0]
    assert D == D_MODEL and C == CTX_LEN and B < MOD_ROWS
    assert S % (GRID_W * Q_ROWS) == 0 and S // GRID_W >= WIN_ROWS

    cc = jnp.concatenate([c, c_ctx[None], jnp.zeros((MOD_ROWS - B - 1, D), F32)], axis=0)
    mod = _modulation(cc, w_ada, b_ada)

    dft_lat = _dft_consts(S)
    dft_ctx = _dft_consts(C)
    x2 = x.reshape(B * S, D)
    c2 = ctx.reshape(B * C, D)
    for l in range(L):
        last = l == L - 1
        mod3 = mod[l].reshape(MOD_ROWS, 1, 3 * D)
        g = norm_g[l].reshape(1, D)
        wpa, wpf, wpn, wout = (t[l].astype(BF16) for t in (w_pa, w_pf, w_pn, w_out))
        lng = gmlp_ln_g[l].reshape(1, W_A)
        lnb = gmlp_ln_b[l].reshape(1, W_A)
        ws = gmlp_ws[l].astype(BF16)
        bst = gmlp_bs[l].T
        qg = q_norm_g[l].reshape(1, HEAD_DIM)
        kg = k_norm_g[l].reshape(1, HEAD_DIM)
        rpb_flat = rpb[l].reshape(-1)

        if last:
            zc = _in_proj(c2, g, mod3, w_in, l, rows_per_mod=1 << 30, mod_row0=B,
                          col0=OFF_K // COL, ncol=2)
            ck, cv = 0, W_C // HEAD_DIM
        else:
            zc = _in_proj(c2, g, mod3, w_in, l, rows_per_mod=1 << 30, mod_row0=B, col0=0,
                          ncol=W_IN // COL)
            ck, cv = OFF_K // HEAD_DIM, OFF_V // HEAD_DIM
        z = _in_proj(x2, g, mod3, w_in, l, rows_per_mod=S // 1024, mod_row0=0, col0=0,
                     ncol=W_IN // COL)

        n = _nbr_attention(z, zc, ck, cv, qg, kg, rpb_flat, S)
        a = _gmlp(z, lng, lnb, ws, bst)
        f = _fourier(z, S, dft_lat)
        x_new = _merge(a, f, n, z, x2, mod3, wpa, wpf, wpn, wout, rows_per_mod=S // 256, mod_row0=0)

        if not last:
            nc = _ctx_attention(zc, qg, kg)
            ac = _gmlp(zc, lng, lnb, ws, bst)
            fc = _fourier(zc, C, dft_ctx)
            c2 = _merge(ac, fc, nc, zc, c2, mod3, wpa, wpf, wpn, wout,
                        rows_per_mod=1 << 30, mod_row0=B)
        x2 = x_new
    return x2.reshape(B, S, D)
```

```python
import functools

import jax
import jax.numpy as jnp
import numpy as np
from jax import lax
from jax.experimental import pallas as pl
from jax.experimental.pallas import tpu as pltpu

F32 = jnp.float32
BF16 = jnp.bfloat16

D_MODEL = 2048
CTX_LEN = 256
GRID_W = 64
EPS = 1e-6
CHUNK = 128
GMLP_GROUPS = 8
W_A = D_MODEL // 2
W_B = D_MODEL // 2
F_GROUPS = 4
F_GW = W_B // F_GROUPS
HEAD_DIM = 128
NA_HEADS = 8
W_C = NA_HEADS * HEAD_DIM
NA_KH = 8
NA_KW = 16

OFF_U = 0
OFF_VG = OFF_U + W_A
OFF_GA = OFF_VG + W_A
OFF_F = OFF_GA + W_A
OFF_GF = OFF_F + W_B
OFF_Q = OFF_GF + W_B
OFF_K = OFF_Q + W_C
OFF_V = OFF_K + W_C
OFF_GN = OFF_V + W_C
OFF_MERGE = OFF_GN + W_C
W_IN = OFF_MERGE + 3 * D_MODEL

LANES = 128
COL = 1024
MOD_ROWS = 16
VMEM_LIMIT = 56 * 1024 * 1024
Z_DTYPE = jnp.bfloat16

Q_ROWS = 4
WIN_ROWS = 12
NEG = -1e30
LOG2E = float(np.log2(np.e))
ATT_SCALE = HEAD_DIM ** -0.5 * LOG2E


GELU_C1 = float(np.sqrt(2.0 / np.pi))
GELU_C2 = GELU_C1 * 0.044715


def _sigmoid(t):
    return 0.5 * jnp.tanh(0.5 * t) + 0.5


def _silu(t):
    h = 0.5 * t
    return h * jnp.tanh(h) + h


def _gelu(t):
    h = 0.5 * t
    return h * jnp.tanh(t * (GELU_C1 + GELU_C2 * (t * t))) + h


def _dot(a, b):
    return jnp.dot(a, b, preferred_element_type=F32)


def _dot_nt(a, b):
    return lax.dot_general(a, b, (((1,), (1,)), ((), ())), preferred_element_type=F32)


def _params(*sem):
    return pltpu.CompilerParams(dimension_semantics=sem, vmem_limit_bytes=VMEM_LIMIT)


def _mod_kernel(cc_ref, w_ref, b_ref, o_ref):
    s = _silu(cc_ref[...])
    s_hi = s.astype(BF16)
    s_lo = (s - s_hi.astype(F32)).astype(BF16)
    w = w_ref[0]
    w_hi = w.astype(BF16)
    w_lo = (w - w_hi.astype(F32)).astype(BF16)
    o_ref[0] = _dot(s_hi, w_hi) + _dot(s_lo, w_hi) + _dot(s_hi, w_lo) + b_ref[0]


def _modulation(cc, w_ada, b_ada):
    L, D, N = w_ada.shape
    tn = 1024
    return pl.pallas_call(
        _mod_kernel,
        grid=(L, N // tn),
        in_specs=[
            pl.BlockSpec((MOD_ROWS, D), lambda l, j: (0, 0)),
            pl.BlockSpec((1, D, tn), lambda l, j: (l, 0, j)),
            pl.BlockSpec((1, 1, tn), lambda l, j: (l, 0, j)),
        ],
        out_specs=pl.BlockSpec((1, MOD_ROWS, tn), lambda l, j: (l, 0, j)),
        out_shape=jax.ShapeDtypeStruct((L, MOD_ROWS, N), F32),
        compiler_params=_params("arbitrary", "arbitrary"),
        name="adaln_mod",
    )(cc, w_ada, b_ada.reshape(L, 1, N))


def _in_proj_kernel(x_ref, g_ref, sh_ref, sc_ref, w_ref, o_ref, h_scr, *, col0):
    j = pl.program_id(1)

    @pl.when(j == 0)
    def _():
        x = x_ref[...]
        y = x * lax.rsqrt(jnp.mean(x * x, axis=-1, keepdims=True) + EPS) * g_ref[...]
        h_scr[...] = (y * (1.0 + sc_ref[0]) + sh_ref[0]).astype(BF16)

    c = j + col0

    def project(act):
        o_ref[...] = act(_dot(h_scr[...], w_ref[0].astype(BF16))).astype(o_ref.dtype)

    @pl.when(c < OFF_GA // COL)
    def _():
        project(_gelu)

    @pl.when((c == OFF_GA // COL) | (c == OFF_GF // COL) | (c == OFF_GN // COL))
    def _():
        project(_silu)

    @pl.when(c >= OFF_MERGE // COL)
    def _():
        project(_sigmoid)

    @pl.when((c == OFF_F // COL) | ((c >= OFF_Q // COL) & (c < OFF_GN // COL)))
    def _():
        project(lambda z: z)


def _in_proj(x2, g, mod3, w_in, layer, *, rows_per_mod, mod_row0, col0, ncol):
    T, D = x2.shape
    tm = min(1024, T)
    mrow = lambda i: mod_row0 + i // rows_per_mod
    return pl.pallas_call(
        functools.partial(_in_proj_kernel, col0=col0),
        grid=(T // tm, ncol),
        in_specs=[
            pl.BlockSpec((tm, D), lambda i, j: (i, 0), pipeline_mode=pl.Buffered(1)),
            pl.BlockSpec((1, D), lambda i, j: (0, 0)),
            pl.BlockSpec((1, 1, D), lambda i, j: (mrow(i), 0, 0)),
            pl.BlockSpec((1, 1, D), lambda i, j: (mrow(i), 0, 1)),
            pl.BlockSpec((1, D, COL), lambda i, j: (layer, 0, j + col0)),
        ],
        out_specs=pl.BlockSpec((tm, COL), lambda i, j: (i, j)),
        out_shape=jax.ShapeDtypeStruct((T, ncol * COL), Z_DTYPE),
        scratch_shapes=[pltpu.VMEM((tm, D), BF16)],
        compiler_params=_params("arbitrary", "arbitrary"),
        name="in_proj",
    )(x2, g, mod3, mod3, w_in)


def _gmlp_kernel(u_ref, vg_ref, ga_ref, lng_ref, lnb_ref, ws_ref, bst_ref, o_ref, vn_scr):
    v = vg_ref[...].astype(F32)
    mu = jnp.mean(v, axis=-1, keepdims=True)
    vc = v - mu
    vn = vc * lax.rsqrt(jnp.mean(vc * vc, axis=-1, keepdims=True) + EPS)
    vn_scr[...] = (vn * lng_ref[...] + lnb_ref[...]).astype(BF16)
    tm = u_ref.shape[0]
    for ch in range(tm // CHUNK):
        rows = slice(ch * CHUNK, (ch + 1) * CHUNK)
        for g in range(GMLP_GROUPS):
            cols = slice(g * CHUNK, (g + 1) * CHUNK)
            sv = _dot(ws_ref[g], vn_scr[rows, cols]) + bst_ref[:, g:g + 1]
            a = u_ref[rows, cols].astype(F32) * sv * ga_ref[rows, cols].astype(F32)
            o_ref[rows, cols] = a.astype(o_ref.dtype)


def _gmlp(z, lng, lnb, ws, bst):
    T = z.shape[0]
    tm = 512
    blk = lambda c: pl.BlockSpec((tm, W_A), lambda i: (i, c))
    return pl.pallas_call(
        _gmlp_kernel,
        grid=(T // tm,),
        in_specs=[
            blk(OFF_U // W_A), blk(OFF_VG // W_A), blk(OFF_GA // W_A),
            pl.BlockSpec((1, W_A), lambda i: (0, 0)),
            pl.BlockSpec((1, W_A), lambda i: (0, 0)),
            pl.BlockSpec((GMLP_GROUPS, CHUNK, CHUNK), lambda i: (0, 0, 0)),
            pl.BlockSpec((CHUNK, GMLP_GROUPS), lambda i: (0, 0)),
        ],
        out_specs=pl.BlockSpec((tm, W_A), lambda i: (i, 0)),
        out_shape=jax.ShapeDtypeStruct((T, W_A), BF16),
        scratch_shapes=[pltpu.VMEM((tm, W_A), BF16)],
        compiler_params=_params("arbitrary"),
        name="gmlp",
    )(z, z, z, lng, lnb, ws, bst)


DFT_RADIX = 4


def _dft_consts(n):
    m = n // DFT_RADIX
    k = np.arange(m, dtype=np.int64)
    ang = 2.0 * np.pi * ((k[:, None] * k[None, :]) % m).astype(np.float64) / m
    cs = np.concatenate([np.cos(ang), np.sin(ang)], axis=0)
    th = 2.0 * np.pi * np.arange(n // 2, dtype=np.float64) / n
    twc = np.broadcast_to(np.cos(th)[:, None], (n // 2, LANES))
    tws = np.broadcast_to(np.sin(th)[:, None], (n // 2, LANES))
    c = np.arange(F_GW, dtype=np.int64)
    angc = 2.0 * np.pi * ((c[:, None] * c[None, :]) % F_GW).astype(np.float64) / F_GW
    bf = lambda t: jnp.asarray(t, dtype=F32).astype(BF16)
    return bf(cs), jnp.asarray(twc, dtype=F32), jnp.asarray(tws, dtype=F32), bf(np.cos(angc)), bf(np.sin(angc))


def _fourier_kernel(f_ref, gf_ref, cs_ref, twc_ref, tws_ref, cc_ref, sc_ref, o_ref, x_scr):
    n = f_ref.shape[0]
    m0 = n // DFT_RADIX
    lane_tiles = F_GW // LANES
    for t in range(lane_tiles):
        x_scr[t] = f_ref[:, t * LANES:(t + 1) * LANES].astype(F32)
    xcat = jnp.concatenate(
        [x_scr[t, pl.ds(r, m0, stride=DFT_RADIX), :].astype(BF16)
         for r in range(DFT_RADIX) for t in range(lane_tiles)], axis=1)
    e = _dot(cs_ref[...], xcat)
    base = [(e[:m0, r * F_GW:(r + 1) * F_GW], e[m0:, r * F_GW:(r + 1) * F_GW]) for r in range(DFT_RADIX)]

    def sub_dft(stride, offset):
        if stride == DFT_RADIX:
            return base[offset]
        ec0, es0 = sub_dft(2 * stride, offset)
        ec1, es1 = sub_dft(2 * stride, offset + stride)
        m = n // (2 * stride)
        c = jnp.concatenate([twc_ref[pl.ds(0, m, stride=stride), :]] * lane_tiles, axis=1)
        s = jnp.concatenate([tws_ref[pl.ds(0, m, stride=stride), :]] * lane_tiles, axis=1)
        a = c * ec1 - s * es1
        b = c * es1 + s * ec1
        return (jnp.concatenate([ec0 + a, ec0 - a], axis=0), jnp.concatenate([es0 + b, es0 - b], axis=0))

    tc, ts = sub_dft(1, 0)
    y = _dot(tc.astype(BF16), cc_ref[...]) - _dot(ts.astype(BF16), sc_ref[...])
    o_ref[...] = (y * ((n * F_GW) ** -0.5) * gf_ref[...].astype(F32)).astype(o_ref.dtype)


def _fourier(z, n, consts):
    T = z.shape[0]
    const = lambda t: pl.BlockSpec(t.shape, lambda b, g: (0, 0), pipeline_mode=pl.Buffered(1))
    return pl.pallas_call(
        _fourier_kernel,
        grid=(T // n, F_GROUPS),
        in_specs=[
            pl.BlockSpec((n, F_GW), lambda b, g: (b, OFF_F // F_GW + g)),
            pl.BlockSpec((n, F_GW), lambda b, g: (b, OFF_GF // F_GW + g)),
        ] + [const(t) for t in consts],
        out_specs=pl.BlockSpec((n, F_GW), lambda b, g: (b, g)),
        out_shape=jax.ShapeDtypeStruct((T, W_B), BF16),
        scratch_shapes=[pltpu.VMEM((F_GW // LANES, n, LANES), F32)],
        compiler_params=_params("arbitrary", "arbitrary"),
        name="fourier",
    )(z, z, *consts)


def _head_rms(t, g):
    t = t.astype(F32)
    return t * lax.rsqrt(jnp.mean(t * t, axis=-1, keepdims=True) + EPS) * g


def _block_geometry(rows):
    geo = []
    nblk = rows // Q_ROWS
    for blk in range(nblk):
        base = min(max(Q_ROWS * blk - NA_KH // 2, 0), rows - WIN_ROWS)
        typ = 0 if blk == 0 else (2 if blk == nblk - 1 else 1)
        geo.append((base, typ))
    return geo


def _build_bias_tables(rpb_ref, tab_scr, head, geo, rows):
    n_dr, n_dc = 2 * NA_KH - 1, 2 * NA_KW - 1
    shape = (GRID_W, 2 * GRID_W)
    lane = lax.broadcasted_iota(jnp.int32, shape, 1)
    cq = lax.broadcasted_iota(jnp.int32, shape, 0)
    kc = lane & (GRID_W - 1)
    dc = kc - cq + (NA_KW - 1)
    cs = jnp.clip(cq - NA_KW // 2, 0, GRID_W - NA_KW)
    col_ok = (kc >= cs) & (kc < cs + NA_KW)
    neg = jnp.full(shape, NEG, F32)
    row_bias = []
    for d in range(n_dr):
        t = neg
        for b in range(n_dc):
            t = jnp.where(dc == b, rpb_ref[(head * n_dr + d) * n_dc + b] * LOG2E, t)
        row_bias.append(jnp.where(col_ok, t, NEG))
    nblk = len(geo)
    for typ, blk in enumerate((0, 1, nblk - 1)):
        base = geo[blk][0]
        for i in range(Q_ROWS):
            r = Q_ROWS * blk + i
            rs = min(max(r - NA_KH // 2, 0), rows - NA_KH)
            for wp in range(WIN_ROWS // 2):
                halves = []
                for kr in (base + 2 * wp, base + 2 * wp + 1):
                    halves.append(row_bias[kr - r + NA_KH - 1] if rs <= kr < rs + NA_KH else neg)
                tab_scr[typ, i * GRID_W:(i + 1) * GRID_W, wp * 2 * GRID_W:(wp + 1) * 2 * GRID_W] = (
                    jnp.where(lane < GRID_W, halves[0], halves[1]))


def _nbr_attn_kernel(rpb_ref, q_ref, k_ref, v_ref, gn_ref, kc_ref, vc_ref, qg_ref, kg_ref, o_ref,
                     qn_scr, kn_scr, kcn_scr, vcb_scr, tab_ref, *, geo, rows):
    @pl.when(pl.program_id(1) == 0)
    def _():
        _build_bias_tables(rpb_ref, tab_ref, pl.program_id(0), geo, rows)

    qn_scr[...] = (_head_rms(q_ref[...], qg_ref[...]) * ATT_SCALE).astype(BF16)
    kn_scr[...] = _head_rms(k_ref[...], kg_ref[...]).astype(BF16)
    kcn_scr[...] = _head_rms(kc_ref[...], kg_ref[...]).astype(BF16)
    vcb_scr[...] = vc_ref[...].astype(BF16)
    nq = Q_ROWS * GRID_W
    nk = WIN_ROWS * GRID_W
    for blk, (base, typ) in enumerate(geo):
        qs = slice(blk * nq, (blk + 1) * nq)
        ks = slice(base * GRID_W, base * GRID_W + nk)
        q = qn_scr[qs, :]
        s_loc = _dot_nt(q, kn_scr[ks, :]) + tab_ref[typ]
        s_ctx = _dot_nt(q, kcn_scr[...])
        m = jnp.maximum(jnp.max(s_loc, axis=-1, keepdims=True), jnp.max(s_ctx, axis=-1, keepdims=True))
        p_loc = jnp.exp2(s_loc - m)
        p_ctx = jnp.exp2(s_ctx - m)
        den = jnp.sum(p_loc, axis=-1, keepdims=True) + jnp.sum(p_ctx, axis=-1, keepdims=True)
        o = _dot(p_loc.astype(BF16), v_ref[ks, :].astype(BF16)) + _dot(p_ctx.astype(BF16), vcb_scr[...])
        o_ref[qs, :] = (o / den * gn_ref[qs, :].astype(F32)).astype(o_ref.dtype)


def _nbr_attention(z, zc, ck, cv, qg, kg, rpb_flat, seq):
    T = z.shape[0]
    rows = seq // GRID_W
    geo = _block_geometry(rows)
    hd = HEAD_DIM
    lat = lambda off: pl.BlockSpec((seq, hd), lambda h, b: (b, off // hd + h))
    return pl.pallas_call(
        functools.partial(_nbr_attn_kernel, geo=geo, rows=rows),
        grid=(NA_HEADS, T // seq),
        in_specs=[
            pl.BlockSpec(memory_space=pltpu.SMEM),
            lat(OFF_Q), lat(OFF_K), lat(OFF_V), lat(OFF_GN),
            pl.BlockSpec((CTX_LEN, hd), lambda h, b: (b, ck + h)),
            pl.BlockSpec((CTX_LEN, hd), lambda h, b: (b, cv + h)),
            pl.BlockSpec((1, hd), lambda h, b: (0, 0)),
            pl.BlockSpec((1, hd), lambda h, b: (0, 0)),
        ],
        out_specs=pl.BlockSpec((seq, hd), lambda h, b: (b, h)),
        out_shape=jax.ShapeDtypeStruct((T, W_C), BF16),
        scratch_shapes=[pltpu.VMEM((seq, hd), BF16), pltpu.VMEM((seq, hd), BF16),
                        pltpu.VMEM((CTX_LEN, hd), BF16), pltpu.VMEM((CTX_LEN, hd), BF16),
                        pltpu.VMEM((3, Q_ROWS * GRID_W, WIN_ROWS * GRID_W), F32)],
        compiler_params=_params("arbitrary", "arbitrary"),
        name="nbr_attn",
    )(rpb_flat, z, z, z, z, zc, zc, qg, kg)


def _ctx_attn_kernel(q_ref, k_ref, v_ref, gn_ref, qg_ref, kg_ref, o_ref):
    for h in range(NA_HEADS):
        cols = slice(h * HEAD_DIM, (h + 1) * HEAD_DIM)
        q = (_head_rms(q_ref[:, cols], qg_ref[...]) * ATT_SCALE).astype(BF16)
        k = _head_rms(k_ref[:, cols], kg_ref[...]).astype(BF16)
        s = _dot_nt(q, k)
        p = jnp.exp2(s - jnp.max(s, axis=-1, keepdims=True))
        o = _dot(p.astype(BF16), v_ref[:, cols].astype(BF16)) / jnp.sum(p, axis=-1, keepdims=True)
        o_ref[:, cols] = (o * gn_ref[:, cols].astype(F32)).astype(o_ref.dtype)


def _ctx_attention(zc, qg, kg):
    T = zc.shape[0]
    blk = lambda off: pl.BlockSpec((CTX_LEN, W_C), lambda b: (b, off // W_C))
    vec = pl.BlockSpec((1, HEAD_DIM), lambda b: (0, 0))
    return pl.pallas_call(
        _ctx_attn_kernel,
        grid=(T // CTX_LEN,),
        in_specs=[blk(OFF_Q), blk(OFF_K), blk(OFF_V), blk(OFF_GN), vec, vec],
        out_specs=pl.BlockSpec((CTX_LEN, W_C), lambda b: (b, 0)),
        out_shape=jax.ShapeDtypeStruct((T, W_C), BF16),
        compiler_params=_params("arbitrary"),
        name="ctx_attn",
    )(zc, zc, zc, zc, qg, kg)


def _merge_kernel(a_ref, f_ref, n_ref, ga0, ga1, gf0, gf1, gn0, gn1, x_ref, gate_ref,
                  wpa_ref, wpf_ref, wpn_ref, wout_ref, o_ref, y_scr):
    a = a_ref[...]
    f = f_ref[...]
    n = n_ref[...]
    for half, (ga, gf, gn) in enumerate(((ga0, gf0, gn0), (ga1, gf1, gn1))):
        cols = slice(half * COL, (half + 1) * COL)
        y = (ga[...].astype(F32) * _dot(a, wpa_ref[:, cols])
             + gf[...].astype(F32) * _dot(f, wpf_ref[:, cols])
             + gn[...].astype(F32) * _dot(n, wpn_ref[:, cols]))
        y_scr[:, cols] = y.astype(BF16)
    o_ref[...] = x_ref[...] + gate_ref[0] * _dot(y_scr[...], wout_ref[...])


def _merge(a, f, n, z, x2, mod3, wpa, wpf, wpn, wout, *, rows_per_mod, mod_row0):
    T, D = x2.shape
    tm = 256
    mrow = lambda i: mod_row0 + i // rows_per_mod
    act = pl.BlockSpec((tm, COL), lambda i: (i, 0))
    gate_blk = lambda c: pl.BlockSpec((tm, COL), lambda i: (i, OFF_MERGE // COL + c))
    const = lambda shape: pl.BlockSpec(shape, lambda i: (0, 0), pipeline_mode=pl.Buffered(1))
    return pl.pallas_call(
        _merge_kernel,
        grid=(T // tm,),
        in_specs=[
            act, act, act,
            gate_blk(0), gate_blk(1), gate_blk(2), gate_blk(3), gate_blk(4), gate_blk(5),
            pl.BlockSpec((tm, D), lambda i: (i, 0)),
            pl.BlockSpec((1, 1, D), lambda i: (mrow(i), 0, 2)),
            const((W_A, D)), const((W_B, D)), const((W_C, D)), const((D, D)),
        ],
        out_specs=pl.BlockSpec((tm, D), lambda i: (i, 0)),
        out_shape=jax.ShapeDtypeStruct((T, D), F32),
        scratch_shapes=[pltpu.VMEM((tm, D), BF16)],
        compiler_params=_params("arbitrary"),
        name="merge_out",
    )(a, f, n, z, z, z, z, z, z, x2, mod3, wpa, wpf, wpn, wout)


def kernel(x, c, ctx, c_ctx, norm_g, w_ada, b_ada, w_in, gmlp_ln_g, gmlp_ln_b, gmlp_ws, gmlp_bs,
           q_norm_g, k_norm_g, rpb, w_pa, w_pf, w_pn, w_out):
    B, S, D = x.shape
    C = ctx.shape[1]
    L = w_in.shape[0]
    assert D == D_MODEL and C == CTX_LEN and B < MOD_ROWS
    assert S % (GRID_W * Q_ROWS) == 0 and S // GRID_W >= WIN_ROWS

    cc = jnp.concatenate([c, c_ctx[None], jnp.zeros((MOD_ROWS - B - 1, D), F32)], axis=0)
    mod = _modulation(cc, w_ada, b_ada)

    dft_lat = _dft_consts(S)
    dft_ctx = _dft_consts(C)
    x2 = x.reshape(B * S, D)
    c2 = ctx.reshape(B * C, D)
    for l in range(L):
        last = l == L - 1
        mod3 = mod[l].reshape(MOD_ROWS, 1, 3 * D)
        g = norm_g[l].reshape(1, D)
        wpa, wpf, wpn, wout = (t[l].astype(BF16) for t in (w_pa, w_pf, w_pn, w_out))
        lng = gmlp_ln_g[l].reshape(1, W_A)
        lnb = gmlp_ln_b[l].reshape(1, W_A)
        ws = gmlp_ws[l].astype(BF16)
        bst = gmlp_bs[l].T
        qg = q_norm_g[l].reshape(1, HEAD_DIM)
        kg = k_norm_g[l].reshape(1, HEAD_DIM)
        rpb_flat = rpb[l].reshape(-1)

        if last:
            zc = _in_proj(c2, g, mod3, w_in, l, rows_per_mod=1 << 30, mod_row0=B,
                          col0=OFF_K // COL, ncol=2)
            ck, cv = 0, W_C // HEAD_DIM
        else:
            zc = _in_proj(c2, g, mod3, w_in, l, rows_per_mod=1 << 30, mod_row0=B, col0=0,
                          ncol=W_IN // COL)
            ck, cv = OFF_K // HEAD_DIM, OFF_V // HEAD_DIM
        z = _in_proj(x2, g, mod3, w_in, l, rows_per_mod=S // 1024, mod_row0=0, col0=0,
                     ncol=W_IN // COL)

        n = _nbr_attention(z, zc, ck, cv, qg, kg, rpb_flat, S)
        a = _gmlp(z, lng, lnb, ws, bst)
        f = _fourier(z, S, dft_lat)
        x_new = _merge(a, f, n, z, x2, mod3, wpa, wpf, wpn, wout, rows_per_mod=S // 256, mod_row0=0)

        if not last:
            nc = _ctx_attention(zc, qg, kg)
            ac = _gmlp(zc, lng, lnb, ws, bst)
            fc = _fourier(zc, C, dft_ctx)
            c2 = _merge(ac, fc, nc, zc, c2, mod3, wpa, wpf, wpn, wout,
                        rows_per_mod=1 << 30, mod_row0=B)
        x2 = x_new
    return x2.reshape(B, S, D)
```

```python
import functools

import jax
import jax.numpy as jnp
import numpy as np
from jax import lax
from jax.experimental import pallas as pl
from jax.experimental.pallas import tpu as pltpu

F32 = jnp.float32
BF16 = jnp.bfloat16

D_MODEL = 2048
CTX_LEN = 256
GRID_W = 64
EPS = 1e-6
CHUNK = 128
GMLP_GROUPS = 8
W_A = D_MODEL // 2
W_B = D_MODEL // 2
F_GROUPS = 4
F_GW = W_B // F_GROUPS
HEAD_DIM = 128
NA_HEADS = 8
W_C = NA_HEADS * HEAD_DIM
NA_KH = 8
NA_KW = 16

OFF_U = 0
OFF_VG = OFF_U + W_A
OFF_GA = OFF_VG + W_A
OFF_F = OFF_GA + W_A
OFF_GF = OFF_F + W_B
OFF_Q = OFF_GF + W_B
OFF_K = OFF_Q + W_C
OFF_V = OFF_K + W_C
OFF_GN = OFF_V + W_C
OFF_MERGE = OFF_GN + W_C
W_IN = OFF_MERGE + 3 * D_MODEL

LANES = 128
IN_PROJ_ROWS = 2048
COL = 1024
MOD_ROWS = 16
VMEM_LIMIT = 56 * 1024 * 1024
Z_DTYPE = jnp.bfloat16

Q_ROWS = 4
WIN_ROWS = 12
NEG = -1e30
LOG2E = float(np.log2(np.e))
ATT_SCALE = HEAD_DIM ** -0.5 * LOG2E


GELU_C1 = float(np.sqrt(2.0 / np.pi))
GELU_C2 = GELU_C1 * 0.044715


def _sigmoid(t):
    return 0.5 * jnp.tanh(0.5 * t) + 0.5


def _silu(t):
    h = 0.5 * t
    return h * jnp.tanh(h) + h


def _gelu(t):
    h = 0.5 * t
    return h * jnp.tanh(t * (GELU_C1 + GELU_C2 * (t * t))) + h


def _dot(a, b):
    return jnp.dot(a, b, preferred_element_type=F32)


def _dot_nt(a, b):
    return lax.dot_general(a, b, (((1,), (1,)), ((), ())), preferred_element_type=F32)


def _params(*sem):
    return pltpu.CompilerParams(dimension_semantics=sem, vmem_limit_bytes=VMEM_LIMIT)


def _mod_kernel(cc_ref, w_ref, b_ref, o_ref):
    s = _silu(cc_ref[...])
    s_hi = s.astype(BF16)
    s_lo = (s - s_hi.astype(F32)).astype(BF16)
    w = w_ref[0]
    w_hi = w.astype(BF16)
    w_lo = (w - w_hi.astype(F32)).astype(BF16)
    o_ref[0] = _dot(s_hi, w_hi) + _dot(s_lo, w_hi) + _dot(s_hi, w_lo) + b_ref[0]


def _modulation(cc, w_ada, b_ada):
    L, D, N = w_ada.shape
    tn = 1024
    return pl.pallas_call(
        _mod_kernel,
        grid=(L, N // tn),
        in_specs=[
            pl.BlockSpec((MOD_ROWS, D), lambda l, j: (0, 0)),
            pl.BlockSpec((1, D, tn), lambda l, j: (l, 0, j)),
            pl.BlockSpec((1, 1, tn), lambda l, j: (l, 0, j)),
        ],
        out_specs=pl.BlockSpec((1, MOD_ROWS, tn), lambda l, j: (l, 0, j)),
        out_shape=jax.ShapeDtypeStruct((L, MOD_ROWS, N), F32),
        compiler_params=_params("arbitrary", "arbitrary"),
        name="adaln_mod",
    )(cc, w_ada, b_ada.reshape(L, 1, N))


def _norm_mod_kernel(x_ref, g_ref, sh_ref, sc_ref, o_ref):
    x = x_ref[...]
    y = x * lax.rsqrt(jnp.mean(x * x, axis=-1, keepdims=True) + EPS) * g_ref[...]
    o_ref[...] = (y * (1.0 + sc_ref[0]) + sh_ref[0]).astype(o_ref.dtype)


def _norm_mod(x2, g, mod3, *, rows_per_mod, mod_row0):
    T, D = x2.shape
    tm = 512
    mrow = lambda i: mod_row0 + i // rows_per_mod
    return pl.pallas_call(
        _norm_mod_kernel,
        grid=(T // tm,),
        in_specs=[
            pl.BlockSpec((tm, D), lambda i: (i, 0)),
            pl.BlockSpec((1, D), lambda i: (0, 0)),
            pl.BlockSpec((1, 1, D), lambda i: (mrow(i), 0, 0)),
            pl.BlockSpec((1, 1, D), lambda i: (mrow(i), 0, 1)),
        ],
        out_specs=pl.BlockSpec((tm, D), lambda i: (i, 0)),
        out_shape=jax.ShapeDtypeStruct((T, D), BF16),
        compiler_params=_params("arbitrary"),
        name="norm_mod",
    )(x2, g, mod3, mod3)


def _in_proj_kernel(h_ref, w_ref, o_ref, *, col0):
    c = pl.program_id(1) + col0

    def project(act):
        o_ref[...] = act(_dot(h_ref[...], w_ref[0].astype(BF16))).astype(o_ref.dtype)

    @pl.when(c < OFF_GA // COL)
    def _():
        project(_gelu)

    @pl.when((c == OFF_GA // COL) | (c == OFF_GF // COL) | (c == OFF_GN // COL))
    def _():
        project(_silu)

    @pl.when(c >= OFF_MERGE // COL)
    def _():
        project(_sigmoid)

    @pl.when((c == OFF_F // COL) | ((c >= OFF_Q // COL) & (c < OFF_GN // COL)))
    def _():
        project(lambda z: z)


def _in_proj(h, w_in, layer, *, col0, ncol):
    T, D = h.shape
    tm = min(IN_PROJ_ROWS, T)
    return pl.pallas_call(
        functools.partial(_in_proj_kernel, col0=col0),
        grid=(T // tm, ncol),
        in_specs=[
            pl.BlockSpec((tm, D), lambda i, j: (i, 0), pipeline_mode=pl.Buffered(1)),
            pl.BlockSpec((1, D, COL), lambda i, j: (layer, 0, j + col0)),
        ],
        out_specs=pl.BlockSpec((tm, COL), lambda i, j: (i, j)),
        out_shape=jax.ShapeDtypeStruct((T, ncol * COL), Z_DTYPE),
        compiler_params=_params("arbitrary", "arbitrary"),
        name="in_proj",
    )(h, w_in)


def _gmlp_kernel(u_ref, vg_ref, ga_ref, lng_ref, lnb_ref, ws_ref, bst_ref, o_ref, vn_scr):
    v = vg_ref[...].astype(F32)
    mu = jnp.mean(v, axis=-1, keepdims=True)
    vc = v - mu
    vn = vc * lax.rsqrt(jnp.mean(vc * vc, axis=-1, keepdims=True) + EPS)
    vn_scr[...] = (vn * lng_ref[...] + lnb_ref[...]).astype(BF16)
    tm = u_ref.shape[0]
    for ch in range(tm // CHUNK):
        rows = slice(ch * CHUNK, (ch + 1) * CHUNK)
        for g in range(GMLP_GROUPS):
            cols = slice(g * CHUNK, (g + 1) * CHUNK)
            sv = _dot(ws_ref[g], vn_scr[rows, cols]) + bst_ref[:, g:g + 1]
            a = u_ref[rows, cols].astype(F32) * sv * ga_ref[rows, cols].astype(F32)
            o_ref[rows, cols] = a.astype(o_ref.dtype)


def _gmlp(z, lng, lnb, ws, bst):
    T = z.shape[0]
    tm = 512
    blk = lambda c: pl.BlockSpec((tm, W_A), lambda i: (i, c))
    return pl.pallas_call(
        _gmlp_kernel,
        grid=(T // tm,),
        in_specs=[
            blk(OFF_U // W_A), blk(OFF_VG // W_A), blk(OFF_GA // W_A),
            pl.BlockSpec((1, W_A), lambda i: (0, 0)),
            pl.BlockSpec((1, W_A), lambda i: (0, 0)),
            pl.BlockSpec((GMLP_GROUPS, CHUNK, CHUNK), lambda i: (0, 0, 0)),
            pl.BlockSpec((CHUNK, GMLP_GROUPS), lambda i: (0, 0)),
        ],
        out_specs=pl.BlockSpec((tm, W_A), lambda i: (i, 0)),
        out_shape=jax.ShapeDtypeStruct((T, W_A), BF16),
        scratch_shapes=[pltpu.VMEM((tm, W_A), BF16)],
        compiler_params=_params("arbitrary"),
        name="gmlp",
    )(z, z, z, lng, lnb, ws, bst)


DFT_RADIX = 4


def _dft_consts(n):
    m = n // DFT_RADIX
    k = np.arange(m, dtype=np.int64)
    ang = 2.0 * np.pi * ((k[:, None] * k[None, :]) % m).astype(np.float64) / m
    cs = np.concatenate([np.cos(ang), np.sin(ang)], axis=0)
    th = 2.0 * np.pi * np.arange(n // 2, dtype=np.float64) / n
    twc = np.broadcast_to(np.cos(th)[:, None], (n // 2, LANES))
    tws = np.broadcast_to(np.sin(th)[:, None], (n // 2, LANES))
    c = np.arange(F_GW, dtype=np.int64)
    angc = 2.0 * np.pi * ((c[:, None] * c[None, :]) % F_GW).astype(np.float64) / F_GW
    bf = lambda t: jnp.asarray(t, dtype=F32).astype(BF16)
    return bf(cs), jnp.asarray(twc, dtype=F32), jnp.asarray(tws, dtype=F32), bf(np.cos(angc)), bf(np.sin(angc))


def _fourier_kernel(f_ref, gf_ref, cs_ref, twc_ref, tws_ref, cc_ref, sc_ref, o_ref, x_scr):
    n = f_ref.shape[0]
    m0 = n // DFT_RADIX
    lane_tiles = F_GW // LANES
    for t in range(lane_tiles):
        x_scr[t] = f_ref[:, t * LANES:(t + 1) * LANES].astype(F32)
    xcat = jnp.concatenate(
        [x_scr[t, pl.ds(r, m0, stride=DFT_RADIX), :].astype(BF16)
         for r in range(DFT_RADIX) for t in range(lane_tiles)], axis=1)
    e = _dot(cs_ref[...], xcat)
    base = [(e[:m0, r * F_GW:(r + 1) * F_GW], e[m0:, r * F_GW:(r + 1) * F_GW]) for r in range(DFT_RADIX)]

    def sub_dft(stride, offset):
        if stride == DFT_RADIX:
            return base[offset]
        ec0, es0 = sub_dft(2 * stride, offset)
        ec1, es1 = sub_dft(2 * stride, offset + stride)
        m = n // (2 * stride)
        c = jnp.concatenate([twc_ref[pl.ds(0, m, stride=stride), :]] * lane_tiles, axis=1)
        s = jnp.concatenate([tws_ref[pl.ds(0, m, stride=stride), :]] * lane_tiles, axis=1)
        a = c * ec1 - s * es1
        b = c * es1 + s * ec1
        return (jnp.concatenate([ec0 + a, ec0 - a], axis=0), jnp.concatenate([es0 + b, es0 - b], axis=0))

    tc, ts = sub_dft(1, 0)
    y = _dot(tc.astype(BF16), cc_ref[...]) - _dot(ts.astype(BF16), sc_ref[...])
    o_ref[...] = (y * ((n * F_GW) ** -0.5) * gf_ref[...].astype(F32)).astype(o_ref.dtype)


def _fourier(z, n, consts):
    T = z.shape[0]
    const = lambda t: pl.BlockSpec(t.shape, lambda b, g: (0, 0), pipeline_mode=pl.Buffered(1))
    return pl.pallas_call(
        _fourier_kernel,
        grid=(T // n, F_GROUPS),
        in_specs=[
            pl.BlockSpec((n, F_GW), lambda b, g: (b, OFF_F // F_GW + g)),
            pl.BlockSpec((n, F_GW), lambda b, g: (b, OFF_GF // F_GW + g)),
        ] + [const(t) for t in consts],
        out_specs=pl.BlockSpec((n, F_GW), lambda b, g: (b, g)),
        out_shape=jax.ShapeDtypeStruct((T, W_B), BF16),
        scratch_shapes=[pltpu.VMEM((F_GW // LANES, n, LANES), F32)],
        compiler_params=_params("arbitrary", "arbitrary"),
        name="fourier",
    )(z, z, *consts)


def _head_rms(t, g):
    t = t.astype(F32)
    return t * lax.rsqrt(jnp.mean(t * t, axis=-1, keepdims=True) + EPS) * g


def _block_geometry(rows):
    geo = []
    nblk = rows // Q_ROWS
    for blk in range(nblk):
        base = min(max(Q_ROWS * blk - NA_KH // 2, 0), rows - WIN_ROWS)
        typ = 0 if blk == 0 else (2 if blk == nblk - 1 else 1)
        geo.append((base, typ))
    return geo


def _build_bias_tables(rpb_ref, tab_scr, head, geo, rows):
    n_dr, n_dc = 2 * NA_KH - 1, 2 * NA_KW - 1
    shape = (GRID_W, 2 * GRID_W)
    lane = lax.broadcasted_iota(jnp.int32, shape, 1)
    cq = lax.broadcasted_iota(jnp.int32, shape, 0)
    kc = lane & (GRID_W - 1)
    dc = kc - cq + (NA_KW - 1)
    cs = jnp.clip(cq - NA_KW // 2, 0, GRID_W - NA_KW)
    col_ok = (kc >= cs) & (kc < cs + NA_KW)
    neg = jnp.full(shape, NEG, F32)
    row_bias = []
    for d in range(n_dr):
        t = neg
        for b in range(n_dc):
            t = jnp.where(dc == b, rpb_ref[(head * n_dr + d) * n_dc + b] * LOG2E, t)
        row_bias.append(jnp.where(col_ok, t, NEG))
    nblk = len(geo)
    for typ, blk in enumerate((0, 1, nblk - 1)):
        base = geo[blk][0]
        for i in range(Q_ROWS):
            r = Q_ROWS * blk + i
            rs = min(max(r - NA_KH // 2, 0), rows - NA_KH)
            for wp in range(WIN_ROWS // 2):
                halves = []
                for kr in (base + 2 * wp, base + 2 * wp + 1):
                    halves.append(row_bias[kr - r + NA_KH - 1] if rs <= kr < rs + NA_KH else neg)
                tab_scr[typ, i * GRID_W:(i + 1) * GRID_W, wp * 2 * GRID_W:(wp + 1) * 2 * GRID_W] = (
                    jnp.where(lane < GRID_W, halves[0], halves[1]))


def _nbr_attn_kernel(rpb_ref, q_ref, k_ref, v_ref, gn_ref, kc_ref, vc_ref, qg_ref, kg_ref, o_ref,
                     qn_scr, kn_scr, kcn_scr, vcb_scr, tab_ref, *, geo, rows):
    @pl.when(pl.program_id(1) == 0)
    def _():
        _build_bias_tables(rpb_ref, tab_ref, pl.program_id(0), geo, rows)

    qn_scr[...] = (_head_rms(q_ref[...], qg_ref[...]) * ATT_SCALE).astype(BF16)
    kn_scr[...] = _head_rms(k_ref[...], kg_ref[...]).astype(BF16)
    kcn_scr[...] = _head_rms(kc_ref[...], kg_ref[...]).astype(BF16)
    vcb_scr[...] = vc_ref[...].astype(BF16)
    nq = Q_ROWS * GRID_W
    nk = WIN_ROWS * GRID_W
    for blk, (base, typ) in enumerate(geo):
        qs = slice(blk * nq, (blk + 1) * nq)
        ks = slice(base * GRID_W, base * GRID_W + nk)
        q = qn_scr[qs, :]
        s_loc = _dot_nt(q, kn_scr[ks, :]) + tab_ref[typ]
        s_ctx = _dot_nt(q, kcn_scr[...])
        m = jnp.maximum(jnp.max(s_loc, axis=-1, keepdims=True), jnp.max(s_ctx, axis=-1, keepdims=True))
        p_loc = jnp.exp2(s_loc - m)
        p_ctx = jnp.exp2(s_ctx - m)
        den = jnp.sum(p_loc, axis=-1, keepdims=True) + jnp.sum(p_ctx, axis=-1, keepdims=True)
        o = _dot(p_loc.astype(BF16), v_ref[ks, :].astype(BF16)) + _dot(p_ctx.astype(BF16), vcb_scr[...])
        o_ref[qs, :] = (o / den * gn_ref[qs, :].astype(F32)).astype(o_ref.dtype)


def _nbr_attention(z, zc, ck, cv, qg, kg, rpb_flat, seq):
    T = z.shape[0]
    rows = seq // GRID_W
    geo = _block_geometry(rows)
    hd = HEAD_DIM
    lat = lambda off: pl.BlockSpec((seq, hd), lambda h, b: (b, off // hd + h))
    return pl.pallas_call(
        functools.partial(_nbr_attn_kernel, geo=geo, rows=rows),
        grid=(NA_HEADS, T // seq),
        in_specs=[
            pl.BlockSpec(memory_space=pltpu.SMEM),
            lat(OFF_Q), lat(OFF_K), lat(OFF_V), lat(OFF_GN),
            pl.BlockSpec((CTX_LEN, hd), lambda h, b: (b, ck + h)),
            pl.BlockSpec((CTX_LEN, hd), lambda h, b: (b, cv + h)),
            pl.BlockSpec((1, hd), lambda h, b: (0, 0)),
            pl.BlockSpec((1, hd), lambda h, b: (0, 0)),
        ],
        out_specs=pl.BlockSpec((seq, hd), lambda h, b: (b, h)),
        out_shape=jax.ShapeDtypeStruct((T, W_C), BF16),
        scratch_shapes=[pltpu.VMEM((seq, hd), BF16), pltpu.VMEM((seq, hd), BF16),
                        pltpu.VMEM((CTX_LEN, hd), BF16), pltpu.VMEM((CTX_LEN, hd), BF16),
                        pltpu.VMEM((3, Q_ROWS * GRID_W, WIN_ROWS * GRID_W), F32)],
        compiler_params=_params("arbitrary", "arbitrary"),
        name="nbr_attn",
    )(rpb_flat, z, z, z, z, zc, zc, qg, kg)


def _ctx_attn_kernel(q_ref, k_ref, v_ref, gn_ref, qg_ref, kg_ref, o_ref):
    for h in range(NA_HEADS):
        cols = slice(h * HEAD_DIM, (h + 1) * HEAD_DIM)
        q = (_head_rms(q_ref[:, cols], qg_ref[...]) * ATT_SCALE).astype(BF16)
        k = _head_rms(k_ref[:, cols], kg_ref[...]).astype(BF16)
        s = _dot_nt(q, k)
        p = jnp.exp2(s - jnp.max(s, axis=-1, keepdims=True))
        o = _dot(p.astype(BF16), v_ref[:, cols].astype(BF16)) / jnp.sum(p, axis=-1, keepdims=True)
        o_ref[:, cols] = (o * gn_ref[:, cols].astype(F32)).astype(o_ref.dtype)


def _ctx_attention(zc, qg, kg):
    T = zc.shape[0]
    blk = lambda off: pl.BlockSpec((CTX_LEN, W_C), lambda b: (b, off // W_C))
    vec = pl.BlockSpec((1, HEAD_DIM), lambda b: (0, 0))
    return pl.pallas_call(
        _ctx_attn_kernel,
        grid=(T // CTX_LEN,),
        in_specs=[blk(OFF_Q), blk(OFF_K), blk(OFF_V), blk(OFF_GN), vec, vec],
        out_specs=pl.BlockSpec((CTX_LEN, W_C), lambda b: (b, 0)),
        out_shape=jax.ShapeDtypeStruct((T, W_C), BF16),
        compiler_params=_params("arbitrary"),
        name="ctx_attn",
    )(zc, zc, zc, zc, qg, kg)


def _merge_kernel(a_ref, f_ref, n_ref, ga0, ga1, gf0, gf1, gn0, gn1, x_ref, gate_ref,
                  wpa_ref, wpf_ref, wpn_ref, wout_ref, o_ref, y_scr):
    a = a_ref[...]
    f = f_ref[...]
    n = n_ref[...]
    for half, (ga, gf, gn) in enumerate(((ga0, gf0, gn0), (ga1, gf1, gn1))):
        cols = slice(half * COL, (half + 1) * COL)
        y = (ga[...].astype(F32) * _dot(a, wpa_ref[:, cols])
             + gf[...].astype(F32) * _dot(f, wpf_ref[:, cols])
             + gn[...].astype(F32) * _dot(n, wpn_ref[:, cols]))
        y_scr[:, cols] = y.astype(BF16)
    o_ref[...] = x_ref[...] + gate_ref[0] * _dot(y_scr[...], wout_ref[...])


def _merge(a, f, n, z, x2, mod3, wpa, wpf, wpn, wout, *, rows_per_mod, mod_row0):
    T, D = x2.shape
    tm = 256
    mrow = lambda i: mod_row0 + i // rows_per_mod
    act = pl.BlockSpec((tm, COL), lambda i: (i, 0))
    gate_blk = lambda c: pl.BlockSpec((tm, COL), lambda i: (i, OFF_MERGE // COL + c))
    const = lambda shape: pl.BlockSpec(shape, lambda i: (0, 0), pipeline_mode=pl.Buffered(1))
    return pl.pallas_call(
        _merge_kernel,
        grid=(T // tm,),
        in_specs=[
            act, act, act,
            gate_blk(0), gate_blk(1), gate_blk(2), gate_blk(3), gate_blk(4), gate_blk(5),
            pl.BlockSpec((tm, D), lambda i: (i, 0)),
            pl.BlockSpec((1, 1, D), lambda i: (mrow(i), 0, 2)),
            const((W_A, D)), const((W_B, D)), const((W_C, D)), const((D, D)),
        ],
        out_specs=pl.BlockSpec((tm, D), lambda i: (i, 0)),
        out_shape=jax.ShapeDtypeStruct((T, D), F32),
        scratch_shapes=[pltpu.VMEM((tm, D), BF16)],
        compiler_params=_params("arbitrary"),
        name="merge_out",
    )(a, f, n, z, z, z, z, z, z, x2, mod3, wpa, wpf, wpn, wout)


def kernel(x, c, ctx, c_ctx, norm_g, w_ada, b_ada, w_in, gmlp_ln_g, gmlp_ln_b, gmlp_ws, gmlp_bs,
           q_norm_g, k_norm_g, rpb, w_pa, w_pf, w_pn, w_out):
    B, S, D = x.shape
    C = ctx.shape[1]
    L = w_in.shape[0]
    assert D == D_MODEL and C == CTX_LEN and B < MOD_ROWS
    assert S % (GRID_W * Q_ROWS) == 0 and S // GRID_W >= WIN_ROWS

    cc = jnp.concatenate([c, c_ctx[None], jnp.zeros((MOD_ROWS - B - 1, D), F32)], axis=0)
    mod = _modulation(cc, w_ada, b_ada)

    dft_lat = _dft_consts(S)
    dft_ctx = _dft_consts(C)
    x2 = x.reshape(B * S, D)
    c2 = ctx.reshape(B * C, D)
    for l in range(L):
        last = l == L - 1
        mod3 = mod[l].reshape(MOD_ROWS, 1, 3 * D)
        g = norm_g[l].reshape(1, D)
        wpa, wpf, wpn, wout = (t[l].astype(BF16) for t in (w_pa, w_pf, w_pn, w_out))
        lng = gmlp_ln_g[l].reshape(1, W_A)
        lnb = gmlp_ln_b[l].reshape(1, W_A)
        ws = gmlp_ws[l].astype(BF16)
        bst = gmlp_bs[l].T
        qg = q_norm_g[l].reshape(1, HEAD_DIM)
        kg = k_norm_g[l].reshape(1, HEAD_DIM)
        rpb_flat = rpb[l].reshape(-1)

        hc = _norm_mod(c2, g, mod3, rows_per_mod=1 << 30, mod_row0=B)
        if last:
            zc = _in_proj(hc, w_in, l, col0=OFF_K // COL, ncol=2)
            ck, cv = 0, W_C // HEAD_DIM
        else:
            zc = _in_proj(hc, w_in, l, col0=0, ncol=W_IN // COL)
            ck, cv = OFF_K // HEAD_DIM, OFF_V // HEAD_DIM
        h = _norm_mod(x2, g, mod3, rows_per_mod=S // 512, mod_row0=0)
        z = _in_proj(h, w_in, l, col0=0, ncol=W_IN // COL)

        n = _nbr_attention(z, zc, ck, cv, qg, kg, rpb_flat, S)
        a = _gmlp(z, lng, lnb, ws, bst)
        f = _fourier(z, S, dft_lat)
        x_new = _merge(a, f, n, z, x2, mod3, wpa, wpf, wpn, wout, rows_per_mod=S // 256, mod_row0=0)

        if not last:
            nc = _ctx_attention(zc, qg, kg)
            ac = _gmlp(zc, lng, lnb, ws, bst)
            fc = _fourier(zc, C, dft_ctx)
            c2 = _merge(ac, fc, nc, zc, c2, mod3, wpa, wpf, wpn, wout,
                        rows_per_mod=1 << 30, mod_row0=B)
        x2 = x_new
    return x2.reshape(B, S, D)
```

```python
import functools

import jax
import jax.numpy as jnp
import numpy as np
from jax import lax
from jax.experimental import pallas as pl
from jax.experimental.pallas import tpu as pltpu

F32 = jnp.float32
BF16 = jnp.bfloat16

D_MODEL = 2048
CTX_LEN = 256
GRID_W = 64
EPS = 1e-6
CHUNK = 128
GMLP_GROUPS = 8
W_A = D_MODEL // 2
W_B = D_MODEL // 2
F_GROUPS = 4
F_GW = W_B // F_GROUPS
HEAD_DIM = 128
NA_HEADS = 8
W_C = NA_HEADS * HEAD_DIM
NA_KH = 8
NA_KW = 16

OFF_U = 0
OFF_VG = OFF_U + W_A
OFF_GA = OFF_VG + W_A
OFF_F = OFF_GA + W_A
OFF_GF = OFF_F + W_B
OFF_Q = OFF_GF + W_B
OFF_K = OFF_Q + W_C
OFF_V = OFF_K + W_C
OFF_GN = OFF_V + W_C
OFF_MERGE = OFF_GN + W_C
W_IN = OFF_MERGE + 3 * D_MODEL

LANES = 128
IN_PROJ_ROWS = 1024
COL = 1024
MOD_ROWS = 16
VMEM_LIMIT = 56 * 1024 * 1024
Z_DTYPE = jnp.bfloat16

Q_ROWS = 4
WIN_ROWS = 12
NEG = -1e30
LOG2E = float(np.log2(np.e))
ATT_SCALE = HEAD_DIM ** -0.5 * LOG2E


GELU_C1 = float(np.sqrt(2.0 / np.pi))
GELU_C2 = GELU_C1 * 0.044715


def _sigmoid(t):
    return 0.5 * jnp.tanh(0.5 * t) + 0.5


def _silu(t):
    h = 0.5 * t
    return h * jnp.tanh(h) + h


def _gelu(t):
    h = 0.5 * t
    return h * jnp.tanh(t * (GELU_C1 + GELU_C2 * (t * t))) + h


def _dot(a, b):
    return jnp.dot(a, b, preferred_element_type=F32)


def _dot_nt(a, b):
    return lax.dot_general(a, b, (((1,), (1,)), ((), ())), preferred_element_type=F32)


def _params(*sem):
    return pltpu.CompilerParams(dimension_semantics=sem, vmem_limit_bytes=VMEM_LIMIT)


def _mod_kernel(cc_ref, w_ref, b_ref, o_ref):
    s = _silu(cc_ref[...])
    s_hi = s.astype(BF16)
    s_lo = (s - s_hi.astype(F32)).astype(BF16)
    w = w_ref[0]
    w_hi = w.astype(BF16)
    w_lo = (w - w_hi.astype(F32)).astype(BF16)
    o_ref[0] = _dot(s_hi, w_hi) + _dot(s_lo, w_hi) + _dot(s_hi, w_lo) + b_ref[0]


def _modulation(cc, w_ada, b_ada):
    L, D, N = w_ada.shape
    tn = 1024
    return pl.pallas_call(
        _mod_kernel,
        grid=(L, N // tn),
        in_specs=[
            pl.BlockSpec((MOD_ROWS, D), lambda l, j: (0, 0)),
            pl.BlockSpec((1, D, tn), lambda l, j: (l, 0, j)),
            pl.BlockSpec((1, 1, tn), lambda l, j: (l, 0, j)),
        ],
        out_specs=pl.BlockSpec((1, MOD_ROWS, tn), lambda l, j: (l, 0, j)),
        out_shape=jax.ShapeDtypeStruct((L, MOD_ROWS, N), F32),
        compiler_params=_params("arbitrary", "arbitrary"),
        name="adaln_mod",
    )(cc, w_ada, b_ada.reshape(L, 1, N))


def _in_proj_kernel(x_ref, g_ref, sh_ref, sc_ref, w_ref, o_ref, h_scr, *, col0):
    j = pl.program_id(1)

    @pl.when(j == 0)
    def _():
        x = x_ref[...]
        y = x * lax.rsqrt(jnp.mean(x * x, axis=-1, keepdims=True) + EPS) * g_ref[...]
        h_scr[...] = (y * (1.0 + sc_ref[0]) + sh_ref[0]).astype(BF16)

    c = j + col0

    def project(act):
        o_ref[...] = act(_dot(h_scr[...], w_ref[0].astype(BF16))).astype(o_ref.dtype)

    @pl.when(c < OFF_GA // COL)
    def _():
        project(_gelu)

    @pl.when((c == OFF_GA // COL) | (c == OFF_GF // COL) | (c == OFF_GN // COL))
    def _():
        project(_silu)

    @pl.when(c >= OFF_MERGE // COL)
    def _():
        project(_sigmoid)

    @pl.when((c == OFF_F // COL) | ((c >= OFF_Q // COL) & (c < OFF_GN // COL)))
    def _():
        project(lambda z: z)


def _in_proj(x2, g, mod3, w_in, layer, *, rows_per_mod, mod_row0, col0, ncol):
    T, D = x2.shape
    tm = min(IN_PROJ_ROWS, T)
    mrow = lambda i: mod_row0 + i // rows_per_mod
    return pl.pallas_call(
        functools.partial(_in_proj_kernel, col0=col0),
        grid=(T // tm, ncol),
        in_specs=[
            pl.BlockSpec((tm, D), lambda i, j: (i, 0), pipeline_mode=pl.Buffered(1)),
            pl.BlockSpec((1, D), lambda i, j: (0, 0)),
            pl.BlockSpec((1, 1, D), lambda i, j: (mrow(i), 0, 0)),
            pl.BlockSpec((1, 1, D), lambda i, j: (mrow(i), 0, 1)),
            pl.BlockSpec((1, D, COL), lambda i, j: (layer, 0, j + col0)),
        ],
        out_specs=pl.BlockSpec((tm, COL), lambda i, j: (i, j)),
        out_shape=jax.ShapeDtypeStruct((T, ncol * COL), Z_DTYPE),
        scratch_shapes=[pltpu.VMEM((tm, D), BF16)],
        compiler_params=_params("arbitrary", "arbitrary"),
        name="in_proj",
    )(x2, g, mod3, mod3, w_in)


def _gmlp_kernel(u_ref, vg_ref, ga_ref, lng_ref, lnb_ref, ws_ref, bst_ref, o_ref, vn_scr):
    v = vg_ref[...].astype(F32)
    mu = jnp.mean(v, axis=-1, keepdims=True)
    vc = v - mu
    vn = vc * lax.rsqrt(jnp.mean(vc * vc, axis=-1, keepdims=True) + EPS)
    vn_scr[...] = (vn * lng_ref[...] + lnb_ref[...]).astype(BF16)
    tm = u_ref.shape[0]
    for ch in range(tm // CHUNK):
        rows = slice(ch * CHUNK, (ch + 1) * CHUNK)
        for g in range(GMLP_GROUPS):
            cols = slice(g * CHUNK, (g + 1) * CHUNK)
            sv = _dot(ws_ref[g], vn_scr[rows, cols]) + bst_ref[:, g:g + 1]
            a = u_ref[rows, cols].astype(F32) * sv * ga_ref[rows, cols].astype(F32)
            o_ref[rows, cols] = a.astype(o_ref.dtype)


def _gmlp(z, lng, lnb, ws, bst):
    T = z.shape[0]
    tm = 512
    blk = lambda c: pl.BlockSpec((tm, W_A), lambda i: (i, c))
    return pl.pallas_call(
        _gmlp_kernel,
        grid=(T // tm,),
        in_specs=[
            blk(OFF_U // W_A), blk(OFF_VG // W_A), blk(OFF_GA // W_A),
            pl.BlockSpec((1, W_A), lambda i: (0, 0)),
            pl.BlockSpec((1, W_A), lambda i: (0, 0)),
            pl.BlockSpec((GMLP_GROUPS, CHUNK, CHUNK), lambda i: (0, 0, 0)),
            pl.BlockSpec((CHUNK, GMLP_GROUPS), lambda i: (0, 0)),
        ],
        out_specs=pl.BlockSpec((tm, W_A), lambda i: (i, 0)),
        out_shape=jax.ShapeDtypeStruct((T, W_A), BF16),
        scratch_shapes=[pltpu.VMEM((tm, W_A), BF16)],
        compiler_params=_params("arbitrary"),
        name="gmlp",
    )(z, z, z, lng, lnb, ws, bst)


DFT_RADIX = 4


def _dft_consts(n):
    m = n // DFT_RADIX
    k = np.arange(m, dtype=np.int64)
    ang = 2.0 * np.pi * ((k[:, None] * k[None, :]) % m).astype(np.float64) / m
    cs = np.concatenate([np.cos(ang), np.sin(ang)], axis=0)
    th = 2.0 * np.pi * np.arange(n // 2, dtype=np.float64) / n
    twc = np.broadcast_to(np.cos(th)[:, None], (n // 2, LANES))
    tws = np.broadcast_to(np.sin(th)[:, None], (n // 2, LANES))
    c = np.arange(F_GW, dtype=np.int64)
    angc = 2.0 * np.pi * ((c[:, None] * c[None, :]) % F_GW).astype(np.float64) / F_GW
    bf = lambda t: jnp.asarray(t, dtype=F32).astype(BF16)
    return bf(cs), jnp.asarray(twc, dtype=F32), jnp.asarray(tws, dtype=F32), bf(np.cos(angc)), bf(np.sin(angc))


def _fourier_kernel(f_ref, gf_ref, cs_ref, twc_ref, tws_ref, cc_ref, sc_ref, o_ref, x_scr):
    n = f_ref.shape[0]
    m0 = n // DFT_RADIX
    lane_tiles = F_GW // LANES
    for t in range(lane_tiles):
        x_scr[t] = f_ref[:, t * LANES:(t + 1) * LANES].astype(F32)
    xcat = jnp.concatenate(
        [x_scr[t, pl.ds(r, m0, stride=DFT_RADIX), :].astype(BF16)
         for r in range(DFT_RADIX) for t in range(lane_tiles)], axis=1)
    e = _dot(cs_ref[...], xcat)
    base = [(e[:m0, r * F_GW:(r + 1) * F_GW], e[m0:, r * F_GW:(r + 1) * F_GW]) for r in range(DFT_RADIX)]

    def sub_dft(stride, offset):
        if stride == DFT_RADIX:
            return base[offset]
        ec0, es0 = sub_dft(2 * stride, offset)
        ec1, es1 = sub_dft(2 * stride, offset + stride)
        m = n // (2 * stride)
        c = jnp.concatenate([twc_ref[pl.ds(0, m, stride=stride), :]] * lane_tiles, axis=1)
        s = jnp.concatenate([tws_ref[pl.ds(0, m, stride=stride), :]] * lane_tiles, axis=1)
        a = c * ec1 - s * es1
        b = c * es1 + s * ec1
        return (jnp.concatenate([ec0 + a, ec0 - a], axis=0), jnp.concatenate([es0 + b, es0 - b], axis=0))

    tc, ts = sub_dft(1, 0)
    y = _dot(tc.astype(BF16), cc_ref[...]) - _dot(ts.astype(BF16), sc_ref[...])
    o_ref[...] = (y * ((n * F_GW) ** -0.5) * gf_ref[...].astype(F32)).astype(o_ref.dtype)


def _fourier(z, n, consts):
    T = z.shape[0]
    const = lambda t: pl.BlockSpec(t.shape, lambda b, g: (0, 0), pipeline_mode=pl.Buffered(1))
    return pl.pallas_call(
        _fourier_kernel,
        grid=(T // n, F_GROUPS),
        in_specs=[
            pl.BlockSpec((n, F_GW), lambda b, g: (b, OFF_F // F_GW + g)),
            pl.BlockSpec((n, F_GW), lambda b, g: (b, OFF_GF // F_GW + g)),
        ] + [const(t) for t in consts],
        out_specs=pl.BlockSpec((n, F_GW), lambda b, g: (b, g)),
        out_shape=jax.ShapeDtypeStruct((T, W_B), BF16),
        scratch_shapes=[pltpu.VMEM((F_GW // LANES, n, LANES), F32)],
        compiler_params=_params("arbitrary", "arbitrary"),
        name="fourier",
    )(z, z, *consts)


def _head_rms(t, g):
    t = t.astype(F32)
    return t * lax.rsqrt(jnp.mean(t * t, axis=-1, keepdims=True) + EPS) * g


def _block_geometry(rows):
    geo = []
    nblk = rows // Q_ROWS
    for blk in range(nblk):
        base = min(max(Q_ROWS * blk - NA_KH // 2, 0), rows - WIN_ROWS)
        typ = 0 if blk == 0 else (2 if blk == nblk - 1 else 1)
        geo.append((base, typ))
    return geo


def _build_bias_tables(rpb_ref, tab_scr, head, geo, rows):
    n_dr, n_dc = 2 * NA_KH - 1, 2 * NA_KW - 1
    shape = (GRID_W, 2 * GRID_W)
    lane = lax.broadcasted_iota(jnp.int32, shape, 1)
    cq = lax.broadcasted_iota(jnp.int32, shape, 0)
    kc = lane & (GRID_W - 1)
    dc = kc - cq + (NA_KW - 1)
    cs = jnp.clip(cq - NA_KW // 2, 0, GRID_W - NA_KW)
    col_ok = (kc >= cs) & (kc < cs + NA_KW)
    neg = jnp.full(shape, NEG, F32)
    row_bias = []
    for d in range(n_dr):
        t = neg
        for b in range(n_dc):
            t = jnp.where(dc == b, rpb_ref[(head * n_dr + d) * n_dc + b] * LOG2E, t)
        row_bias.append(jnp.where(col_ok, t, NEG))
    nblk = len(geo)
    for typ, blk in enumerate((0, 1, nblk - 1)):
        base = geo[blk][0]
        for i in range(Q_ROWS):
            r = Q_ROWS * blk + i
            rs = min(max(r - NA_KH // 2, 0), rows - NA_KH)
            for wp in range(WIN_ROWS // 2):
                halves = []
                for kr in (base + 2 * wp, base + 2 * wp + 1):
                    halves.append(row_bias[kr - r + NA_KH - 1] if rs <= kr < rs + NA_KH else neg)
                tab_scr[typ, i * GRID_W:(i + 1) * GRID_W, wp * 2 * GRID_W:(wp + 1) * 2 * GRID_W] = (
                    jnp.where(lane < GRID_W, halves[0], halves[1]))


def _nbr_attn_kernel(rpb_ref, q_ref, k_ref, v_ref, gn_ref, kc_ref, vc_ref, qg_ref, kg_ref, o_ref,
                     qn_scr, kn_scr, kcn_scr, vx_scr, vcx_scr, tab_ref, *, geo, rows):
    hd = HEAD_DIM

    @pl.when(pl.program_id(1) == 0)
    def _():
        _build_bias_tables(rpb_ref, tab_ref, pl.program_id(0), geo, rows)
        vx_scr[:, hd:] = jnp.ones((vx_scr.shape[0], hd), BF16)
        vcx_scr[:, hd:] = jnp.ones((vcx_scr.shape[0], hd), BF16)

    qn_scr[...] = (_head_rms(q_ref[...], qg_ref[...]) * ATT_SCALE).astype(BF16)
    kn_scr[...] = _head_rms(k_ref[...], kg_ref[...]).astype(BF16)
    kcn_scr[...] = _head_rms(kc_ref[...], kg_ref[...]).astype(BF16)
    vx_scr[:, :hd] = v_ref[...].astype(BF16)
    vcx_scr[:, :hd] = vc_ref[...].astype(BF16)
    nq = Q_ROWS * GRID_W
    nk = WIN_ROWS * GRID_W
    def scores(blk):
        base, typ = geo[blk]
        q = qn_scr[blk * nq:(blk + 1) * nq, :]
        s_loc = _dot_nt(q, kn_scr[base * GRID_W:base * GRID_W + nk, :]) + tab_ref[typ]
        return s_loc, _dot_nt(q, kcn_scr[...])

    nxt = scores(0)
    for blk, (base, typ) in enumerate(geo):
        qs = slice(blk * nq, (blk + 1) * nq)
        ks = slice(base * GRID_W, base * GRID_W + nk)
        s_loc, s_ctx = nxt
        if blk + 1 < len(geo):
            nxt = scores(blk + 1)
        m = jnp.maximum(jnp.max(s_loc, axis=-1, keepdims=True), jnp.max(s_ctx, axis=-1, keepdims=True))
        p_loc = jnp.exp2(s_loc - m)
        p_ctx = jnp.exp2(s_ctx - m)
        ox = _dot(p_loc.astype(BF16), vx_scr[ks, :]) + _dot(p_ctx.astype(BF16), vcx_scr[...])
        o_ref[qs, :] = (ox[:, :hd] / ox[:, hd:] * gn_ref[qs, :].astype(F32)).astype(o_ref.dtype)


def _nbr_attention(z, zc, ck, cv, qg, kg, rpb_flat, seq):
    T = z.shape[0]
    rows = seq // GRID_W
    geo = _block_geometry(rows)
    hd = HEAD_DIM
    lat = lambda off: pl.BlockSpec((seq, hd), lambda h, b: (b, off // hd + h))
    return pl.pallas_call(
        functools.partial(_nbr_attn_kernel, geo=geo, rows=rows),
        grid=(NA_HEADS, T // seq),
        in_specs=[
            pl.BlockSpec(memory_space=pltpu.SMEM),
            lat(OFF_Q), lat(OFF_K), lat(OFF_V), lat(OFF_GN),
            pl.BlockSpec((CTX_LEN, hd), lambda h, b: (b, ck + h)),
            pl.BlockSpec((CTX_LEN, hd), lambda h, b: (b, cv + h)),
            pl.BlockSpec((1, hd), lambda h, b: (0, 0)),
            pl.BlockSpec((1, hd), lambda h, b: (0, 0)),
        ],
        out_specs=pl.BlockSpec((seq, hd), lambda h, b: (b, h)),
        out_shape=jax.ShapeDtypeStruct((T, W_C), BF16),
        scratch_shapes=[pltpu.VMEM((seq, hd), BF16), pltpu.VMEM((seq, hd), BF16),
                        pltpu.VMEM((CTX_LEN, hd), BF16),
                        pltpu.VMEM((seq, 2 * hd), BF16), pltpu.VMEM((CTX_LEN, 2 * hd), BF16),
                        pltpu.VMEM((3, Q_ROWS * GRID_W, WIN_ROWS * GRID_W), F32)],
        compiler_params=_params("arbitrary", "arbitrary"),
        name="nbr_attn",
    )(rpb_flat, z, z, z, z, zc, zc, qg, kg)


def _ctx_attn_kernel(q_ref, k_ref, v_ref, gn_ref, qg_ref, kg_ref, o_ref):
    for h in range(NA_HEADS):
        cols = slice(h * HEAD_DIM, (h + 1) * HEAD_DIM)
        q = (_head_rms(q_ref[:, cols], qg_ref[...]) * ATT_SCALE).astype(BF16)
        k = _head_rms(k_ref[:, cols], kg_ref[...]).astype(BF16)
        s = _dot_nt(q, k)
        p = jnp.exp2(s - jnp.max(s, axis=-1, keepdims=True))
        o = _dot(p.astype(BF16), v_ref[:, cols].astype(BF16)) / jnp.sum(p, axis=-1, keepdims=True)
        o_ref[:, cols] = (o * gn_ref[:, cols].astype(F32)).astype(o_ref.dtype)


def _ctx_attention(zc, qg, kg):
    T = zc.shape[0]
    blk = lambda off: pl.BlockSpec((CTX_LEN, W_C), lambda b: (b, off // W_C))
    vec = pl.BlockSpec((1, HEAD_DIM), lambda b: (0, 0))
    return pl.pallas_call(
        _ctx_attn_kernel,
        grid=(T // CTX_LEN,),
        in_specs=[blk(OFF_Q), blk(OFF_K), blk(OFF_V), blk(OFF_GN), vec, vec],
        out_specs=pl.BlockSpec((CTX_LEN, W_C), lambda b: (b, 0)),
        out_shape=jax.ShapeDtypeStruct((T, W_C), BF16),
        compiler_params=_params("arbitrary"),
        name="ctx_attn",
    )(zc, zc, zc, zc, qg, kg)


def _merge_kernel(a_ref, f_ref, n_ref, ga0, ga1, gf0, gf1, gn0, gn1, x_ref, gate_ref,
                  wpa_ref, wpf_ref, wpn_ref, wout_ref, o_ref, y_scr):
    a = a_ref[...]
    f = f_ref[...]
    n = n_ref[...]
    for half, (ga, gf, gn) in enumerate(((ga0, gf0, gn0), (ga1, gf1, gn1))):
        cols = slice(half * COL, (half + 1) * COL)
        y = (ga[...].astype(F32) * _dot(a, wpa_ref[:, cols])
             + gf[...].astype(F32) * _dot(f, wpf_ref[:, cols])
             + gn[...].astype(F32) * _dot(n, wpn_ref[:, cols]))
        y_scr[:, cols] = y.astype(BF16)
    o_ref[...] = x_ref[...] + gate_ref[0] * _dot(y_scr[...], wout_ref[...])


def _merge(a, f, n, z, x2, mod3, wpa, wpf, wpn, wout, *, rows_per_mod, mod_row0):
    T, D = x2.shape
    tm = 256
    mrow = lambda i: mod_row0 + i // rows_per_mod
    act = pl.BlockSpec((tm, COL), lambda i: (i, 0))
    gate_blk = lambda c: pl.BlockSpec((tm, COL), lambda i: (i, OFF_MERGE // COL + c))
    const = lambda shape: pl.BlockSpec(shape, lambda i: (0, 0), pipeline_mode=pl.Buffered(1))
    return pl.pallas_call(
        _merge_kernel,
        grid=(T // tm,),
        in_specs=[
            act, act, act,
            gate_blk(0), gate_blk(1), gate_blk(2), gate_blk(3), gate_blk(4), gate_blk(5),
            pl.BlockSpec((tm, D), lambda i: (i, 0)),
            pl.BlockSpec((1, 1, D), lambda i: (mrow(i), 0, 2)),
            const((W_A, D)), const((W_B, D)), const((W_C, D)), const((D, D)),
        ],
        out_specs=pl.BlockSpec((tm, D), lambda i: (i, 0)),
        out_shape=jax.ShapeDtypeStruct((T, D), F32),
        scratch_shapes=[pltpu.VMEM((tm, D), BF16)],
        compiler_params=_params("arbitrary"),
        name="merge_out",
    )(a, f, n, z, z, z, z, z, z, x2, mod3, wpa, wpf, wpn, wout)


def kernel(x, c, ctx, c_ctx, norm_g, w_ada, b_ada, w_in, gmlp_ln_g, gmlp_ln_b, gmlp_ws, gmlp_bs,
           q_norm_g, k_norm_g, rpb, w_pa, w_pf, w_pn, w_out):
    B, S, D = x.shape
    C = ctx.shape[1]
    L = w_in.shape[0]
    assert D == D_MODEL and C == CTX_LEN and B < MOD_ROWS
    assert S % (GRID_W * Q_ROWS) == 0 and S // GRID_W >= WIN_ROWS

    cc = jnp.concatenate([c, c_ctx[None], jnp.zeros((MOD_ROWS - B - 1, D), F32)], axis=0)
    mod = _modulation(cc, w_ada, b_ada)

    dft_lat = _dft_consts(S)
    dft_ctx = _dft_consts(C)
    x2 = x.reshape(B * S, D)
    c2 = ctx.reshape(B * C, D)
    for l in range(L):
        last = l == L - 1
        mod3 = mod[l].reshape(MOD_ROWS, 1, 3 * D)
        g = norm_g[l].reshape(1, D)
        wpa, wpf, wpn, wout = (t[l].astype(BF16) for t in (w_pa, w_pf, w_pn, w_out))
        lng = gmlp_ln_g[l].reshape(1, W_A)
        lnb = gmlp_ln_b[l].reshape(1, W_A)
        ws = gmlp_ws[l].astype(BF16)
        bst = gmlp_bs[l].T
        qg = q_norm_g[l].reshape(1, HEAD_DIM)
        kg = k_norm_g[l].reshape(1, HEAD_DIM)
        rpb_flat = rpb[l].reshape(-1)

        if last:
            zc = _in_proj(c2, g, mod3, w_in, l, rows_per_mod=1 << 30, mod_row0=B,
                          col0=OFF_K // COL, ncol=2)
            ck, cv = 0, W_C // HEAD_DIM
        else:
            zc = _in_proj(c2, g, mod3, w_in, l, rows_per_mod=1 << 30, mod_row0=B, col0=0,
                          ncol=W_IN // COL)
            ck, cv = OFF_K // HEAD_DIM, OFF_V // HEAD_DIM
        z = _in_proj(x2, g, mod3, w_in, l, rows_per_mod=S // IN_PROJ_ROWS, mod_row0=0, col0=0,
                     ncol=W_IN // COL)

        n = _nbr_attention(z, zc, ck, cv, qg, kg, rpb_flat, S)
        a = _gmlp(z, lng, lnb, ws, bst)
        f = _fourier(z, S, dft_lat)
        x_new = _merge(a, f, n, z, x2, mod3, wpa, wpf, wpn, wout, rows_per_mod=S // 256, mod_row0=0)

        if not last:
            nc = _ctx_attention(zc, qg, kg)
            ac = _gmlp(zc, lng, lnb, ws, bst)
            fc = _fourier(zc, C, dft_ctx)
            c2 = _merge(ac, fc, nc, zc, c2, mod3, wpa, wpf, wpn, wout,
                        rows_per_mod=1 << 30, mod_row0=B)
        x2 = x_new
    return x2.reshape(B, S, D)
```

```python
import functools

import jax
import jax.numpy as jnp
import numpy as np
from jax import lax
from jax.experimental import pallas as pl
from jax.experimental.pallas import tpu as pltpu

F32 = jnp.float32
BF16 = jnp.bfloat16

D_MODEL = 2048
CTX_LEN = 256
GRID_W = 64
EPS = 1e-6
CHUNK = 128
GMLP_GROUPS = 8
W_A = D_MODEL // 2
W_B = D_MODEL // 2
F_GROUPS = 4
F_GW = W_B // F_GROUPS
HEAD_DIM = 128
NA_HEADS = 8
W_C = NA_HEADS * HEAD_DIM
NA_KH = 8
NA_KW = 16

OFF_U = 0
OFF_VG = OFF_U + W_A
OFF_GA = OFF_VG + W_A
OFF_F = OFF_GA + W_A
OFF_GF = OFF_F + W_B
OFF_Q = OFF_GF + W_B
OFF_K = OFF_Q + W_C
OFF_V = OFF_K + W_C
OFF_GN = OFF_V + W_C
OFF_MERGE = OFF_GN + W_C
W_IN = OFF_MERGE + 3 * D_MODEL

LANES = 128
IN_PROJ_ROWS = 1024
COL = 1024
MOD_ROWS = 16
VMEM_LIMIT = 56 * 1024 * 1024
Z_DTYPE = jnp.bfloat16

Q_ROWS = 4
WIN_ROWS = 12
NEG = -1e30
LOG2E = float(np.log2(np.e))
ATT_SCALE = HEAD_DIM ** -0.5 * LOG2E


GELU_C1 = float(np.sqrt(2.0 / np.pi))
GELU_C2 = GELU_C1 * 0.044715


def _sigmoid(t):
    return 0.5 * jnp.tanh(0.5 * t) + 0.5


def _silu(t):
    h = 0.5 * t
    return h * jnp.tanh(h) + h


def _gelu(t):
    h = 0.5 * t
    return h * jnp.tanh(t * (GELU_C1 + GELU_C2 * (t * t))) + h


def _dot(a, b):
    return jnp.dot(a, b, preferred_element_type=F32)


def _dot_nt(a, b):
    return lax.dot_general(a, b, (((1,), (1,)), ((), ())), preferred_element_type=F32)


def _params(*sem):
    return pltpu.CompilerParams(dimension_semantics=sem, vmem_limit_bytes=VMEM_LIMIT)


def _mod_kernel(cc_ref, w_ref, b_ref, o_ref):
    s = _silu(cc_ref[...])
    s_hi = s.astype(BF16)
    s_lo = (s - s_hi.astype(F32)).astype(BF16)
    w = w_ref[0]
    w_hi = w.astype(BF16)
    w_lo = (w - w_hi.astype(F32)).astype(BF16)
    o_ref[0] = _dot(s_hi, w_hi) + _dot(s_lo, w_hi) + _dot(s_hi, w_lo) + b_ref[0]


def _modulation(cc, w_ada, b_ada):
    L, D, N = w_ada.shape
    tn = 1024
    return pl.pallas_call(
        _mod_kernel,
        grid=(L, N // tn),
        in_specs=[
            pl.BlockSpec((MOD_ROWS, D), lambda l, j: (0, 0)),
            pl.BlockSpec((1, D, tn), lambda l, j: (l, 0, j)),
            pl.BlockSpec((1, 1, tn), lambda l, j: (l, 0, j)),
        ],
        out_specs=pl.BlockSpec((1, MOD_ROWS, tn), lambda l, j: (l, 0, j)),
        out_shape=jax.ShapeDtypeStruct((L, MOD_ROWS, N), F32),
        compiler_params=_params("arbitrary", "arbitrary"),
        name="adaln_mod",
    )(cc, w_ada, b_ada.reshape(L, 1, N))


def _in_proj_kernel(x_ref, g_ref, sh_ref, sc_ref, w_ref, o_ref, h_scr, *, col0):
    j = pl.program_id(1)

    @pl.when(j == 0)
    def _():
        x = x_ref[...]
        y = x * lax.rsqrt(jnp.mean(x * x, axis=-1, keepdims=True) + EPS) * g_ref[...]
        h_scr[...] = (y * (1.0 + sc_ref[0]) + sh_ref[0]).astype(BF16)

    c = j + col0

    def project(act):
        o_ref[...] = act(_dot(h_scr[...], w_ref[0].astype(BF16))).astype(o_ref.dtype)

    @pl.when(c < OFF_GA // COL)
    def _():
        project(_gelu)

    @pl.when((c == OFF_GA // COL) | (c == OFF_GF // COL) | (c == OFF_GN // COL))
    def _():
        project(_silu)

    @pl.when(c >= OFF_MERGE // COL)
    def _():
        project(_sigmoid)

    @pl.when((c == OFF_F // COL) | ((c >= OFF_Q // COL) & (c < OFF_GN // COL)))
    def _():
        project(lambda z: z)


def _in_proj(x2, g, mod3, w_in, layer, *, rows_per_mod, mod_row0, col0, ncol):
    T, D = x2.shape
    tm = min(IN_PROJ_ROWS, T)
    mrow = lambda i: mod_row0 + i // rows_per_mod
    return pl.pallas_call(
        functools.partial(_in_proj_kernel, col0=col0),
        grid=(T // tm, ncol),
        in_specs=[
            pl.BlockSpec((tm, D), lambda i, j: (i, 0), pipeline_mode=pl.Buffered(1)),
            pl.BlockSpec((1, D), lambda i, j: (0, 0)),
            pl.BlockSpec((1, 1, D), lambda i, j: (mrow(i), 0, 0)),
            pl.BlockSpec((1, 1, D), lambda i, j: (mrow(i), 0, 1)),
            pl.BlockSpec((1, D, COL), lambda i, j: (layer, 0, j + col0)),
        ],
        out_specs=pl.BlockSpec((tm, COL), lambda i, j: (i, j)),
        out_shape=jax.ShapeDtypeStruct((T, ncol * COL), Z_DTYPE),
        scratch_shapes=[pltpu.VMEM((tm, D), BF16)],
        compiler_params=_params("arbitrary", "arbitrary"),
        name="in_proj",
    )(x2, g, mod3, mod3, w_in)


def _spatial_gating(u_ref, vg_ref, ga_ref, lng_ref, lnb_ref, ws_ref, bst_ref, o_ref, vn_scr):
    v = vg_ref[...].astype(F32)
    mu = jnp.mean(v, axis=-1, keepdims=True)
    vc = v - mu
    vn = vc * lax.rsqrt(jnp.mean(vc * vc, axis=-1, keepdims=True) + EPS)
    vn_scr[...] = (vn * lng_ref[...] + lnb_ref[...]).astype(BF16)
    tm = u_ref.shape[0]
    for ch in range(tm // CHUNK):
        rows = slice(ch * CHUNK, (ch + 1) * CHUNK)
        for g in range(GMLP_GROUPS):
            cols = slice(g * CHUNK, (g + 1) * CHUNK)
            sv = _dot(ws_ref[g], vn_scr[rows, cols]) + bst_ref[:, g:g + 1]
            a = u_ref[rows, cols].astype(F32) * sv * ga_ref[rows, cols].astype(F32)
            o_ref[rows, cols] = a.astype(o_ref.dtype)


DFT_RADIX = 4


def _dft_consts(n):
    m = n // DFT_RADIX
    k = np.arange(m, dtype=np.int64)
    ang = 2.0 * np.pi * ((k[:, None] * k[None, :]) % m).astype(np.float64) / m
    cs = np.concatenate([np.cos(ang), np.sin(ang)], axis=0)
    th = 2.0 * np.pi * np.arange(n // 2, dtype=np.float64) / n
    twc = np.broadcast_to(np.cos(th)[:, None], (n // 2, LANES))
    tws = np.broadcast_to(np.sin(th)[:, None], (n // 2, LANES))
    c = np.arange(F_GW, dtype=np.int64)
    angc = 2.0 * np.pi * ((c[:, None] * c[None, :]) % F_GW).astype(np.float64) / F_GW
    bf = lambda t: jnp.asarray(t, dtype=F32).astype(BF16)
    return bf(cs), jnp.asarray(twc, dtype=F32), jnp.asarray(tws, dtype=F32), bf(np.cos(angc)), bf(np.sin(angc))


def _fourier_kernel(f_ref, gf_ref, cs_ref, twc_ref, tws_ref, cc_ref, sc_ref, o_ref, x_scr):
    n = f_ref.shape[0]
    m0 = n // DFT_RADIX
    lane_tiles = F_GW // LANES
    for t in range(lane_tiles):
        x_scr[t] = f_ref[:, t * LANES:(t + 1) * LANES].astype(F32)
    xcat = jnp.concatenate(
        [x_scr[t, pl.ds(r, m0, stride=DFT_RADIX), :].astype(BF16)
         for r in range(DFT_RADIX) for t in range(lane_tiles)], axis=1)
    e = _dot(cs_ref[...], xcat)
    base = [(e[:m0, r * F_GW:(r + 1) * F_GW], e[m0:, r * F_GW:(r + 1) * F_GW]) for r in range(DFT_RADIX)]

    def sub_dft(stride, offset):
        if stride == DFT_RADIX:
            return base[offset]
        ec0, es0 = sub_dft(2 * stride, offset)
        ec1, es1 = sub_dft(2 * stride, offset + stride)
        m = n // (2 * stride)
        c = jnp.concatenate([twc_ref[pl.ds(0, m, stride=stride), :]] * lane_tiles, axis=1)
        s = jnp.concatenate([tws_ref[pl.ds(0, m, stride=stride), :]] * lane_tiles, axis=1)
        a = c * ec1 - s * es1
        b = c * es1 + s * ec1
        return (jnp.concatenate([ec0 + a, ec0 - a], axis=0), jnp.concatenate([es0 + b, es0 - b], axis=0))

    tc, ts = sub_dft(1, 0)
    y = _dot(tc.astype(BF16), cc_ref[...]) - _dot(ts.astype(BF16), sc_ref[...])
    o_ref[...] = (y * ((n * F_GW) ** -0.5) * gf_ref[...].astype(F32)).astype(o_ref.dtype)


def _fourier(z, n, consts):
    T = z.shape[0]
    const = lambda t: pl.BlockSpec(t.shape, lambda b, g: (0, 0), pipeline_mode=pl.Buffered(1))
    return pl.pallas_call(
        _fourier_kernel,
        grid=(T // n, F_GROUPS),
        in_specs=[
            pl.BlockSpec((n, F_GW), lambda b, g: (b, OFF_F // F_GW + g)),
            pl.BlockSpec((n, F_GW), lambda b, g: (b, OFF_GF // F_GW + g)),
        ] + [const(t) for t in consts],
        out_specs=pl.BlockSpec((n, F_GW), lambda b, g: (b, g)),
        out_shape=jax.ShapeDtypeStruct((T, W_B), BF16),
        scratch_shapes=[pltpu.VMEM((F_GW // LANES, n, LANES), F32)],
        compiler_params=_params("arbitrary", "arbitrary"),
        name="fourier",
    )(z, z, *consts)


def _head_rms(t, g):
    t = t.astype(F32)
    return t * lax.rsqrt(jnp.mean(t * t, axis=-1, keepdims=True) + EPS) * g


def _block_geometry(rows):
    geo = []
    nblk = rows // Q_ROWS
    for blk in range(nblk):
        base = min(max(Q_ROWS * blk - NA_KH // 2, 0), rows - WIN_ROWS)
        typ = 0 if blk == 0 else (2 if blk == nblk - 1 else 1)
        geo.append((base, typ))
    return geo


def _build_bias_tables(rpb_ref, tab_scr, head, geo, rows):
    n_dr, n_dc = 2 * NA_KH - 1, 2 * NA_KW - 1
    shape = (GRID_W, 2 * GRID_W)
    lane = lax.broadcasted_iota(jnp.int32, shape, 1)
    cq = lax.broadcasted_iota(jnp.int32, shape, 0)
    kc = lane & (GRID_W - 1)
    dc = kc - cq + (NA_KW - 1)
    cs = jnp.clip(cq - NA_KW // 2, 0, GRID_W - NA_KW)
    col_ok = (kc >= cs) & (kc < cs + NA_KW)
    neg = jnp.full(shape, NEG, F32)
    row_bias = []
    for d in range(n_dr):
        t = neg
        for b in range(n_dc):
            t = jnp.where(dc == b, rpb_ref[(head * n_dr + d) * n_dc + b] * LOG2E, t)
        row_bias.append(jnp.where(col_ok, t, NEG))
    nblk = len(geo)
    for typ, blk in enumerate((0, 1, nblk - 1)):
        base = geo[blk][0]
        for i in range(Q_ROWS):
            r = Q_ROWS * blk + i
            rs = min(max(r - NA_KH // 2, 0), rows - NA_KH)
            for wp in range(WIN_ROWS // 2):
                halves = []
                for kr in (base + 2 * wp, base + 2 * wp + 1):
                    halves.append(row_bias[kr - r + NA_KH - 1] if rs <= kr < rs + NA_KH else neg)
                tab_scr[typ, i * GRID_W:(i + 1) * GRID_W, wp * 2 * GRID_W:(wp + 1) * 2 * GRID_W] = (
                    jnp.where(lane < GRID_W, halves[0], halves[1]))


def _nbr_attn_kernel(rpb_ref, q_ref, k_ref, v_ref, gn_ref, kc_ref, vc_ref, qg_ref, kg_ref, o_ref,
                     qn_scr, kn_scr, kcn_scr, vx_scr, vcx_scr, tab_ref, *, geo, rows):
    hd = HEAD_DIM

    @pl.when(pl.program_id(1) == 0)
    def _():
        _build_bias_tables(rpb_ref, tab_ref, pl.program_id(0), geo, rows)
        vx_scr[:, hd:] = jnp.ones((vx_scr.shape[0], hd), BF16)
        vcx_scr[:, hd:] = jnp.ones((vcx_scr.shape[0], hd), BF16)

    qn_scr[...] = (_head_rms(q_ref[...], qg_ref[...]) * ATT_SCALE).astype(BF16)
    kn_scr[...] = _head_rms(k_ref[...], kg_ref[...]).astype(BF16)
    kcn_scr[...] = _head_rms(kc_ref[...], kg_ref[...]).astype(BF16)
    vx_scr[:, :hd] = v_ref[...].astype(BF16)
    vcx_scr[:, :hd] = vc_ref[...].astype(BF16)
    nq = Q_ROWS * GRID_W
    nk = WIN_ROWS * GRID_W
    def scores(blk):
        base, typ = geo[blk]
        q = qn_scr[blk * nq:(blk + 1) * nq, :]
        s_loc = _dot_nt(q, kn_scr[base * GRID_W:base * GRID_W + nk, :]) + tab_ref[typ]
        return s_loc, _dot_nt(q, kcn_scr[...])

    nxt = scores(0)
    for blk, (base, typ) in enumerate(geo):
        qs = slice(blk * nq, (blk + 1) * nq)
        ks = slice(base * GRID_W, base * GRID_W + nk)
        s_loc, s_ctx = nxt
        if blk + 1 < len(geo):
            nxt = scores(blk + 1)
        m = jnp.maximum(jnp.max(s_loc, axis=-1, keepdims=True), jnp.max(s_ctx, axis=-1, keepdims=True))
        p_loc = jnp.exp2(s_loc - m)
        p_ctx = jnp.exp2(s_ctx - m)
        ox = _dot(p_loc.astype(BF16), vx_scr[ks, :]) + _dot(p_ctx.astype(BF16), vcx_scr[...])
        o_ref[qs, :] = (ox[:, :hd] / ox[:, hd:] * gn_ref[qs, :].astype(F32)).astype(o_ref.dtype)


def _nbr_attention(z, zc, ck, cv, qg, kg, rpb_flat, seq):
    T = z.shape[0]
    rows = seq // GRID_W
    geo = _block_geometry(rows)
    hd = HEAD_DIM
    lat = lambda off: pl.BlockSpec((seq, hd), lambda h, b: (b, off // hd + h))
    return pl.pallas_call(
        functools.partial(_nbr_attn_kernel, geo=geo, rows=rows),
        grid=(NA_HEADS, T // seq),
        in_specs=[
            pl.BlockSpec(memory_space=pltpu.SMEM),
            lat(OFF_Q), lat(OFF_K), lat(OFF_V), lat(OFF_GN),
            pl.BlockSpec((CTX_LEN, hd), lambda h, b: (b, ck + h)),
            pl.BlockSpec((CTX_LEN, hd), lambda h, b: (b, cv + h)),
            pl.BlockSpec((1, hd), lambda h, b: (0, 0)),
            pl.BlockSpec((1, hd), lambda h, b: (0, 0)),
        ],
        out_specs=pl.BlockSpec((seq, hd), lambda h, b: (b, h)),
        out_shape=jax.ShapeDtypeStruct((T, W_C), BF16),
        scratch_shapes=[pltpu.VMEM((seq, hd), BF16), pltpu.VMEM((seq, hd), BF16),
                        pltpu.VMEM((CTX_LEN, hd), BF16),
                        pltpu.VMEM((seq, 2 * hd), BF16), pltpu.VMEM((CTX_LEN, 2 * hd), BF16),
                        pltpu.VMEM((3, Q_ROWS * GRID_W, WIN_ROWS * GRID_W), F32)],
        compiler_params=_params("arbitrary", "arbitrary"),
        name="nbr_attn",
    )(rpb_flat, z, z, z, z, zc, zc, qg, kg)


def _ctx_attn_kernel(q_ref, k_ref, v_ref, gn_ref, qg_ref, kg_ref, o_ref):
    for h in range(NA_HEADS):
        cols = slice(h * HEAD_DIM, (h + 1) * HEAD_DIM)
        q = (_head_rms(q_ref[:, cols], qg_ref[...]) * ATT_SCALE).astype(BF16)
        k = _head_rms(k_ref[:, cols], kg_ref[...]).astype(BF16)
        s = _dot_nt(q, k)
        p = jnp.exp2(s - jnp.max(s, axis=-1, keepdims=True))
        o = _dot(p.astype(BF16), v_ref[:, cols].astype(BF16)) / jnp.sum(p, axis=-1, keepdims=True)
        o_ref[:, cols] = (o * gn_ref[:, cols].astype(F32)).astype(o_ref.dtype)


def _ctx_attention(zc, qg, kg):
    T = zc.shape[0]
    blk = lambda off: pl.BlockSpec((CTX_LEN, W_C), lambda b: (b, off // W_C))
    vec = pl.BlockSpec((1, HEAD_DIM), lambda b: (0, 0))
    return pl.pallas_call(
        _ctx_attn_kernel,
        grid=(T // CTX_LEN,),
        in_specs=[blk(OFF_Q), blk(OFF_K), blk(OFF_V), blk(OFF_GN), vec, vec],
        out_specs=pl.BlockSpec((CTX_LEN, W_C), lambda b: (b, 0)),
        out_shape=jax.ShapeDtypeStruct((T, W_C), BF16),
        compiler_params=_params("arbitrary"),
        name="ctx_attn",
    )(zc, zc, zc, zc, qg, kg)


def _merge_kernel(u_ref, vg_ref, ga_ref, f_ref, n_ref, ga0, ga1, gf0, gf1, gn0, gn1, x_ref, gate_ref,
                  lng_ref, lnb_ref, ws_ref, bst_ref, wpa_ref, wpf_ref, wpn_ref, wout_ref, o_ref,
                  a_scr, vn_scr, yfn_scr, y_scr):
    f = f_ref[...]
    n = n_ref[...]
    for half, (gf, gn) in enumerate(((gf0, gn0), (gf1, gn1))):
        cols = slice(half * COL, (half + 1) * COL)
        yfn_scr[:, cols] = (gf[...].astype(F32) * _dot(f, wpf_ref[:, cols])
                            + gn[...].astype(F32) * _dot(n, wpn_ref[:, cols]))
    _spatial_gating(u_ref, vg_ref, ga_ref, lng_ref, lnb_ref, ws_ref, bst_ref, a_scr, vn_scr)
    a = a_scr[...]
    for half, ga in enumerate((ga0, ga1)):
        cols = slice(half * COL, (half + 1) * COL)
        y = yfn_scr[:, cols] + ga[...].astype(F32) * _dot(a, wpa_ref[:, cols])
        y_scr[:, cols] = y.astype(BF16)
    o_ref[...] = x_ref[...] + gate_ref[0] * _dot(y_scr[...], wout_ref[...])


def _merge(f, n, z, x2, mod3, lng, lnb, ws, bst, wpa, wpf, wpn, wout, *, rows_per_mod, mod_row0):
    T, D = x2.shape
    tm = 256
    mrow = lambda i: mod_row0 + i // rows_per_mod
    act = pl.BlockSpec((tm, COL), lambda i: (i, 0))
    zblk = lambda c: pl.BlockSpec((tm, COL), lambda i: (i, c))
    const = lambda shape: pl.BlockSpec(shape, lambda i: (0,) * len(shape), pipeline_mode=pl.Buffered(1))
    return pl.pallas_call(
        _merge_kernel,
        grid=(T // tm,),
        in_specs=[
            zblk(OFF_U // COL), zblk(OFF_VG // COL), zblk(OFF_GA // COL),
            act, act,
        ] + [zblk(OFF_MERGE // COL + c) for c in range(3 * D // COL)] + [
            pl.BlockSpec((tm, D), lambda i: (i, 0)),
            pl.BlockSpec((1, 1, D), lambda i: (mrow(i), 0, 2)),
            const((1, W_A)), const((1, W_A)), const((GMLP_GROUPS, CHUNK, CHUNK)), const((CHUNK, GMLP_GROUPS)),
            const((W_A, D)), const((W_B, D)), const((W_C, D)), const((D, D)),
        ],
        out_specs=pl.BlockSpec((tm, D), lambda i: (i, 0)),
        out_shape=jax.ShapeDtypeStruct((T, D), F32),
        scratch_shapes=[pltpu.VMEM((tm, W_A), BF16), pltpu.VMEM((tm, W_A), BF16),
                        pltpu.VMEM((tm, D), F32), pltpu.VMEM((tm, D), BF16)],
        compiler_params=_params("arbitrary"),
        name="merge_out",
    )(z, z, z, f, n, z, z, z, z, z, z, x2, mod3, lng, lnb, ws, bst, wpa, wpf, wpn, wout)


def kernel(x, c, ctx, c_ctx, norm_g, w_ada, b_ada, w_in, gmlp_ln_g, gmlp_ln_b, gmlp_ws, gmlp_bs,
           q_norm_g, k_norm_g, rpb, w_pa, w_pf, w_pn, w_out):
    B, S, D = x.shape
    C = ctx.shape[1]
    L = w_in.shape[0]
    assert D == D_MODEL and C == CTX_LEN and B < MOD_ROWS
    assert S % (GRID_W * Q_ROWS) == 0 and S // GRID_W >= WIN_ROWS

    cc = jnp.concatenate([c, c_ctx[None], jnp.zeros((MOD_ROWS - B - 1, D), F32)], axis=0)
    mod = _modulation(cc, w_ada, b_ada)

    dft_lat = _dft_consts(S)
    dft_ctx = _dft_consts(C)
    x2 = x.reshape(B * S, D)
    c2 = ctx.reshape(B * C, D)
    for l in range(L):
        last = l == L - 1
        mod3 = mod[l].reshape(MOD_ROWS, 1, 3 * D)
        g = norm_g[l].reshape(1, D)
        wpa, wpf, wpn, wout = (t[l].astype(BF16) for t in (w_pa, w_pf, w_pn, w_out))
        lng = gmlp_ln_g[l].reshape(1, W_A)
        lnb = gmlp_ln_b[l].reshape(1, W_A)
        ws = gmlp_ws[l].astype(BF16)
        bst = gmlp_bs[l].T
        qg = q_norm_g[l].reshape(1, HEAD_DIM)
        kg = k_norm_g[l].reshape(1, HEAD_DIM)
        rpb_flat = rpb[l].reshape(-1)

        if last:
            zc = _in_proj(c2, g, mod3, w_in, l, rows_per_mod=1 << 30, mod_row0=B,
                          col0=OFF_K // COL, ncol=2)
            ck, cv = 0, W_C // HEAD_DIM
        else:
            zc = _in_proj(c2, g, mod3, w_in, l, rows_per_mod=1 << 30, mod_row0=B, col0=0,
                          ncol=W_IN // COL)
            ck, cv = OFF_K // HEAD_DIM, OFF_V // HEAD_DIM
        z = _in_proj(x2, g, mod3, w_in, l, rows_per_mod=S // IN_PROJ_ROWS, mod_row0=0, col0=0,
                     ncol=W_IN // COL)

        n = _nbr_attention(z, zc, ck, cv, qg, kg, rpb_flat, S)
        f = _fourier(z, S, dft_lat)
        x_new = _merge(f, n, z, x2, mod3, lng, lnb, ws, bst, wpa, wpf, wpn, wout,
                       rows_per_mod=S // 256, mod_row0=0)

        if not last:
            nc = _ctx_attention(zc, qg, kg)
            fc = _fourier(zc, C, dft_ctx)
            c2 = _merge(fc, nc, zc, c2, mod3, lng, lnb, ws, bst, wpa, wpf, wpn, wout,
                        rows_per_mod=1 << 30, mod_row0=B)
        x2 = x_new
    return x2.reshape(B, S, D)
```

```python
import functools

import jax
import jax.numpy as jnp
import numpy as np
from jax import lax
from jax.experimental import pallas as pl
from jax.experimental.pallas import tpu as pltpu

F32 = jnp.float32
BF16 = jnp.bfloat16

D_MODEL = 2048
CTX_LEN = 256
GRID_W = 64
EPS = 1e-6
CHUNK = 128
GMLP_GROUPS = 8
W_A = D_MODEL // 2
W_B = D_MODEL // 2
F_GROUPS = 4
F_GW = W_B // F_GROUPS
HEAD_DIM = 128
NA_HEADS = 8
W_C = NA_HEADS * HEAD_DIM
NA_KH = 8
NA_KW = 16

OFF_U = 0
OFF_VG = OFF_U + W_A
OFF_GA = OFF_VG + W_A
OFF_F = OFF_GA + W_A
OFF_GF = OFF_F + W_B
OFF_Q = OFF_GF + W_B
OFF_K = OFF_Q + W_C
OFF_V = OFF_K + W_C
OFF_GN = OFF_V + W_C
OFF_MERGE = OFF_GN + W_C
W_IN = OFF_MERGE + 3 * D_MODEL

LANES = 128
IN_PROJ_ROWS = 1024
COL = 1024
MOD_ROWS = 16
VMEM_LIMIT = 56 * 1024 * 1024
Z_DTYPE = jnp.bfloat16

Q_ROWS = 4
WIN_ROWS = 12
NEG = -1e30
LOG2E = float(np.log2(np.e))
ATT_SCALE = HEAD_DIM ** -0.5 * LOG2E


GELU_C1 = float(np.sqrt(2.0 / np.pi))
GELU_C2 = GELU_C1 * 0.044715


def _sigmoid(t):
    return 0.5 * jnp.tanh(0.5 * t) + 0.5


def _silu(t):
    h = 0.5 * t
    return h * jnp.tanh(h) + h


def _gelu(t):
    h = 0.5 * t
    return h * jnp.tanh(t * (GELU_C1 + GELU_C2 * (t * t))) + h


def _dot(a, b):
    return jnp.dot(a, b, preferred_element_type=F32)


def _dot_nt(a, b):
    return lax.dot_general(a, b, (((1,), (1,)), ((), ())), preferred_element_type=F32)


def _params(*sem):
    return pltpu.CompilerParams(dimension_semantics=sem, vmem_limit_bytes=VMEM_LIMIT)


def _mod_kernel(cc_ref, w_ref, b_ref, o_ref):
    s = _silu(cc_ref[...])
    s_hi = s.astype(BF16)
    s_lo = (s - s_hi.astype(F32)).astype(BF16)
    w = w_ref[0]
    w_hi = w.astype(BF16)
    w_lo = (w - w_hi.astype(F32)).astype(BF16)
    o_ref[0] = _dot(s_hi, w_hi) + _dot(s_lo, w_hi) + _dot(s_hi, w_lo) + b_ref[0]


def _modulation(cc, w_ada, b_ada):
    L, D, N = w_ada.shape
    tn = 1024
    return pl.pallas_call(
        _mod_kernel,
        grid=(L, N // tn),
        in_specs=[
            pl.BlockSpec((MOD_ROWS, D), lambda l, j: (0, 0)),
            pl.BlockSpec((1, D, tn), lambda l, j: (l, 0, j)),
            pl.BlockSpec((1, 1, tn), lambda l, j: (l, 0, j)),
        ],
        out_specs=pl.BlockSpec((1, MOD_ROWS, tn), lambda l, j: (l, 0, j)),
        out_shape=jax.ShapeDtypeStruct((L, MOD_ROWS, N), F32),
        compiler_params=_params("arbitrary", "arbitrary"),
        name="adaln_mod",
    )(cc, w_ada, b_ada.reshape(L, 1, N))


def _in_proj_kernel(x_ref, g_ref, sh_ref, sc_ref, w_ref, o_ref, h_scr, *, col0):
    j = pl.program_id(1)

    @pl.when(j == 0)
    def _():
        x = x_ref[...]
        gain = g_ref[...] * (1.0 + sc_ref[0])
        r = lax.rsqrt(jnp.mean(x * x, axis=-1, keepdims=True) + EPS)
        h_scr[...] = (x * r * gain + sh_ref[0]).astype(BF16)

    c = j + col0

    def project(act):
        o_ref[...] = act(_dot(h_scr[...], w_ref[0].astype(BF16))).astype(o_ref.dtype)

    @pl.when(c < OFF_GA // COL)
    def _():
        project(_gelu)

    @pl.when((c == OFF_GA // COL) | (c == OFF_GF // COL) | (c == OFF_GN // COL))
    def _():
        project(_silu)

    @pl.when(c >= OFF_MERGE // COL)
    def _():
        project(_sigmoid)

    @pl.when((c == OFF_F // COL) | ((c >= OFF_Q // COL) & (c < OFF_GN // COL)))
    def _():
        project(lambda z: z)


def _in_proj(x2, g, mod3, w_in, layer, *, rows_per_mod, mod_row0, col0, ncol):
    T, D = x2.shape
    tm = min(IN_PROJ_ROWS, T)
    mrow = lambda i: mod_row0 + i // rows_per_mod
    return pl.pallas_call(
        functools.partial(_in_proj_kernel, col0=col0),
        grid=(T // tm, ncol),
        in_specs=[
            pl.BlockSpec((tm, D), lambda i, j: (i, 0)),
            pl.BlockSpec((1, D), lambda i, j: (0, 0)),
            pl.BlockSpec((1, 1, D), lambda i, j: (mrow(i), 0, 0)),
            pl.BlockSpec((1, 1, D), lambda i, j: (mrow(i), 0, 1)),
            pl.BlockSpec((1, D, COL), lambda i, j: (layer, 0, j + col0)),
        ],
        out_specs=pl.BlockSpec((tm, COL), lambda i, j: (i, j)),
        out_shape=jax.ShapeDtypeStruct((T, ncol * COL), Z_DTYPE),
        scratch_shapes=[pltpu.VMEM((tm, D), BF16)],
        compiler_params=_params("arbitrary", "arbitrary"),
        name="in_proj",
    )(x2, g, mod3, mod3, w_in)


def _spatial_gating(u_ref, vg_ref, ga_ref, lng_ref, lnb_ref, ws_ref, bst_ref, o_ref, vn_scr):
    v = vg_ref[...].astype(F32)
    mu = jnp.mean(v, axis=-1, keepdims=True)
    vc = v - mu
    vn = vc * lax.rsqrt(jnp.mean(vc * vc, axis=-1, keepdims=True) + EPS)
    vn_scr[...] = (vn * lng_ref[...] + lnb_ref[...]).astype(BF16)
    tm = u_ref.shape[0]
    for ch in range(tm // CHUNK):
        rows = slice(ch * CHUNK, (ch + 1) * CHUNK)
        for g in range(GMLP_GROUPS):
            cols = slice(g * CHUNK, (g + 1) * CHUNK)
            sv = _dot(ws_ref[g], vn_scr[rows, cols]) + bst_ref[:, g:g + 1]
            a = u_ref[rows, cols].astype(F32) * sv * ga_ref[rows, cols].astype(F32)
            o_ref[rows, cols] = a.astype(o_ref.dtype)


DFT_RADIX = 4


def _dft_consts(n):
    m = n // DFT_RADIX
    k = np.arange(m, dtype=np.int64)
    ang = 2.0 * np.pi * ((k[:, None] * k[None, :]) % m).astype(np.float64) / m
    cs = np.concatenate([np.cos(ang), np.sin(ang)], axis=0)
    th = 2.0 * np.pi * np.arange(n // 2, dtype=np.float64) / n
    twc = np.broadcast_to(np.cos(th)[:, None], (n // 2, LANES))
    tws = np.broadcast_to(np.sin(th)[:, None], (n // 2, LANES))
    c = np.arange(F_GW, dtype=np.int64)
    angc = 2.0 * np.pi * ((c[:, None] * c[None, :]) % F_GW).astype(np.float64) / F_GW
    bf = lambda t: jnp.asarray(t, dtype=F32).astype(BF16)
    return bf(cs), jnp.asarray(twc, dtype=F32), jnp.asarray(tws, dtype=F32), bf(np.cos(angc)), bf(np.sin(angc))


def _fourier_kernel(f_ref, gf_ref, cs_ref, twc_ref, tws_ref, cc_ref, sc_ref, o_ref, x_scr):
    n = f_ref.shape[0]
    m0 = n // DFT_RADIX
    lane_tiles = F_GW // LANES
    for t in range(lane_tiles):
        x_scr[t] = f_ref[:, t * LANES:(t + 1) * LANES].astype(F32)
    xcat = jnp.concatenate(
        [x_scr[t, pl.ds(r, m0, stride=DFT_RADIX), :].astype(BF16)
         for r in range(DFT_RADIX) for t in range(lane_tiles)], axis=1)
    e = _dot(cs_ref[...], xcat)
    base = [(e[:m0, r * F_GW:(r + 1) * F_GW], e[m0:, r * F_GW:(r + 1) * F_GW]) for r in range(DFT_RADIX)]

    def sub_dft(stride, offset):
        if stride == DFT_RADIX:
            return base[offset]
        ec0, es0 = sub_dft(2 * stride, offset)
        ec1, es1 = sub_dft(2 * stride, offset + stride)
        m = n // (2 * stride)
        c = jnp.concatenate([twc_ref[pl.ds(0, m, stride=stride), :]] * lane_tiles, axis=1)
        s = jnp.concatenate([tws_ref[pl.ds(0, m, stride=stride), :]] * lane_tiles, axis=1)
        a = c * ec1 - s * es1
        b = c * es1 + s * ec1
        return (jnp.concatenate([ec0 + a, ec0 - a], axis=0), jnp.concatenate([es0 + b, es0 - b], axis=0))

    tc, ts = sub_dft(1, 0)
    y = _dot(tc.astype(BF16), cc_ref[...]) - _dot(ts.astype(BF16), sc_ref[...])
    o_ref[...] = (y * ((n * F_GW) ** -0.5) * gf_ref[...].astype(F32)).astype(o_ref.dtype)


def _fourier(z, n, consts):
    T = z.shape[0]
    const = lambda t: pl.BlockSpec(t.shape, lambda b, g: (0, 0), pipeline_mode=pl.Buffered(1))
    return pl.pallas_call(
        _fourier_kernel,
        grid=(T // n, F_GROUPS),
        in_specs=[
            pl.BlockSpec((n, F_GW), lambda b, g: (b, OFF_F // F_GW + g)),
            pl.BlockSpec((n, F_GW), lambda b, g: (b, OFF_GF // F_GW + g)),
        ] + [const(t) for t in consts],
        out_specs=pl.BlockSpec((n, F_GW), lambda b, g: (b, g)),
        out_shape=jax.ShapeDtypeStruct((T, W_B), BF16),
        scratch_shapes=[pltpu.VMEM((F_GW // LANES, n, LANES), F32)],
        compiler_params=_params("arbitrary", "arbitrary"),
        name="fourier",
    )(z, z, *consts)


def _head_rms(t, g):
    t = t.astype(F32)
    return t * lax.rsqrt(jnp.mean(t * t, axis=-1, keepdims=True) + EPS) * g


def _block_geometry(rows):
    geo = []
    nblk = rows // Q_ROWS
    for blk in range(nblk):
        base = min(max(Q_ROWS * blk - NA_KH // 2, 0), rows - WIN_ROWS)
        typ = 0 if blk == 0 else (2 if blk == nblk - 1 else 1)
        geo.append((base, typ))
    return geo


def _build_bias_tables(rpb_ref, tab_scr, head, geo, rows):
    n_dr, n_dc = 2 * NA_KH - 1, 2 * NA_KW - 1
    shape = (GRID_W, 2 * GRID_W)
    lane = lax.broadcasted_iota(jnp.int32, shape, 1)
    cq = lax.broadcasted_iota(jnp.int32, shape, 0)
    kc = lane & (GRID_W - 1)
    dc = kc - cq + (NA_KW - 1)
    cs = jnp.clip(cq - NA_KW // 2, 0, GRID_W - NA_KW)
    col_ok = (kc >= cs) & (kc < cs + NA_KW)
    neg = jnp.full(shape, NEG, F32)
    row_bias = []
    for d in range(n_dr):
        t = neg
        for b in range(n_dc):
            t = jnp.where(dc == b, rpb_ref[(head * n_dr + d) * n_dc + b] * LOG2E, t)
        row_bias.append(jnp.where(col_ok, t, NEG))
    nblk = len(geo)
    for typ, blk in enumerate((0, 1, nblk - 1)):
        base = geo[blk][0]
        for i in range(Q_ROWS):
            r = Q_ROWS * blk + i
            rs = min(max(r - NA_KH // 2, 0), rows - NA_KH)
            for wp in range(WIN_ROWS // 2):
                halves = []
                for kr in (base + 2 * wp, base + 2 * wp + 1):
                    halves.append(row_bias[kr - r + NA_KH - 1] if rs <= kr < rs + NA_KH else neg)
                tab_scr[typ, i * GRID_W:(i + 1) * GRID_W, wp * 2 * GRID_W:(wp + 1) * 2 * GRID_W] = (
                    jnp.where(lane < GRID_W, halves[0], halves[1]))


def _nbr_attn_kernel(rpb_ref, q_ref, k_ref, v_ref, gn_ref, kc_ref, vc_ref, qg_ref, kg_ref, o_ref,
                     qn_scr, kn_scr, kcn_scr, vx_scr, vcx_scr, tab_ref, *, geo, rows):
    hd = HEAD_DIM

    @pl.when(pl.program_id(1) == 0)
    def _():
        _build_bias_tables(rpb_ref, tab_ref, pl.program_id(0), geo, rows)
        vx_scr[:, hd:] = jnp.ones((vx_scr.shape[0], hd), BF16)
        vcx_scr[:, hd:] = jnp.ones((vcx_scr.shape[0], hd), BF16)

    qn_scr[...] = _head_rms(q_ref[...], qg_ref[...] * ATT_SCALE).astype(BF16)
    kn_scr[...] = _head_rms(k_ref[...], kg_ref[...]).astype(BF16)
    kcn_scr[...] = _head_rms(kc_ref[...], kg_ref[...]).astype(BF16)
    vx_scr[:, :hd] = v_ref[...].astype(BF16)
    vcx_scr[:, :hd] = vc_ref[...].astype(BF16)
    nq = Q_ROWS * GRID_W
    nk = WIN_ROWS * GRID_W
    def scores(blk):
        base, typ = geo[blk]
        q = qn_scr[blk * nq:(blk + 1) * nq, :]
        s_loc = _dot_nt(q, kn_scr[base * GRID_W:base * GRID_W + nk, :]) + tab_ref[typ]
        return s_loc, _dot_nt(q, kcn_scr[...])

    nxt = scores(0)
    for blk, (base, typ) in enumerate(geo):
        qs = slice(blk * nq, (blk + 1) * nq)
        ks = slice(base * GRID_W, base * GRID_W + nk)
        s_loc, s_ctx = nxt
        if blk + 1 < len(geo):
            nxt = scores(blk + 1)
        m = jnp.maximum(jnp.max(s_loc, axis=-1, keepdims=True), jnp.max(s_ctx, axis=-1, keepdims=True))
        p_loc = jnp.exp2(s_loc - m)
        p_ctx = jnp.exp2(s_ctx - m)
        ox = _dot(p_loc.astype(BF16), vx_scr[ks, :]) + _dot(p_ctx.astype(BF16), vcx_scr[...])
        o_ref[qs, :] = (ox[:, :hd] / ox[:, hd:] * gn_ref[qs, :].astype(F32)).astype(o_ref.dtype)


def _nbr_attention(z, zc, ck, cv, qg, kg, rpb_flat, seq):
    T = z.shape[0]
    rows = seq // GRID_W
    geo = _block_geometry(rows)
    hd = HEAD_DIM
    lat = lambda off: pl.BlockSpec((seq, hd), lambda h, b: (b, off // hd + h))
    return pl.pallas_call(
        functools.partial(_nbr_attn_kernel, geo=geo, rows=rows),
        grid=(NA_HEADS, T // seq),
        in_specs=[
            pl.BlockSpec(memory_space=pltpu.SMEM),
            lat(OFF_Q), lat(OFF_K), lat(OFF_V), lat(OFF_GN),
            pl.BlockSpec((CTX_LEN, hd), lambda h, b: (b, ck + h)),
            pl.BlockSpec((CTX_LEN, hd), lambda h, b: (b, cv + h)),
            pl.BlockSpec((1, hd), lambda h, b: (0, 0)),
            pl.BlockSpec((1, hd), lambda h, b: (0, 0)),
        ],
        out_specs=pl.BlockSpec((seq, hd), lambda h, b: (b, h)),
        out_shape=jax.ShapeDtypeStruct((T, W_C), BF16),
        scratch_shapes=[pltpu.VMEM((seq, hd), BF16), pltpu.VMEM((seq, hd), BF16),
                        pltpu.VMEM((CTX_LEN, hd), BF16),
                        pltpu.VMEM((seq, 2 * hd), BF16), pltpu.VMEM((CTX_LEN, 2 * hd), BF16),
                        pltpu.VMEM((3, Q_ROWS * GRID_W, WIN_ROWS * GRID_W), F32)],
        compiler_params=_params("arbitrary", "arbitrary"),
        name="nbr_attn",
    )(rpb_flat, z, z, z, z, zc, zc, qg, kg)


def _ctx_attn_kernel(q_ref, k_ref, v_ref, gn_ref, qg_ref, kg_ref, o_ref):
    for h in range(NA_HEADS):
        cols = slice(h * HEAD_DIM, (h + 1) * HEAD_DIM)
        q = _head_rms(q_ref[:, cols], qg_ref[...] * ATT_SCALE).astype(BF16)
        k = _head_rms(k_ref[:, cols], kg_ref[...]).astype(BF16)
        s = _dot_nt(q, k)
        p = jnp.exp2(s - jnp.max(s, axis=-1, keepdims=True))
        o = _dot(p.astype(BF16), v_ref[:, cols].astype(BF16)) / jnp.sum(p, axis=-1, keepdims=True)
        o_ref[:, cols] = (o * gn_ref[:, cols].astype(F32)).astype(o_ref.dtype)


def _ctx_attention(zc, qg, kg):
    T = zc.shape[0]
    blk = lambda off: pl.BlockSpec((CTX_LEN, W_C), lambda b: (b, off // W_C))
    vec = pl.BlockSpec((1, HEAD_DIM), lambda b: (0, 0))
    return pl.pallas_call(
        _ctx_attn_kernel,
        grid=(T // CTX_LEN,),
        in_specs=[blk(OFF_Q), blk(OFF_K), blk(OFF_V), blk(OFF_GN), vec, vec],
        out_specs=pl.BlockSpec((CTX_LEN, W_C), lambda b: (b, 0)),
        out_shape=jax.ShapeDtypeStruct((T, W_C), BF16),
        compiler_params=_params("arbitrary"),
        name="ctx_attn",
    )(zc, zc, zc, zc, qg, kg)


def _merge_kernel(u_ref, vg_ref, ga_ref, f_ref, n_ref, ga0, ga1, gf0, gf1, gn0, gn1, x_ref, gate_ref,
                  lng_ref, lnb_ref, ws_ref, bst_ref, wpa_ref, wpf_ref, wpn_ref, wout_ref, o_ref,
                  a_scr, vn_scr, yfn_scr, y_scr):
    f = f_ref[...]
    n = n_ref[...]
    for half, (gf, gn) in enumerate(((gf0, gn0), (gf1, gn1))):
        cols = slice(half * COL, (half + 1) * COL)
        yfn_scr[:, cols] = (gf[...].astype(F32) * _dot(f, wpf_ref[:, cols])
                            + gn[...].astype(F32) * _dot(n, wpn_ref[:, cols]))
    _spatial_gating(u_ref, vg_ref, ga_ref, lng_ref, lnb_ref, ws_ref, bst_ref, a_scr, vn_scr)
    a = a_scr[...]
    for half, ga in enumerate((ga0, ga1)):
        cols = slice(half * COL, (half + 1) * COL)
        y = yfn_scr[:, cols] + ga[...].astype(F32) * _dot(a, wpa_ref[:, cols])
        y_scr[:, cols] = y.astype(BF16)
    o_ref[...] = x_ref[...] + gate_ref[0] * _dot(y_scr[...], wout_ref[...])


def _merge(f, n, z, x2, mod3, lng, lnb, ws, bst, wpa, wpf, wpn, wout, *, rows_per_mod, mod_row0):
    T, D = x2.shape
    tm = 256
    mrow = lambda i: mod_row0 + i // rows_per_mod
    act = pl.BlockSpec((tm, COL), lambda i: (i, 0))
    zblk = lambda c: pl.BlockSpec((tm, COL), lambda i: (i, c))
    const = lambda shape: pl.BlockSpec(shape, lambda i: (0,) * len(shape), pipeline_mode=pl.Buffered(1))
    return pl.pallas_call(
        _merge_kernel,
        grid=(T // tm,),
        in_specs=[
            zblk(OFF_U // COL), zblk(OFF_VG // COL), zblk(OFF_GA // COL),
            act, act,
        ] + [zblk(OFF_MERGE // COL + c) for c in range(3 * D // COL)] + [
            pl.BlockSpec((tm, D), lambda i: (i, 0)),
            pl.BlockSpec((1, 1, D), lambda i: (mrow(i), 0, 2)),
            const((1, W_A)), const((1, W_A)), const((GMLP_GROUPS, CHUNK, CHUNK)), const((CHUNK, GMLP_GROUPS)),
            const((W_A, D)), const((W_B, D)), const((W_C, D)), const((D, D)),
        ],
        out_specs=pl.BlockSpec((tm, D), lambda i: (i, 0)),
        out_shape=jax.ShapeDtypeStruct((T, D), F32),
        scratch_shapes=[pltpu.VMEM((tm, W_A), BF16), pltpu.VMEM((tm, W_A), BF16),
                        pltpu.VMEM((tm, D), F32), pltpu.VMEM((tm, D), BF16)],
        compiler_params=_params("arbitrary"),
        name="merge_out",
    )(z, z, z, f, n, z, z, z, z, z, z, x2, mod3, lng, lnb, ws, bst, wpa, wpf, wpn, wout)


def kernel(x, c, ctx, c_ctx, norm_g, w_ada, b_ada, w_in, gmlp_ln_g, gmlp_ln_b, gmlp_ws, gmlp_bs,
           q_norm_g, k_norm_g, rpb, w_pa, w_pf, w_pn, w_out):
    B, S, D = x.shape
    C = ctx.shape[1]
    L = w_in.shape[0]
    assert D == D_MODEL and C == CTX_LEN and B < MOD_ROWS
    assert S % (GRID_W * Q_ROWS) == 0 and S // GRID_W >= WIN_ROWS

    cc = jnp.concatenate([c, c_ctx[None], jnp.zeros((MOD_ROWS - B - 1, D), F32)], axis=0)
    mod = _modulation(cc, w_ada, b_ada)

    dft_lat = _dft_consts(S)
    dft_ctx = _dft_consts(C)
    x2 = x.reshape(B * S, D)
    c2 = ctx.reshape(B * C, D)
    for l in range(L):
        last = l == L - 1
        mod3 = mod[l].reshape(MOD_ROWS, 1, 3 * D)
        g = norm_g[l].reshape(1, D)
        wpa, wpf, wpn, wout = (t[l].astype(BF16) for t in (w_pa, w_pf, w_pn, w_out))
        lng = gmlp_ln_g[l].reshape(1, W_A)
        lnb = gmlp_ln_b[l].reshape(1, W_A)
        ws = gmlp_ws[l].astype(BF16)
        bst = gmlp_bs[l].T
        qg = q_norm_g[l].reshape(1, HEAD_DIM)
        kg = k_norm_g[l].reshape(1, HEAD_DIM)
        rpb_flat = rpb[l].reshape(-1)

        if last:
            zc = _in_proj(c2, g, mod3, w_in, l, rows_per_mod=1 << 30, mod_row0=B,
                          col0=OFF_K // COL, ncol=2)
            ck, cv = 0, W_C // HEAD_DIM
        else:
            zc = _in_proj(c2, g, mod3, w_in, l, rows_per_mod=1 << 30, mod_row0=B, col0=0,
                          ncol=W_IN // COL)
            ck, cv = OFF_K // HEAD_DIM, OFF_V // HEAD_DIM
        z = _in_proj(x2, g, mod3, w_in, l, rows_per_mod=S // IN_PROJ_ROWS, mod_row0=0, col0=0,
                     ncol=W_IN // COL)

        n = _nbr_attention(z, zc, ck, cv, qg, kg, rpb_flat, S)
        f = _fourier(z, S, dft_lat)
        x_new = _merge(f, n, z, x2, mod3, lng, lnb, ws, bst, wpa, wpf, wpn, wout,
                       rows_per_mod=S // 256, mod_row0=0)

        if not last:
            nc = _ctx_attention(zc, qg, kg)
            fc = _fourier(zc, C, dft_ctx)
            c2 = _merge(fc, nc, zc, c2, mod3, lng, lnb, ws, bst, wpa, wpf, wpn, wout,
                        rows_per_mod=1 << 30, mod_row0=B)
        x2 = x_new
    return x2.reshape(B, S, D)
```

```python
import functools

import jax
import jax.numpy as jnp
import numpy as np
from jax import lax
from jax.experimental import pallas as pl
from jax.experimental.pallas import tpu as pltpu

F32 = jnp.float32
BF16 = jnp.bfloat16

D_MODEL = 2048
CTX_LEN = 256
GRID_W = 64
EPS = 1e-6
CHUNK = 128
GMLP_GROUPS = 8
W_A = D_MODEL // 2
W_B = D_MODEL // 2
F_GROUPS = 4
F_GW = W_B // F_GROUPS
HEAD_DIM = 128
NA_HEADS = 8
W_C = NA_HEADS * HEAD_DIM
NA_KH = 8
NA_KW = 16

OFF_U = 0
OFF_VG = OFF_U + W_A
OFF_GA = OFF_VG + W_A
OFF_F = OFF_GA + W_A
OFF_GF = OFF_F + W_B
OFF_Q = OFF_GF + W_B
OFF_K = OFF_Q + W_C
OFF_V = OFF_K + W_C
OFF_GN = OFF_V + W_C
OFF_MERGE = OFF_GN + W_C
W_IN = OFF_MERGE + 3 * D_MODEL

LANES = 128
IN_PROJ_ROWS = 1024
COL = 1024
MOD_ROWS = 16
VMEM_LIMIT = 56 * 1024 * 1024
Z_DTYPE = jnp.bfloat16

Q_ROWS = 4
WIN_ROWS = 12
NEG = -1e30
LOG2E = float(np.log2(np.e))
ATT_SCALE = HEAD_DIM ** -0.5 * LOG2E


GELU_C1 = float(np.sqrt(2.0 / np.pi))
GELU_C2 = GELU_C1 * 0.044715


def _sigmoid(t):
    return 0.5 * jnp.tanh(0.5 * t) + 0.5


def _silu(t):
    h = 0.5 * t
    return h * jnp.tanh(h) + h


def _gelu(t):
    h = 0.5 * t
    return h * jnp.tanh(t * (GELU_C1 + GELU_C2 * (t * t))) + h


def _dot(a, b):
    return jnp.dot(a, b, preferred_element_type=F32)


def _dot_nt(a, b):
    return lax.dot_general(a, b, (((1,), (1,)), ((), ())), preferred_element_type=F32)


def _params(*sem):
    return pltpu.CompilerParams(dimension_semantics=sem, vmem_limit_bytes=VMEM_LIMIT)


def _mod_kernel(cc_ref, w_ref, b_ref, o_ref):
    s = _silu(cc_ref[...])
    s_hi = s.astype(BF16)
    s_lo = (s - s_hi.astype(F32)).astype(BF16)
    w = w_ref[0]
    w_hi = w.astype(BF16)
    w_lo = (w - w_hi.astype(F32)).astype(BF16)
    o_ref[0] = _dot(s_hi, w_hi) + _dot(s_lo, w_hi) + _dot(s_hi, w_lo) + b_ref[0]


def _modulation(cc, w_ada, b_ada):
    L, D, N = w_ada.shape
    tn = 1024
    return pl.pallas_call(
        _mod_kernel,
        grid=(L, N // tn),
        in_specs=[
            pl.BlockSpec((MOD_ROWS, D), lambda l, j: (0, 0)),
            pl.BlockSpec((1, D, tn), lambda l, j: (l, 0, j)),
            pl.BlockSpec((1, 1, tn), lambda l, j: (l, 0, j)),
        ],
        out_specs=pl.BlockSpec((1, MOD_ROWS, tn), lambda l, j: (l, 0, j)),
        out_shape=jax.ShapeDtypeStruct((L, MOD_ROWS, N), F32),
        compiler_params=_params("arbitrary", "arbitrary"),
        name="adaln_mod",
    )(cc, w_ada, b_ada.reshape(L, 1, N))


def _in_proj_kernel(x_ref, g_ref, sh_ref, sc_ref, w_ref, o_ref, h_scr, *, col0):
    j = pl.program_id(1)

    @pl.when(j == 0)
    def _():
        x = x_ref[...]
        gain = g_ref[...] * (1.0 + sc_ref[0])
        r = lax.rsqrt(jnp.mean(x * x, axis=-1, keepdims=True) + EPS)
        h_scr[...] = (x * r * gain + sh_ref[0]).astype(BF16)

    c = j + col0

    def project(act):
        o_ref[...] = act(_dot(h_scr[...], w_ref[0].astype(BF16))).astype(o_ref.dtype)

    @pl.when(c < OFF_GA // COL)
    def _():
        project(_gelu)

    @pl.when((c == OFF_GA // COL) | (c == OFF_GF // COL) | (c == OFF_GN // COL))
    def _():
        project(_silu)

    @pl.when(c >= OFF_MERGE // COL)
    def _():
        project(_sigmoid)

    @pl.when((c == OFF_F // COL) | ((c >= OFF_Q // COL) & (c < OFF_GN // COL)))
    def _():
        project(lambda z: z)


def _in_proj(x2, g, mod3, w_in, layer, *, rows_per_mod, mod_row0, col0, ncol):
    T, D = x2.shape
    tm = min(IN_PROJ_ROWS, T)
    mrow = lambda i: mod_row0 + i // rows_per_mod
    return pl.pallas_call(
        functools.partial(_in_proj_kernel, col0=col0),
        grid=(T // tm, ncol),
        in_specs=[
            pl.BlockSpec((tm, D), lambda i, j: (i, 0)),
            pl.BlockSpec((1, D), lambda i, j: (0, 0)),
            pl.BlockSpec((1, 1, D), lambda i, j: (mrow(i), 0, 0)),
            pl.BlockSpec((1, 1, D), lambda i, j: (mrow(i), 0, 1)),
            pl.BlockSpec((1, D, COL), lambda i, j: (layer, 0, j + col0)),
        ],
        out_specs=pl.BlockSpec((tm, COL), lambda i, j: (i, j)),
        out_shape=jax.ShapeDtypeStruct((T, ncol * COL), Z_DTYPE),
        scratch_shapes=[pltpu.VMEM((tm, D), BF16)],
        compiler_params=_params("arbitrary", "arbitrary"),
        name="in_proj",
    )(x2, g, mod3, mod3, w_in)


def _spatial_gating(u_ref, vg_ref, ga_ref, lng_ref, lnb_ref, ws_ref, bst_ref, o_ref, vn_scr):
    v = vg_ref[...].astype(F32)
    mu = jnp.mean(v, axis=-1, keepdims=True)
    vc = v - mu
    vn = vc * lax.rsqrt(jnp.mean(vc * vc, axis=-1, keepdims=True) + EPS)
    vn_scr[...] = (vn * lng_ref[...] + lnb_ref[...]).astype(BF16)
    tm = u_ref.shape[0]
    for ch in range(tm // CHUNK):
        rows = slice(ch * CHUNK, (ch + 1) * CHUNK)
        for g in range(GMLP_GROUPS):
            cols = slice(g * CHUNK, (g + 1) * CHUNK)
            sv = _dot(ws_ref[g], vn_scr[rows, cols]) + bst_ref[:, g:g + 1]
            a = u_ref[rows, cols].astype(F32) * sv * ga_ref[rows, cols].astype(F32)
            o_ref[rows, cols] = a.astype(o_ref.dtype)


DFT_RADIX = 4


def _dft_consts(n):
    m = n // DFT_RADIX
    k = np.arange(m, dtype=np.int64)
    ang = 2.0 * np.pi * ((k[:, None] * k[None, :]) % m).astype(np.float64) / m
    cs = np.concatenate([np.cos(ang), np.sin(ang)], axis=0)
    th = 2.0 * np.pi * np.arange(n // 2, dtype=np.float64) / n
    twc = np.broadcast_to(np.cos(th)[:, None], (n // 2, LANES))
    tws = np.broadcast_to(np.sin(th)[:, None], (n // 2, LANES))
    c = np.arange(F_GW, dtype=np.int64)
    angc = 2.0 * np.pi * ((c[:, None] * c[None, :]) % F_GW).astype(np.float64) / F_GW
    bf = lambda t: jnp.asarray(t, dtype=F32).astype(BF16)
    return bf(cs), jnp.asarray(twc, dtype=F32), jnp.asarray(tws, dtype=F32), bf(np.cos(angc)), bf(np.sin(angc))


def _fourier_kernel(f_ref, gf_ref, cs_ref, twc_ref, tws_ref, cc_ref, sc_ref, o_ref, x_scr):
    n = f_ref.shape[0]
    m0 = n // DFT_RADIX
    lane_tiles = F_GW // LANES

    def position_dft(g):
        for t in range(lane_tiles):
            x_scr[g, t] = f_ref[:, g * F_GW + t * LANES:g * F_GW + (t + 1) * LANES].astype(F32)
        xcat = jnp.concatenate(
            [x_scr[g, t, pl.ds(r, m0, stride=DFT_RADIX), :].astype(BF16)
             for r in range(DFT_RADIX) for t in range(lane_tiles)], axis=1)
        return _dot(cs_ref[...], xcat)

    def butterflies(e):
        base = [(e[:m0, r * F_GW:(r + 1) * F_GW], e[m0:, r * F_GW:(r + 1) * F_GW])
                for r in range(DFT_RADIX)]

        def sub_dft(stride, offset):
            if stride == DFT_RADIX:
                return base[offset]
            ec0, es0 = sub_dft(2 * stride, offset)
            ec1, es1 = sub_dft(2 * stride, offset + stride)
            m = n // (2 * stride)
            c = jnp.concatenate([twc_ref[pl.ds(0, m, stride=stride), :]] * lane_tiles, axis=1)
            s = jnp.concatenate([tws_ref[pl.ds(0, m, stride=stride), :]] * lane_tiles, axis=1)
            a = c * ec1 - s * es1
            b = c * es1 + s * ec1
            return (jnp.concatenate([ec0 + a, ec0 - a], axis=0),
                    jnp.concatenate([es0 + b, es0 - b], axis=0))

        return sub_dft(1, 0)

    nxt = position_dft(0)
    for g in range(F_GROUPS):
        e = nxt
        if g + 1 < F_GROUPS:
            nxt = position_dft(g + 1)
        tc, ts = butterflies(e)
        cols = slice(g * F_GW, (g + 1) * F_GW)
        y = _dot(tc.astype(BF16), cc_ref[...]) - _dot(ts.astype(BF16), sc_ref[...])
        o_ref[:, cols] = (y * ((n * F_GW) ** -0.5) * gf_ref[:, cols].astype(F32)).astype(o_ref.dtype)


def _fourier(z, n, consts):
    T = z.shape[0]
    const = lambda t: pl.BlockSpec(t.shape, lambda b: (0, 0), pipeline_mode=pl.Buffered(1))
    return pl.pallas_call(
        _fourier_kernel,
        grid=(T // n,),
        in_specs=[
            pl.BlockSpec((n, W_B), lambda b: (b, OFF_F // W_B)),
            pl.BlockSpec((n, W_B), lambda b: (b, OFF_GF // W_B)),
        ] + [const(t) for t in consts],
        out_specs=pl.BlockSpec((n, W_B), lambda b: (b, 0)),
        out_shape=jax.ShapeDtypeStruct((T, W_B), BF16),
        scratch_shapes=[pltpu.VMEM((F_GROUPS, F_GW // LANES, n, LANES), F32)],
        compiler_params=_params("arbitrary"),
        name="fourier",
    )(z, z, *consts)


def _head_rms(t, g):
    t = t.astype(F32)
    return t * lax.rsqrt(jnp.mean(t * t, axis=-1, keepdims=True) + EPS) * g


def _block_geometry(rows):
    geo = []
    nblk = rows // Q_ROWS
    for blk in range(nblk):
        base = min(max(Q_ROWS * blk - NA_KH // 2, 0), rows - WIN_ROWS)
        typ = 0 if blk == 0 else (2 if blk == nblk - 1 else 1)
        geo.append((base, typ))
    return geo


def _build_bias_tables(rpb_ref, tab_scr, head, geo, rows):
    n_dr, n_dc = 2 * NA_KH - 1, 2 * NA_KW - 1
    shape = (GRID_W, 2 * GRID_W)
    lane = lax.broadcasted_iota(jnp.int32, shape, 1)
    cq = lax.broadcasted_iota(jnp.int32, shape, 0)
    kc = lane & (GRID_W - 1)
    dc = kc - cq + (NA_KW - 1)
    cs = jnp.clip(cq - NA_KW // 2, 0, GRID_W - NA_KW)
    col_ok = (kc >= cs) & (kc < cs + NA_KW)
    neg = jnp.full(shape, NEG, F32)
    row_bias = []
    for d in range(n_dr):
        t = neg
        for b in range(n_dc):
            t = jnp.where(dc == b, rpb_ref[(head * n_dr + d) * n_dc + b] * LOG2E, t)
        row_bias.append(jnp.where(col_ok, t, NEG))
    nblk = len(geo)
    for typ, blk in enumerate((0, 1, nblk - 1)):
        base = geo[blk][0]
        for i in range(Q_ROWS):
            r = Q_ROWS * blk + i
            rs = min(max(r - NA_KH // 2, 0), rows - NA_KH)
            for wp in range(WIN_ROWS // 2):
                halves = []
                for kr in (base + 2 * wp, base + 2 * wp + 1):
                    halves.append(row_bias[kr - r + NA_KH - 1] if rs <= kr < rs + NA_KH else neg)
                tab_scr[typ, i * GRID_W:(i + 1) * GRID_W, wp * 2 * GRID_W:(wp + 1) * 2 * GRID_W] = (
                    jnp.where(lane < GRID_W, halves[0], halves[1]))


def _nbr_attn_kernel(rpb_ref, q_ref, k_ref, v_ref, gn_ref, kc_ref, vc_ref, qg_ref, kg_ref, o_ref,
                     qn_scr, kn_scr, kcn_scr, vx_scr, vcx_scr, tab_ref, *, geo, rows):
    hd = HEAD_DIM

    @pl.when(pl.program_id(1) == 0)
    def _():
        _build_bias_tables(rpb_ref, tab_ref, pl.program_id(0), geo, rows)
        vx_scr[:, hd:] = jnp.ones((vx_scr.shape[0], hd), BF16)
        vcx_scr[:, hd:] = jnp.ones((vcx_scr.shape[0], hd), BF16)

    qn_scr[...] = _head_rms(q_ref[...], qg_ref[...] * ATT_SCALE).astype(BF16)
    kn_scr[...] = _head_rms(k_ref[...], kg_ref[...]).astype(BF16)
    kcn_scr[...] = _head_rms(kc_ref[...], kg_ref[...]).astype(BF16)
    vx_scr[:, :hd] = v_ref[...].astype(BF16)
    vcx_scr[:, :hd] = vc_ref[...].astype(BF16)
    nq = Q_ROWS * GRID_W
    nk = WIN_ROWS * GRID_W
    def scores(blk):
        base, typ = geo[blk]
        q = qn_scr[blk * nq:(blk + 1) * nq, :]
        s_loc = _dot_nt(q, kn_scr[base * GRID_W:base * GRID_W + nk, :]) + tab_ref[typ]
        return s_loc, _dot_nt(q, kcn_scr[...])

    nxt = scores(0)
    for blk, (base, typ) in enumerate(geo):
        qs = slice(blk * nq, (blk + 1) * nq)
        ks = slice(base * GRID_W, base * GRID_W + nk)
        s_loc, s_ctx = nxt
        if blk + 1 < len(geo):
            nxt = scores(blk + 1)
        m = jnp.maximum(jnp.max(s_loc, axis=-1, keepdims=True), jnp.max(s_ctx, axis=-1, keepdims=True))
        p_loc = jnp.exp2(s_loc - m)
        p_ctx = jnp.exp2(s_ctx - m)
        ox = _dot(p_loc.astype(BF16), vx_scr[ks, :]) + _dot(p_ctx.astype(BF16), vcx_scr[...])
        o_ref[qs, :] = (ox[:, :hd] / ox[:, hd:] * gn_ref[qs, :].astype(F32)).astype(o_ref.dtype)


def _nbr_attention(z, zc, ck, cv, qg, kg, rpb_flat, seq):
    T = z.shape[0]
    rows = seq // GRID_W
    geo = _block_geometry(rows)
    hd = HEAD_DIM
    lat = lambda off: pl.BlockSpec((seq, hd), lambda h, b: (b, off // hd + h))
    return pl.pallas_call(
        functools.partial(_nbr_attn_kernel, geo=geo, rows=rows),
        grid=(NA_HEADS, T // seq),
        in_specs=[
            pl.BlockSpec(memory_space=pltpu.SMEM),
            lat(OFF_Q), lat(OFF_K), lat(OFF_V), lat(OFF_GN),
            pl.BlockSpec((CTX_LEN, hd), lambda h, b: (b, ck + h)),
            pl.BlockSpec((CTX_LEN, hd), lambda h, b: (b, cv + h)),
            pl.BlockSpec((1, hd), lambda h, b: (0, 0)),
            pl.BlockSpec((1, hd), lambda h, b: (0, 0)),
        ],
        out_specs=pl.BlockSpec((seq, hd), lambda h, b: (b, h)),
        out_shape=jax.ShapeDtypeStruct((T, W_C), BF16),
        scratch_shapes=[pltpu.VMEM((seq, hd), BF16), pltpu.VMEM((seq, hd), BF16),
                        pltpu.VMEM((CTX_LEN, hd), BF16),
                        pltpu.VMEM((seq, 2 * hd), BF16), pltpu.VMEM((CTX_LEN, 2 * hd), BF16),
                        pltpu.VMEM((3, Q_ROWS * GRID_W, WIN_ROWS * GRID_W), F32)],
        compiler_params=_params("arbitrary", "arbitrary"),
        name="nbr_attn",
    )(rpb_flat, z, z, z, z, zc, zc, qg, kg)


def _ctx_attn_kernel(q_ref, k_ref, v_ref, gn_ref, qg_ref, kg_ref, o_ref):
    for h in range(NA_HEADS):
        cols = slice(h * HEAD_DIM, (h + 1) * HEAD_DIM)
        q = _head_rms(q_ref[:, cols], qg_ref[...] * ATT_SCALE).astype(BF16)
        k = _head_rms(k_ref[:, cols], kg_ref[...]).astype(BF16)
        s = _dot_nt(q, k)
        p = jnp.exp2(s - jnp.max(s, axis=-1, keepdims=True))
        o = _dot(p.astype(BF16), v_ref[:, cols].astype(BF16)) / jnp.sum(p, axis=-1, keepdims=True)
        o_ref[:, cols] = (o * gn_ref[:, cols].astype(F32)).astype(o_ref.dtype)


def _ctx_attention(zc, qg, kg):
    T = zc.shape[0]
    blk = lambda off: pl.BlockSpec((CTX_LEN, W_C), lambda b: (b, off // W_C))
    vec = pl.BlockSpec((1, HEAD_DIM), lambda b: (0, 0))
    return pl.pallas_call(
        _ctx_attn_kernel,
        grid=(T // CTX_LEN,),
        in_specs=[blk(OFF_Q), blk(OFF_K), blk(OFF_V), blk(OFF_GN), vec, vec],
        out_specs=pl.BlockSpec((CTX_LEN, W_C), lambda b: (b, 0)),
        out_shape=jax.ShapeDtypeStruct((T, W_C), BF16),
        compiler_params=_params("arbitrary"),
        name="ctx_attn",
    )(zc, zc, zc, zc, qg, kg)


def _merge_kernel(u_ref, vg_ref, ga_ref, f_ref, n_ref, ga0, ga1, gf0, gf1, gn0, gn1, x_ref, gate_ref,
                  lng_ref, lnb_ref, ws_ref, bst_ref, wpa_ref, wpf_ref, wpn_ref, wout_ref, o_ref,
                  a_scr, vn_scr, yfn_scr, y_scr):
    f = f_ref[...]
    n = n_ref[...]
    for half, (gf, gn) in enumerate(((gf0, gn0), (gf1, gn1))):
        cols = slice(half * COL, (half + 1) * COL)
        yfn_scr[:, cols] = (gf[...].astype(F32) * _dot(f, wpf_ref[:, cols])
                            + gn[...].astype(F32) * _dot(n, wpn_ref[:, cols]))
    _spatial_gating(u_ref, vg_ref, ga_ref, lng_ref, lnb_ref, ws_ref, bst_ref, a_scr, vn_scr)
    a = a_scr[...]
    for half, ga in enumerate((ga0, ga1)):
        cols = slice(half * COL, (half + 1) * COL)
        y = yfn_scr[:, cols] + ga[...].astype(F32) * _dot(a, wpa_ref[:, cols])
        y_scr[:, cols] = y.astype(BF16)
    o_ref[...] = x_ref[...] + gate_ref[0] * _dot(y_scr[...], wout_ref[...])


def _merge(f, n, z, x2, mod3, lng, lnb, ws, bst, wpa, wpf, wpn, wout, *, rows_per_mod, mod_row0):
    T, D = x2.shape
    tm = 256
    mrow = lambda i: mod_row0 + i // rows_per_mod
    act = pl.BlockSpec((tm, COL), lambda i: (i, 0))
    zblk = lambda c: pl.BlockSpec((tm, COL), lambda i: (i, c))
    const = lambda shape: pl.BlockSpec(shape, lambda i: (0,) * len(shape), pipeline_mode=pl.Buffered(1))
    return pl.pallas_call(
        _merge_kernel,
        grid=(T // tm,),
        in_specs=[
            zblk(OFF_U // COL), zblk(OFF_VG // COL), zblk(OFF_GA // COL),
            act, act,
        ] + [zblk(OFF_MERGE // COL + c) for c in range(3 * D // COL)] + [
            pl.BlockSpec((tm, D), lambda i: (i, 0)),
            pl.BlockSpec((1, 1, D), lambda i: (mrow(i), 0, 2)),
            const((1, W_A)), const((1, W_A)), const((GMLP_GROUPS, CHUNK, CHUNK)), const((CHUNK, GMLP_GROUPS)),
            const((W_A, D)), const((W_B, D)), const((W_C, D)), const((D, D)),
        ],
        out_specs=pl.BlockSpec((tm, D), lambda i: (i, 0)),
        out_shape=jax.ShapeDtypeStruct((T, D), F32),
        scratch_shapes=[pltpu.VMEM((tm, W_A), BF16), pltpu.VMEM((tm, W_A), BF16),
                        pltpu.VMEM((tm, D), F32), pltpu.VMEM((tm, D), BF16)],
        compiler_params=_params("arbitrary"),
        name="merge_out",
    )(z, z, z, f, n, z, z, z, z, z, z, x2, mod3, lng, lnb, ws, bst, wpa, wpf, wpn, wout)


def kernel(x, c, ctx, c_ctx, norm_g, w_ada, b_ada, w_in, gmlp_ln_g, gmlp_ln_b, gmlp_ws, gmlp_bs,
           q_norm_g, k_norm_g, rpb, w_pa, w_pf, w_pn, w_out):
    B, S, D = x.shape
    C = ctx.shape[1]
    L = w_in.shape[0]
    assert D == D_MODEL and C == CTX_LEN and B < MOD_ROWS
    assert S % (GRID_W * Q_ROWS) == 0 and S // GRID_W >= WIN_ROWS

    cc = jnp.concatenate([c, c_ctx[None], jnp.zeros((MOD_ROWS - B - 1, D), F32)], axis=0)
    mod = _modulation(cc, w_ada, b_ada)

    dft_lat = _dft_consts(S)
    dft_ctx = _dft_consts(C)
    x2 = x.reshape(B * S, D)
    c2 = ctx.reshape(B * C, D)
    for l in range(L):
        last = l == L - 1
        mod3 = mod[l].reshape(MOD_ROWS, 1, 3 * D)
        g = norm_g[l].reshape(1, D)
        wpa, wpf, wpn, wout = (t[l].astype(BF16) for t in (w_pa, w_pf, w_pn, w_out))
        lng = gmlp_ln_g[l].reshape(1, W_A)
        lnb = gmlp_ln_b[l].reshape(1, W_A)
        ws = gmlp_ws[l].astype(BF16)
        bst = gmlp_bs[l].T
        qg = q_norm_g[l].reshape(1, HEAD_DIM)
        kg = k_norm_g[l].reshape(1, HEAD_DIM)
        rpb_flat = rpb[l].reshape(-1)

        if last:
            zc = _in_proj(c2, g, mod3, w_in, l, rows_per_mod=1 << 30, mod_row0=B,
                          col0=OFF_K // COL, ncol=2)
            ck, cv = 0, W_C // HEAD_DIM
        else:
            zc = _in_proj(c2, g, mod3, w_in, l, rows_per_mod=1 << 30, mod_row0=B, col0=0,
                          ncol=W_IN // COL)
            ck, cv = OFF_K // HEAD_DIM, OFF_V // HEAD_DIM
        z = _in_proj(x2, g, mod3, w_in, l, rows_per_mod=S // IN_PROJ_ROWS, mod_row0=0, col0=0,
                     ncol=W_IN // COL)

        n = _nbr_attention(z, zc, ck, cv, qg, kg, rpb_flat, S)
        f = _fourier(z, S, dft_lat)
        x_new = _merge(f, n, z, x2, mod3, lng, lnb, ws, bst, wpa, wpf, wpn, wout,
                       rows_per_mod=S // 256, mod_row0=0)

        if not last:
            nc = _ctx_attention(zc, qg, kg)
            fc = _fourier(zc, C, dft_ctx)
            c2 = _merge(fc, nc, zc, c2, mod3, lng, lnb, ws, bst, wpa, wpf, wpn, wout,
                        rows_per_mod=1 << 30, mod_row0=B)
        x2 = x_new
    return x2.reshape(B, S, D)
```

```python
import functools

import jax
import jax.numpy as jnp
import numpy as np
from jax import lax
from jax.experimental import pallas as pl
from jax.experimental.pallas import tpu as pltpu

F32 = jnp.float32
BF16 = jnp.bfloat16

D_MODEL = 2048
CTX_LEN = 256
GRID_W = 64
EPS = 1e-6
CHUNK = 128
GMLP_GROUPS = 8
W_A = D_MODEL // 2
W_B = D_MODEL // 2
F_GROUPS = 4
F_GW = W_B // F_GROUPS
HEAD_DIM = 128
NA_HEADS = 8
W_C = NA_HEADS * HEAD_DIM
NA_KH = 8
NA_KW = 16

OFF_U = 0
OFF_VG = OFF_U + W_A
OFF_GA = OFF_VG + W_A
OFF_F = OFF_GA + W_A
OFF_GF = OFF_F + W_B
OFF_Q = OFF_GF + W_B
OFF_K = OFF_Q + W_C
OFF_V = OFF_K + W_C
OFF_GN = OFF_V + W_C
OFF_MERGE = OFF_GN + W_C
W_IN = OFF_MERGE + 3 * D_MODEL

LANES = 128
IN_PROJ_ROWS = 1024
MERGE_ROWS = 256
COL = 1024
MOD_ROWS = 16
ONE_ROW = 1 << 30
VMEM_LIMIT = 56 * 1024 * 1024
Z_DTYPE = jnp.bfloat16

Q_ROWS = 4
WIN_ROWS = 12
NEG = -1e30
LOG2E = float(np.log2(np.e))
ATT_SCALE = HEAD_DIM ** -0.5 * LOG2E


GELU_C1 = float(np.sqrt(2.0 / np.pi))
GELU_C2 = GELU_C1 * 0.044715


def _sigmoid(t):
    return 0.5 * jnp.tanh(0.5 * t) + 0.5


def _silu(t):
    h = 0.5 * t
    return h * jnp.tanh(h) + h


def _gelu(t):
    h = 0.5 * t
    return h * jnp.tanh(t * (GELU_C1 + GELU_C2 * (t * t))) + h


def _dot(a, b):
    return jnp.dot(a, b, preferred_element_type=F32)


def _dot_nt(a, b):
    return lax.dot_general(a, b, (((1,), (1,)), ((), ())), preferred_element_type=F32)


def _params(*sem):
    return pltpu.CompilerParams(dimension_semantics=sem, vmem_limit_bytes=VMEM_LIMIT)


def _mod_kernel(cc_ref, w_ref, b_ref, o_ref):
    s = _silu(cc_ref[...])
    s_hi = s.astype(BF16)
    s_lo = (s - s_hi.astype(F32)).astype(BF16)
    w = w_ref[0]
    w_hi = w.astype(BF16)
    w_lo = (w - w_hi.astype(F32)).astype(BF16)
    o_ref[0] = _dot(s_hi, w_hi) + _dot(s_lo, w_hi) + _dot(s_hi, w_lo) + b_ref[0]


def _modulation(cc, w_ada, b_ada):
    L, D, N = w_ada.shape
    tn = COL
    return pl.pallas_call(
        _mod_kernel,
        grid=(L, N // tn),
        in_specs=[
            pl.BlockSpec((MOD_ROWS, D), lambda l, j: (0, 0)),
            pl.BlockSpec((1, D, tn), lambda l, j: (l, 0, j)),
            pl.BlockSpec((1, 1, tn), lambda l, j: (l, 0, j)),
        ],
        out_specs=pl.BlockSpec((1, MOD_ROWS, tn), lambda l, j: (l, 0, j)),
        out_shape=jax.ShapeDtypeStruct((L, MOD_ROWS, N), F32),
        compiler_params=_params("arbitrary", "arbitrary"),
        name="adaln_mod",
    )(cc, w_ada, b_ada.reshape(L, 1, N))


def _in_proj_kernel(x_ref, g_ref, sh_ref, sc_ref, w_ref, o_ref, h_scr, *, col0):
    j = pl.program_id(1)

    @pl.when(j == 0)
    def _():
        x = x_ref[...]
        gain = g_ref[...] * (1.0 + sc_ref[0])
        r = lax.rsqrt(jnp.mean(x * x, axis=-1, keepdims=True) + EPS)
        h_scr[...] = (x * r * gain + sh_ref[0]).astype(BF16)

    c = j + col0

    def project(act):
        o_ref[...] = act(_dot(h_scr[...], w_ref[0].astype(BF16))).astype(o_ref.dtype)

    @pl.when(c < OFF_GA // COL)
    def _():
        project(_gelu)

    @pl.when((c == OFF_GA // COL) | (c == OFF_GF // COL) | (c == OFF_GN // COL))
    def _():
        project(_silu)

    @pl.when(c >= OFF_MERGE // COL)
    def _():
        project(_sigmoid)

    @pl.when((c == OFF_F // COL) | ((c >= OFF_Q // COL) & (c < OFF_GN // COL)))
    def _():
        project(lambda z: z)


def _in_proj(x2, g, mod3, w_in, layer, *, rows_per_mod, mod_row0, col0, ncol):
    T, D = x2.shape
    tm = min(IN_PROJ_ROWS, T)
    mrow = lambda i: mod_row0 + i // rows_per_mod
    return pl.pallas_call(
        functools.partial(_in_proj_kernel, col0=col0),
        grid=(T // tm, ncol),
        in_specs=[
            pl.BlockSpec((tm, D), lambda i, j: (i, 0)),
            pl.BlockSpec((1, D), lambda i, j: (0, 0)),
            pl.BlockSpec((1, 1, D), lambda i, j: (mrow(i), 0, 0)),
            pl.BlockSpec((1, 1, D), lambda i, j: (mrow(i), 0, 1)),
            pl.BlockSpec((1, D, COL), lambda i, j: (layer, 0, j + col0)),
        ],
        out_specs=pl.BlockSpec((tm, COL), lambda i, j: (i, j)),
        out_shape=jax.ShapeDtypeStruct((T, ncol * COL), Z_DTYPE),
        scratch_shapes=[pltpu.VMEM((tm, D), BF16)],
        compiler_params=_params("arbitrary", "arbitrary"),
        name="in_proj",
    )(x2, g, mod3, mod3, w_in)


def _spatial_gating(u_ref, vg_ref, ga_ref, lng_ref, lnb_ref, ws_ref, bst_ref, o_ref, vn_scr):
    v = vg_ref[...].astype(F32)
    mu = jnp.mean(v, axis=-1, keepdims=True)
    vc = v - mu
    vn = vc * lax.rsqrt(jnp.mean(vc * vc, axis=-1, keepdims=True) + EPS)
    vn_scr[...] = (vn * lng_ref[...] + lnb_ref[...]).astype(BF16)
    tm = u_ref.shape[0]
    for ch in range(tm // CHUNK):
        rows = slice(ch * CHUNK, (ch + 1) * CHUNK)
        for g in range(GMLP_GROUPS):
            cols = slice(g * CHUNK, (g + 1) * CHUNK)
            sv = _dot(ws_ref[g], vn_scr[rows, cols]) + bst_ref[:, g:g + 1]
            a = u_ref[rows, cols].astype(F32) * sv * ga_ref[rows, cols].astype(F32)
            o_ref[rows, cols] = a.astype(o_ref.dtype)


DFT_RADIX = 4


def _dft_consts(n):
    m = n // DFT_RADIX
    k = np.arange(m, dtype=np.int64)
    ang = 2.0 * np.pi * ((k[:, None] * k[None, :]) % m).astype(np.float64) / m
    cs = np.concatenate([np.cos(ang), np.sin(ang)], axis=0)
    th = 2.0 * np.pi * np.arange(n // 2, dtype=np.float64) / n
    twc = np.broadcast_to(np.cos(th)[:, None], (n // 2, LANES))
    tws = np.broadcast_to(np.sin(th)[:, None], (n // 2, LANES))
    c = np.arange(F_GW, dtype=np.int64)
    angc = 2.0 * np.pi * ((c[:, None] * c[None, :]) % F_GW).astype(np.float64) / F_GW
    bf = lambda t: jnp.asarray(t, dtype=F32).astype(BF16)
    return bf(cs), jnp.asarray(twc, dtype=F32), jnp.asarray(tws, dtype=F32), bf(np.cos(angc)), bf(np.sin(angc))


def _fourier_kernel(f_ref, gf_ref, cs_ref, twc_ref, tws_ref, cc_ref, sc_ref, o_ref, x_scr):
    n = f_ref.shape[0]
    m0 = n // DFT_RADIX
    lane_tiles = F_GW // LANES

    def position_dft(g):
        for t in range(lane_tiles):
            x_scr[g, t] = f_ref[:, g * F_GW + t * LANES:g * F_GW + (t + 1) * LANES].astype(F32)
        xcat = jnp.concatenate(
            [x_scr[g, t, pl.ds(r, m0, stride=DFT_RADIX), :].astype(BF16)
             for r in range(DFT_RADIX) for t in range(lane_tiles)], axis=1)
        return _dot(cs_ref[...], xcat)

    def butterflies(e):
        base = [(e[:m0, r * F_GW:(r + 1) * F_GW], e[m0:, r * F_GW:(r + 1) * F_GW])
                for r in range(DFT_RADIX)]

        def sub_dft(stride, offset):
            if stride == DFT_RADIX:
                return base[offset]
            ec0, es0 = sub_dft(2 * stride, offset)
            ec1, es1 = sub_dft(2 * stride, offset + stride)
            m = n // (2 * stride)
            c = jnp.concatenate([twc_ref[pl.ds(0, m, stride=stride), :]] * lane_tiles, axis=1)
            s = jnp.concatenate([tws_ref[pl.ds(0, m, stride=stride), :]] * lane_tiles, axis=1)
            a = c * ec1 - s * es1
            b = c * es1 + s * ec1
            return (jnp.concatenate([ec0 + a, ec0 - a], axis=0),
                    jnp.concatenate([es0 + b, es0 - b], axis=0))

        return sub_dft(1, 0)

    nxt = position_dft(0)
    for g in range(F_GROUPS):
        e = nxt
        if g + 1 < F_GROUPS:
            nxt = position_dft(g + 1)
        tc, ts = butterflies(e)
        cols = slice(g * F_GW, (g + 1) * F_GW)
        y = _dot(tc.astype(BF16), cc_ref[...]) - _dot(ts.astype(BF16), sc_ref[...])
        o_ref[:, cols] = (y * ((n * F_GW) ** -0.5) * gf_ref[:, cols].astype(F32)).astype(o_ref.dtype)


def _fourier(z, n, consts):
    T = z.shape[0]
    const = lambda t: pl.BlockSpec(t.shape, lambda b: (0, 0), pipeline_mode=pl.Buffered(1))
    return pl.pallas_call(
        _fourier_kernel,
        grid=(T // n,),
        in_specs=[
            pl.BlockSpec((n, W_B), lambda b: (b, OFF_F // W_B)),
            pl.BlockSpec((n, W_B), lambda b: (b, OFF_GF // W_B)),
        ] + [const(t) for t in consts],
        out_specs=pl.BlockSpec((n, W_B), lambda b: (b, 0)),
        out_shape=jax.ShapeDtypeStruct((T, W_B), BF16),
        scratch_shapes=[pltpu.VMEM((F_GROUPS, F_GW // LANES, n, LANES), F32)],
        compiler_params=_params("arbitrary"),
        name="fourier",
    )(z, z, *consts)


def _head_rms(t, g):
    t = t.astype(F32)
    return t * lax.rsqrt(jnp.mean(t * t, axis=-1, keepdims=True) + EPS) * g


def _block_geometry(rows):
    assert Q_ROWS == NA_KH // 2 and rows % Q_ROWS == 0 and rows // Q_ROWS >= 3
    assert WIN_ROWS % 2 == 0 and Q_ROWS + NA_KH - 1 <= WIN_ROWS <= rows
    geo = []
    nblk = rows // Q_ROWS
    for blk in range(nblk):
        base = min(max(Q_ROWS * blk - NA_KH // 2, 0), rows - WIN_ROWS)
        typ = 0 if blk == 0 else (2 if blk == nblk - 1 else 1)
        geo.append((base, typ))
    return geo


def _build_bias_tables(rpb_ref, tab_scr, head, geo, rows):
    n_dr, n_dc = 2 * NA_KH - 1, 2 * NA_KW - 1
    shape = (GRID_W, 2 * GRID_W)
    lane = lax.broadcasted_iota(jnp.int32, shape, 1)
    cq = lax.broadcasted_iota(jnp.int32, shape, 0)
    kc = lane & (GRID_W - 1)
    dc = kc - cq + (NA_KW - 1)
    cs = jnp.clip(cq - NA_KW // 2, 0, GRID_W - NA_KW)
    col_ok = (kc >= cs) & (kc < cs + NA_KW)
    neg = jnp.full(shape, NEG, F32)
    row_bias = []
    for d in range(n_dr):
        t = neg
        for b in range(n_dc):
            t = jnp.where(dc == b, rpb_ref[(head * n_dr + d) * n_dc + b] * LOG2E, t)
        row_bias.append(jnp.where(col_ok, t, NEG))
    nblk = len(geo)
    for typ, blk in enumerate((0, 1, nblk - 1)):
        base = geo[blk][0]
        for i in range(Q_ROWS):
            r = Q_ROWS * blk + i
            rs = min(max(r - NA_KH // 2, 0), rows - NA_KH)
            for wp in range(WIN_ROWS // 2):
                halves = []
                for kr in (base + 2 * wp, base + 2 * wp + 1):
                    halves.append(row_bias[kr - r + NA_KH - 1] if rs <= kr < rs + NA_KH else neg)
                tab_scr[typ, i * GRID_W:(i + 1) * GRID_W, wp * 2 * GRID_W:(wp + 1) * 2 * GRID_W] = (
                    jnp.where(lane < GRID_W, halves[0], halves[1]))


def _nbr_attn_kernel(rpb_ref, q_ref, k_ref, v_ref, gn_ref, kc_ref, vc_ref, qg_ref, kg_ref, o_ref,
                     qn_scr, kn_scr, kcn_scr, vx_scr, vcx_scr, tab_ref, *, geo, rows):
    hd = HEAD_DIM

    @pl.when(pl.program_id(1) == 0)
    def _():
        _build_bias_tables(rpb_ref, tab_ref, pl.program_id(0), geo, rows)
        vx_scr[:, hd:] = jnp.ones((vx_scr.shape[0], hd), BF16)
        vcx_scr[:, hd:] = jnp.ones((vcx_scr.shape[0], hd), BF16)

    qn_scr[...] = _head_rms(q_ref[...], qg_ref[...] * ATT_SCALE).astype(BF16)
    kn_scr[...] = _head_rms(k_ref[...], kg_ref[...]).astype(BF16)
    kcn_scr[...] = _head_rms(kc_ref[...], kg_ref[...]).astype(BF16)
    vx_scr[:, :hd] = v_ref[...].astype(BF16)
    vcx_scr[:, :hd] = vc_ref[...].astype(BF16)
    nq = Q_ROWS * GRID_W
    nk = WIN_ROWS * GRID_W

    def scores(blk):
        base, typ = geo[blk]
        q = qn_scr[blk * nq:(blk + 1) * nq, :]
        s_loc = _dot_nt(q, kn_scr[base * GRID_W:base * GRID_W + nk, :]) + tab_ref[typ]
        return s_loc, _dot_nt(q, kcn_scr[...])

    nxt = scores(0)
    for blk, (base, typ) in enumerate(geo):
        qs = slice(blk * nq, (blk + 1) * nq)
        ks = slice(base * GRID_W, base * GRID_W + nk)
        s_loc, s_ctx = nxt
        if blk + 1 < len(geo):
            nxt = scores(blk + 1)
        m = jnp.maximum(jnp.max(s_loc, axis=-1, keepdims=True), jnp.max(s_ctx, axis=-1, keepdims=True))
        p_loc = jnp.exp2(s_loc - m)
        p_ctx = jnp.exp2(s_ctx - m)
        ox = _dot(p_loc.astype(BF16), vx_scr[ks, :]) + _dot(p_ctx.astype(BF16), vcx_scr[...])
        o_ref[qs, :] = (ox[:, :hd] / ox[:, hd:] * gn_ref[qs, :].astype(F32)).astype(o_ref.dtype)


def _nbr_attention(z, zc, ck, cv, qg, kg, rpb_flat, seq):
    T = z.shape[0]
    rows = seq // GRID_W
    geo = _block_geometry(rows)
    hd = HEAD_DIM
    lat = lambda off: pl.BlockSpec((seq, hd), lambda h, b: (b, off // hd + h))
    return pl.pallas_call(
        functools.partial(_nbr_attn_kernel, geo=geo, rows=rows),
        grid=(NA_HEADS, T // seq),
        in_specs=[
            pl.BlockSpec(memory_space=pltpu.SMEM),
            lat(OFF_Q), lat(OFF_K), lat(OFF_V), lat(OFF_GN),
            pl.BlockSpec((CTX_LEN, hd), lambda h, b: (b, ck + h)),
            pl.BlockSpec((CTX_LEN, hd), lambda h, b: (b, cv + h)),
            pl.BlockSpec((1, hd), lambda h, b: (0, 0)),
            pl.BlockSpec((1, hd), lambda h, b: (0, 0)),
        ],
        out_specs=pl.BlockSpec((seq, hd), lambda h, b: (b, h)),
        out_shape=jax.ShapeDtypeStruct((T, W_C), BF16),
        scratch_shapes=[pltpu.VMEM((seq, hd), BF16), pltpu.VMEM((seq, hd), BF16),
                        pltpu.VMEM((CTX_LEN, hd), BF16),
                        pltpu.VMEM((seq, 2 * hd), BF16), pltpu.VMEM((CTX_LEN, 2 * hd), BF16),
                        pltpu.VMEM((3, Q_ROWS * GRID_W, WIN_ROWS * GRID_W), F32)],
        compiler_params=_params("arbitrary", "arbitrary"),
        name="nbr_attn",
    )(rpb_flat, z, z, z, z, zc, zc, qg, kg)


def _ctx_attn_kernel(q_ref, k_ref, v_ref, gn_ref, qg_ref, kg_ref, o_ref):
    for h in range(NA_HEADS):
        cols = slice(h * HEAD_DIM, (h + 1) * HEAD_DIM)
        q = _head_rms(q_ref[:, cols], qg_ref[...] * ATT_SCALE).astype(BF16)
        k = _head_rms(k_ref[:, cols], kg_ref[...]).astype(BF16)
        s = _dot_nt(q, k)
        p = jnp.exp2(s - jnp.max(s, axis=-1, keepdims=True))
        o = _dot(p.astype(BF16), v_ref[:, cols].astype(BF16)) / jnp.sum(p, axis=-1, keepdims=True)
        o_ref[:, cols] = (o * gn_ref[:, cols].astype(F32)).astype(o_ref.dtype)


def _ctx_attention(zc, qg, kg):
    T = zc.shape[0]
    blk = lambda off: pl.BlockSpec((CTX_LEN, W_C), lambda b: (b, off // W_C))
    vec = pl.BlockSpec((1, HEAD_DIM), lambda b: (0, 0))
    return pl.pallas_call(
        _ctx_attn_kernel,
        grid=(T // CTX_LEN,),
        in_specs=[blk(OFF_Q), blk(OFF_K), blk(OFF_V), blk(OFF_GN), vec, vec],
        out_specs=pl.BlockSpec((CTX_LEN, W_C), lambda b: (b, 0)),
        out_shape=jax.ShapeDtypeStruct((T, W_C), BF16),
        compiler_params=_params("arbitrary"),
        name="ctx_attn",
    )(zc, zc, zc, zc, qg, kg)


def _merge_kernel(u_ref, vg_ref, ga_ref, f_ref, n_ref, ga0, ga1, gf0, gf1, gn0, gn1, x_ref, gate_ref,
                  lng_ref, lnb_ref, ws_ref, bst_ref, wpa_ref, wpf_ref, wpn_ref, wout_ref, o_ref,
                  a_scr, vn_scr, yfn_scr, y_scr):
    f = f_ref[...]
    n = n_ref[...]
    for half, (gf, gn) in enumerate(((gf0, gn0), (gf1, gn1))):
        cols = slice(half * COL, (half + 1) * COL)
        yfn_scr[:, cols] = (gf[...].astype(F32) * _dot(f, wpf_ref[:, cols])
                            + gn[...].astype(F32) * _dot(n, wpn_ref[:, cols]))
    _spatial_gating(u_ref, vg_ref, ga_ref, lng_ref, lnb_ref, ws_ref, bst_ref, a_scr, vn_scr)
    a = a_scr[...]
    for half, ga in enumerate((ga0, ga1)):
        cols = slice(half * COL, (half + 1) * COL)
        y = yfn_scr[:, cols] + ga[...].astype(F32) * _dot(a, wpa_ref[:, cols])
        y_scr[:, cols] = y.astype(BF16)
    o_ref[...] = x_ref[...] + gate_ref[0] * _dot(y_scr[...], wout_ref[...])


def _merge(f, n, z, x2, mod3, lng, lnb, ws, bst, wpa, wpf, wpn, wout, *, rows_per_mod, mod_row0):
    T, D = x2.shape
    tm = MERGE_ROWS
    mrow = lambda i: mod_row0 + i // rows_per_mod
    act = pl.BlockSpec((tm, COL), lambda i: (i, 0))
    zblk = lambda c: pl.BlockSpec((tm, COL), lambda i: (i, c))
    const = lambda shape: pl.BlockSpec(shape, lambda i: (0,) * len(shape), pipeline_mode=pl.Buffered(1))
    return pl.pallas_call(
        _merge_kernel,
        grid=(T // tm,),
        in_specs=[
            zblk(OFF_U // COL), zblk(OFF_VG // COL), zblk(OFF_GA // COL),
            act, act,
        ] + [zblk(OFF_MERGE // COL + c) for c in range(3 * D // COL)] + [
            pl.BlockSpec((tm, D), lambda i: (i, 0)),
            pl.BlockSpec((1, 1, D), lambda i: (mrow(i), 0, 2)),
            const((1, W_A)), const((1, W_A)), const((GMLP_GROUPS, CHUNK, CHUNK)), const((CHUNK, GMLP_GROUPS)),
            const((W_A, D)), const((W_B, D)), const((W_C, D)), const((D, D)),
        ],
        out_specs=pl.BlockSpec((tm, D), lambda i: (i, 0)),
        out_shape=jax.ShapeDtypeStruct((T, D), F32),
        scratch_shapes=[pltpu.VMEM((tm, W_A), BF16), pltpu.VMEM((tm, W_A), BF16),
                        pltpu.VMEM((tm, D), F32), pltpu.VMEM((tm, D), BF16)],
        compiler_params=_params("arbitrary"),
        name="merge_out",
    )(z, z, z, f, n, z, z, z, z, z, z, x2, mod3, lng, lnb, ws, bst, wpa, wpf, wpn, wout)


def kernel(x, c, ctx, c_ctx, norm_g, w_ada, b_ada, w_in, gmlp_ln_g, gmlp_ln_b, gmlp_ws, gmlp_bs,
           q_norm_g, k_norm_g, rpb, w_pa, w_pf, w_pn, w_out):
    B, S, D = x.shape
    C = ctx.shape[1]
    L = w_in.shape[0]
    assert D == D_MODEL and C == CTX_LEN and B < MOD_ROWS
    assert S % (GRID_W * Q_ROWS) == 0 and S // GRID_W >= WIN_ROWS

    cc = jnp.concatenate([c, c_ctx[None], jnp.zeros((MOD_ROWS - B - 1, D), F32)], axis=0)
    mod = _modulation(cc, w_ada, b_ada)

    dft_lat = _dft_consts(S)
    dft_ctx = _dft_consts(C)
    x2 = x.reshape(B * S, D)
    c2 = ctx.reshape(B * C, D)
    for l in range(L):
        last = l == L - 1
        mod3 = mod[l].reshape(MOD_ROWS, 1, 3 * D)
        g = norm_g[l].reshape(1, D)
        wpa, wpf, wpn, wout = (t[l].astype(BF16) for t in (w_pa, w_pf, w_pn, w_out))
        lng = gmlp_ln_g[l].reshape(1, W_A)
        lnb = gmlp_ln_b[l].reshape(1, W_A)
        ws = gmlp_ws[l].astype(BF16)
        bst = gmlp_bs[l].T
        qg = q_norm_g[l].reshape(1, HEAD_DIM)
        kg = k_norm_g[l].reshape(1, HEAD_DIM)
        rpb_flat = rpb[l].reshape(-1)

        if last:
            zc = _in_proj(c2, g, mod3, w_in, l, rows_per_mod=ONE_ROW, mod_row0=B,
                          col0=OFF_K // COL, ncol=2)
            ck, cv = 0, W_C // HEAD_DIM
        else:
            zc = _in_proj(c2, g, mod3, w_in, l, rows_per_mod=ONE_ROW, mod_row0=B, col0=0,
                          ncol=W_IN // COL)
            ck, cv = OFF_K // HEAD_DIM, OFF_V // HEAD_DIM
        z = _in_proj(x2, g, mod3, w_in, l, rows_per_mod=S // IN_PROJ_ROWS, mod_row0=0, col0=0,
                     ncol=W_IN // COL)

        n = _nbr_attention(z, zc, ck, cv, qg, kg, rpb_flat, S)
        f = _fourier(z, S, dft_lat)
        x_new = _merge(f, n, z, x2, mod3, lng, lnb, ws, bst, wpa, wpf, wpn, wout,
                       rows_per_mod=S // MERGE_ROWS, mod_row0=0)

        if not last:
            nc = _ctx_attention(zc, qg, kg)
            fc = _fourier(zc, C, dft_ctx)
            c2 = _merge(fc, nc, zc, c2, mod3, lng, lnb, ws, bst, wpa, wpf, wpn, wout,
                        rows_per_mod=ONE_ROW, mod_row0=B)
        x2 = x_new
    return x2.reshape(B, S, D)
```

```python
import functools

import jax
import jax.numpy as jnp
import numpy as np
from jax import lax
from jax.experimental import pallas as pl
from jax.experimental.pallas import tpu as pltpu

F32 = jnp.float32
BF16 = jnp.bfloat16

D_MODEL = 2048
CTX_LEN = 256
GRID_W = 64
EPS = 1e-6
CHUNK = 128
GMLP_GROUPS = 8
W_A = D_MODEL // 2
W_B = D_MODEL // 2
F_GROUPS = 4
F_GW = W_B // F_GROUPS
HEAD_DIM = 128
NA_HEADS = 8
W_C = NA_HEADS * HEAD_DIM
NA_KH = 8
NA_KW = 16

OFF_U = 0
OFF_VG = OFF_U + W_A
OFF_GA = OFF_VG + W_A
OFF_F = OFF_GA + W_A
OFF_GF = OFF_F + W_B
OFF_Q = OFF_GF + W_B
OFF_K = OFF_Q + W_C
OFF_V = OFF_K + W_C
OFF_GN = OFF_V + W_C
OFF_MERGE = OFF_GN + W_C
W_IN = OFF_MERGE + 3 * D_MODEL

LANES = 128
IN_PROJ_ROWS = 1024
MERGE_ROWS = 256
COL = 1024
MOD_ROWS = 16
ONE_ROW = 1 << 30
VMEM_LIMIT = 56 * 1024 * 1024
Z_DTYPE = jnp.bfloat16

Q_ROWS = 4
WIN_ROWS = 12
NEG = -1e30
LOG2E = float(np.log2(np.e))
ATT_SCALE = HEAD_DIM ** -0.5 * LOG2E


GELU_C1 = float(np.sqrt(2.0 / np.pi))
GELU_C2 = GELU_C1 * 0.044715


def _sigmoid(t):
    return 0.5 * jnp.tanh(0.5 * t) + 0.5


def _silu(t):
    h = 0.5 * t
    return h * jnp.tanh(h) + h


def _gelu(t):
    h = 0.5 * t
    return h * jnp.tanh(t * (GELU_C1 + GELU_C2 * (t * t))) + h


def _dot(a, b):
    return jnp.dot(a, b, preferred_element_type=F32)


def _dot_nt(a, b):
    return lax.dot_general(a, b, (((1,), (1,)), ((), ())), preferred_element_type=F32)


def _params(*sem):
    return pltpu.CompilerParams(dimension_semantics=sem, vmem_limit_bytes=VMEM_LIMIT)


def _mod_kernel(cc_ref, w_ref, b_ref, o_ref):
    s = _silu(cc_ref[...])
    s_hi = s.astype(BF16)
    s_lo = (s - s_hi.astype(F32)).astype(BF16)
    w = w_ref[0]
    w_hi = w.astype(BF16)
    w_lo = (w - w_hi.astype(F32)).astype(BF16)
    o_ref[0] = _dot(s_hi, w_hi) + _dot(s_lo, w_hi) + _dot(s_hi, w_lo) + b_ref[0]


def _modulation(cc, w_ada, b_ada):
    L, D, N = w_ada.shape
    tn = COL
    return pl.pallas_call(
        _mod_kernel,
        grid=(L, N // tn),
        in_specs=[
            pl.BlockSpec((MOD_ROWS, D), lambda l, j: (0, 0)),
            pl.BlockSpec((1, D, tn), lambda l, j: (l, 0, j)),
            pl.BlockSpec((1, 1, tn), lambda l, j: (l, 0, j)),
        ],
        out_specs=pl.BlockSpec((1, MOD_ROWS, tn), lambda l, j: (l, 0, j)),
        out_shape=jax.ShapeDtypeStruct((L, MOD_ROWS, N), F32),
        compiler_params=_params("arbitrary", "arbitrary"),
        name="adaln_mod",
    )(cc, w_ada, b_ada.reshape(L, 1, N))


def _in_proj_kernel(x_ref, g_ref, sh_ref, sc_ref, w_ref, o_ref, h_scr, *, col0):
    j = pl.program_id(1)

    @pl.when(j == 0)
    def _():
        x = x_ref[...]
        gain = g_ref[0] * (1.0 + sc_ref[0])
        r = lax.rsqrt(jnp.mean(x * x, axis=-1, keepdims=True) + EPS)
        h_scr[...] = (x * r * gain + sh_ref[0]).astype(BF16)

    c = j + col0

    def project(act):
        o_ref[...] = act(_dot(h_scr[...], w_ref[0].astype(BF16))).astype(o_ref.dtype)

    @pl.when(c < OFF_GA // COL)
    def _():
        project(_gelu)

    @pl.when((c == OFF_GA // COL) | (c == OFF_GF // COL) | (c == OFF_GN // COL))
    def _():
        project(_silu)

    @pl.when(c >= OFF_MERGE // COL)
    def _():
        project(_sigmoid)

    @pl.when((c == OFF_F // COL) | ((c >= OFF_Q // COL) & (c < OFF_GN // COL)))
    def _():
        project(lambda z: z)


def _in_proj(x2, norm_g, mod, w_in, layer, *, rows_per_mod, mod_row0, col0, ncol):
    T, D = x2.shape
    tm = min(IN_PROJ_ROWS, T)
    mrow = lambda i: layer * MOD_ROWS + mod_row0 + i // rows_per_mod
    return pl.pallas_call(
        functools.partial(_in_proj_kernel, col0=col0),
        grid=(T // tm, ncol),
        in_specs=[
            pl.BlockSpec((tm, D), lambda i, j: (i, 0)),
            pl.BlockSpec((1, 1, D), lambda i, j: (layer, 0, 0)),
            pl.BlockSpec((1, 1, D), lambda i, j: (mrow(i), 0, 0)),
            pl.BlockSpec((1, 1, D), lambda i, j: (mrow(i), 0, 1)),
            pl.BlockSpec((1, D, COL), lambda i, j: (layer, 0, j + col0)),
        ],
        out_specs=pl.BlockSpec((tm, COL), lambda i, j: (i, j)),
        out_shape=jax.ShapeDtypeStruct((T, ncol * COL), Z_DTYPE),
        scratch_shapes=[pltpu.VMEM((tm, D), BF16)],
        compiler_params=_params("arbitrary", "arbitrary"),
        name="in_proj",
    )(x2, norm_g, mod, mod, w_in)


def _spatial_gating(u_ref, vg_ref, ga_ref, lng_ref, lnb_ref, ws_ref, bst_ref, o_ref, vn_scr):
    v = vg_ref[...].astype(F32)
    mu = jnp.mean(v, axis=-1, keepdims=True)
    vc = v - mu
    vn = vc * lax.rsqrt(jnp.mean(vc * vc, axis=-1, keepdims=True) + EPS)
    vn_scr[...] = (vn * lng_ref[0] + lnb_ref[0]).astype(BF16)
    nchunk = u_ref.shape[0] // CHUNK
    for g in range(GMLP_GROUPS):
        cols = slice(g * CHUNK, (g + 1) * CHUNK)
        v_all = jnp.concatenate([vn_scr[ch * CHUNK:(ch + 1) * CHUNK, cols] for ch in range(nchunk)], axis=1)
        sv_all = _dot(ws_ref[0, g], v_all) + bst_ref[0, :, g:g + 1]
        for ch in range(nchunk):
            rows = slice(ch * CHUNK, (ch + 1) * CHUNK)
            sv = sv_all[:, ch * CHUNK:(ch + 1) * CHUNK]
            a = u_ref[rows, cols].astype(F32) * sv * ga_ref[rows, cols].astype(F32)
            o_ref[rows, cols] = a.astype(o_ref.dtype)


DFT_RADIX = 4


def _dft_consts(n):
    m = n // DFT_RADIX
    k = np.arange(m, dtype=np.int64)
    ang = 2.0 * np.pi * ((k[:, None] * k[None, :]) % m).astype(np.float64) / m
    cs = np.concatenate([np.cos(ang), np.sin(ang)], axis=0)
    th = 2.0 * np.pi * np.arange(n // 2, dtype=np.float64) / n
    twc = np.broadcast_to(np.cos(th)[:, None], (n // 2, LANES))
    tws = np.broadcast_to(np.sin(th)[:, None], (n // 2, LANES))
    c = np.arange(F_GW, dtype=np.int64)
    angc = 2.0 * np.pi * ((c[:, None] * c[None, :]) % F_GW).astype(np.float64) / F_GW
    bf = lambda t: jnp.asarray(t, dtype=F32).astype(BF16)
    return bf(cs), jnp.asarray(twc, dtype=F32), jnp.asarray(tws, dtype=F32), bf(np.cos(angc)), bf(np.sin(angc))


def _fourier_kernel(f_ref, gf_ref, cs_ref, twc_ref, tws_ref, cc_ref, sc_ref, o_ref, x_scr):
    n = f_ref.shape[0]
    m0 = n // DFT_RADIX
    lane_tiles = F_GW // LANES

    def position_dft(g):
        for t in range(lane_tiles):
            x_scr[g, t] = f_ref[:, g * F_GW + t * LANES:g * F_GW + (t + 1) * LANES].astype(F32)
        xcat = jnp.concatenate(
            [x_scr[g, t, pl.ds(r, m0, stride=DFT_RADIX), :].astype(BF16)
             for r in range(DFT_RADIX) for t in range(lane_tiles)], axis=1)
        return _dot(cs_ref[...], xcat)

    def butterflies(e):
        base = [(e[:m0, r * F_GW:(r + 1) * F_GW], e[m0:, r * F_GW:(r + 1) * F_GW])
                for r in range(DFT_RADIX)]

        def sub_dft(stride, offset):
            if stride == DFT_RADIX:
                return base[offset]
            ec0, es0 = sub_dft(2 * stride, offset)
            ec1, es1 = sub_dft(2 * stride, offset + stride)
            m = n // (2 * stride)
            c = jnp.concatenate([twc_ref[pl.ds(0, m, stride=stride), :]] * lane_tiles, axis=1)
            s = jnp.concatenate([tws_ref[pl.ds(0, m, stride=stride), :]] * lane_tiles, axis=1)
            a = c * ec1 - s * es1
            b = c * es1 + s * ec1
            return (jnp.concatenate([ec0 + a, ec0 - a], axis=0),
                    jnp.concatenate([es0 + b, es0 - b], axis=0))

        return sub_dft(1, 0)

    nxt = position_dft(0)
    for g in range(F_GROUPS):
        e = nxt
        if g + 1 < F_GROUPS:
            nxt = position_dft(g + 1)
        tc, ts = butterflies(e)
        cols = slice(g * F_GW, (g + 1) * F_GW)
        y = _dot(tc.astype(BF16), cc_ref[...]) - _dot(ts.astype(BF16), sc_ref[...])
        o_ref[:, cols] = (y * ((n * F_GW) ** -0.5) * gf_ref[:, cols].astype(F32)).astype(o_ref.dtype)


def _fourier(z, n, consts):
    T = z.shape[0]
    const = lambda t: pl.BlockSpec(t.shape, lambda b: (0, 0), pipeline_mode=pl.Buffered(1))
    return pl.pallas_call(
        _fourier_kernel,
        grid=(T // n,),
        in_specs=[
            pl.BlockSpec((n, W_B), lambda b: (b, OFF_F // W_B)),
            pl.BlockSpec((n, W_B), lambda b: (b, OFF_GF // W_B)),
        ] + [const(t) for t in consts],
        out_specs=pl.BlockSpec((n, W_B), lambda b: (b, 0)),
        out_shape=jax.ShapeDtypeStruct((T, W_B), BF16),
        scratch_shapes=[pltpu.VMEM((F_GROUPS, F_GW // LANES, n, LANES), F32)],
        compiler_params=_params("arbitrary"),
        name="fourier",
    )(z, z, *consts)


def _head_rms(t, g):
    t = t.astype(F32)
    return t * lax.rsqrt(jnp.mean(t * t, axis=-1, keepdims=True) + EPS) * g


def _block_geometry(rows):
    assert Q_ROWS == NA_KH // 2 and rows % Q_ROWS == 0 and rows // Q_ROWS >= 3
    assert WIN_ROWS % 2 == 0 and Q_ROWS + NA_KH - 1 <= WIN_ROWS <= rows
    geo = []
    nblk = rows // Q_ROWS
    for blk in range(nblk):
        base = min(max(Q_ROWS * blk - NA_KH // 2, 0), rows - WIN_ROWS)
        typ = 0 if blk == 0 else (2 if blk == nblk - 1 else 1)
        geo.append((base, typ))
    return geo


def _build_bias_tables(rpb_ref, tab_scr, head, geo, rows):
    n_dr, n_dc = 2 * NA_KH - 1, 2 * NA_KW - 1
    shape = (GRID_W, 2 * GRID_W)
    lane = lax.broadcasted_iota(jnp.int32, shape, 1)
    cq = lax.broadcasted_iota(jnp.int32, shape, 0)
    kc = lane & (GRID_W - 1)
    dc = kc - cq + (NA_KW - 1)
    cs = jnp.clip(cq - NA_KW // 2, 0, GRID_W - NA_KW)
    col_ok = (kc >= cs) & (kc < cs + NA_KW)
    neg = jnp.full(shape, NEG, F32)
    row_bias = []
    for d in range(n_dr):
        t = neg
        for b in range(n_dc):
            t = jnp.where(dc == b, rpb_ref[(head * n_dr + d) * n_dc + b] * LOG2E, t)
        row_bias.append(jnp.where(col_ok, t, NEG))
    nblk = len(geo)
    for typ, blk in enumerate((0, 1, nblk - 1)):
        base = geo[blk][0]
        for i in range(Q_ROWS):
            r = Q_ROWS * blk + i
            rs = min(max(r - NA_KH // 2, 0), rows - NA_KH)
            for wp in range(WIN_ROWS // 2):
                halves = []
                for kr in (base + 2 * wp, base + 2 * wp + 1):
                    halves.append(row_bias[kr - r + NA_KH - 1] if rs <= kr < rs + NA_KH else neg)
                tab_scr[typ, i * GRID_W:(i + 1) * GRID_W, wp * 2 * GRID_W:(wp + 1) * 2 * GRID_W] = (
                    jnp.where(lane < GRID_W, halves[0], halves[1]))


def _nbr_attn_kernel(rpb_ref, q_ref, k_ref, v_ref, gn_ref, kc_ref, vc_ref, qg_ref, kg_ref, o_ref,
                     qn_scr, kn_scr, kcn_scr, vx_scr, vcx_scr, tab_ref, *, geo, rows, layer):
    hd = HEAD_DIM

    @pl.when(pl.program_id(1) == 0)
    def _():
        _build_bias_tables(rpb_ref, tab_ref, layer * NA_HEADS + pl.program_id(0), geo, rows)
        vx_scr[:, hd:] = jnp.ones((vx_scr.shape[0], hd), BF16)
        vcx_scr[:, hd:] = jnp.ones((vcx_scr.shape[0], hd), BF16)

    qn_scr[...] = _head_rms(q_ref[...], qg_ref[0] * ATT_SCALE).astype(BF16)
    kn_scr[...] = _head_rms(k_ref[...], kg_ref[0]).astype(BF16)
    kcn_scr[...] = _head_rms(kc_ref[...], kg_ref[0]).astype(BF16)
    vx_scr[:, :hd] = v_ref[...].astype(BF16)
    vcx_scr[:, :hd] = vc_ref[...].astype(BF16)
    nq = Q_ROWS * GRID_W
    nk = WIN_ROWS * GRID_W

    def scores(blk):
        base, typ = geo[blk]
        q = qn_scr[blk * nq:(blk + 1) * nq, :]
        s_loc = _dot_nt(q, kn_scr[base * GRID_W:base * GRID_W + nk, :]) + tab_ref[typ]
        return s_loc, _dot_nt(q, kcn_scr[...])

    nxt = scores(0)
    for blk, (base, typ) in enumerate(geo):
        qs = slice(blk * nq, (blk + 1) * nq)
        ks = slice(base * GRID_W, base * GRID_W + nk)
        s_loc, s_ctx = nxt
        if blk + 1 < len(geo):
            nxt = scores(blk + 1)
        m = jnp.maximum(jnp.max(s_loc, axis=-1, keepdims=True), jnp.max(s_ctx, axis=-1, keepdims=True))
        p_loc = jnp.exp2(s_loc - m)
        p_ctx = jnp.exp2(s_ctx - m)
        ox = _dot(p_loc.astype(BF16), vx_scr[ks, :]) + _dot(p_ctx.astype(BF16), vcx_scr[...])
        o_ref[qs, :] = (ox[:, :hd] / ox[:, hd:] * gn_ref[qs, :].astype(F32)).astype(o_ref.dtype)


def _nbr_attention(z, zc, ck, cv, qg, kg, rpb_flat, layer, seq):
    T = z.shape[0]
    rows = seq // GRID_W
    geo = _block_geometry(rows)
    hd = HEAD_DIM
    lat = lambda off: pl.BlockSpec((seq, hd), lambda h, b: (b, off // hd + h))
    return pl.pallas_call(
        functools.partial(_nbr_attn_kernel, geo=geo, rows=rows, layer=layer),
        grid=(NA_HEADS, T // seq),
        in_specs=[
            pl.BlockSpec(memory_space=pltpu.SMEM),
            lat(OFF_Q), lat(OFF_K), lat(OFF_V), lat(OFF_GN),
            pl.BlockSpec((CTX_LEN, hd), lambda h, b: (b, ck + h)),
            pl.BlockSpec((CTX_LEN, hd), lambda h, b: (b, cv + h)),
            pl.BlockSpec((1, 1, hd), lambda h, b: (layer, 0, 0)),
            pl.BlockSpec((1, 1, hd), lambda h, b: (layer, 0, 0)),
        ],
        out_specs=pl.BlockSpec((seq, hd), lambda h, b: (b, h)),
        out_shape=jax.ShapeDtypeStruct((T, W_C), BF16),
        scratch_shapes=[pltpu.VMEM((seq, hd), BF16), pltpu.VMEM((seq, hd), BF16),
                        pltpu.VMEM((CTX_LEN, hd), BF16),
                        pltpu.VMEM((seq, 2 * hd), BF16), pltpu.VMEM((CTX_LEN, 2 * hd), BF16),
                        pltpu.VMEM((3, Q_ROWS * GRID_W, WIN_ROWS * GRID_W), F32)],
        compiler_params=_params("arbitrary", "arbitrary"),
        name="nbr_attn",
    )(rpb_flat, z, z, z, z, zc, zc, qg, kg)


def _ctx_attn_kernel(q_ref, k_ref, v_ref, gn_ref, qg_ref, kg_ref, o_ref):
    for h in range(NA_HEADS):
        cols = slice(h * HEAD_DIM, (h + 1) * HEAD_DIM)
        q = _head_rms(q_ref[:, cols], qg_ref[0] * ATT_SCALE).astype(BF16)
        k = _head_rms(k_ref[:, cols], kg_ref[0]).astype(BF16)
        s = _dot_nt(q, k)
        p = jnp.exp2(s - jnp.max(s, axis=-1, keepdims=True))
        o = _dot(p.astype(BF16), v_ref[:, cols].astype(BF16)) / jnp.sum(p, axis=-1, keepdims=True)
        o_ref[:, cols] = (o * gn_ref[:, cols].astype(F32)).astype(o_ref.dtype)


def _ctx_attention(zc, qg, kg, layer):
    T = zc.shape[0]
    blk = lambda off: pl.BlockSpec((CTX_LEN, W_C), lambda b: (b, off // W_C))
    vec = pl.BlockSpec((1, 1, HEAD_DIM), lambda b: (layer, 0, 0))
    return pl.pallas_call(
        _ctx_attn_kernel,
        grid=(T // CTX_LEN,),
        in_specs=[blk(OFF_Q), blk(OFF_K), blk(OFF_V), blk(OFF_GN), vec, vec],
        out_specs=pl.BlockSpec((CTX_LEN, W_C), lambda b: (b, 0)),
        out_shape=jax.ShapeDtypeStruct((T, W_C), BF16),
        compiler_params=_params("arbitrary"),
        name="ctx_attn",
    )(zc, zc, zc, zc, qg, kg)


def _merge_kernel(u_ref, vg_ref, ga_ref, f_ref, n_ref, ga0, ga1, gf0, gf1, gn0, gn1, x_ref, gate_ref,
                  lng_ref, lnb_ref, ws_ref, bst_ref, wpa_ref, wpf_ref, wpn_ref, wout_ref, o_ref,
                  a_scr, vn_scr, yfn_scr, y_scr):
    f = f_ref[...]
    n = n_ref[...]
    for half, (gf, gn) in enumerate(((gf0, gn0), (gf1, gn1))):
        cols = slice(half * COL, (half + 1) * COL)
        yfn_scr[:, cols] = (gf[...].astype(F32) * _dot(f, wpf_ref[0, :, cols])
                            + gn[...].astype(F32) * _dot(n, wpn_ref[0, :, cols]))
    _spatial_gating(u_ref, vg_ref, ga_ref, lng_ref, lnb_ref, ws_ref, bst_ref, a_scr, vn_scr)
    a = a_scr[...]
    for half, ga in enumerate((ga0, ga1)):
        cols = slice(half * COL, (half + 1) * COL)
        y = yfn_scr[:, cols] + ga[...].astype(F32) * _dot(a, wpa_ref[0, :, cols])
        y_scr[:, cols] = y.astype(BF16)
    o_ref[...] = x_ref[...] + gate_ref[0] * _dot(y_scr[...], wout_ref[0])


def _merge(f, n, z, x2, mod, lng, lnb, ws, bst, wpa, wpf, wpn, wout, layer, *, rows_per_mod, mod_row0):
    T, D = x2.shape
    tm = MERGE_ROWS
    mrow = lambda i: layer * MOD_ROWS + mod_row0 + i // rows_per_mod
    act = pl.BlockSpec((tm, COL), lambda i: (i, 0))
    zblk = lambda c: pl.BlockSpec((tm, COL), lambda i: (i, c))
    const = lambda *shape: pl.BlockSpec((1,) + shape, lambda i: (layer,) + (0,) * len(shape),
                                        pipeline_mode=pl.Buffered(1))
    return pl.pallas_call(
        _merge_kernel,
        grid=(T // tm,),
        in_specs=[
            zblk(OFF_U // COL), zblk(OFF_VG // COL), zblk(OFF_GA // COL),
            act, act,
        ] + [zblk(OFF_MERGE // COL + c) for c in range(3 * D // COL)] + [
            pl.BlockSpec((tm, D), lambda i: (i, 0)),
            pl.BlockSpec((1, 1, D), lambda i: (mrow(i), 0, 2)),
            const(1, W_A), const(1, W_A), const(GMLP_GROUPS, CHUNK, CHUNK), const(CHUNK, GMLP_GROUPS),
            const(W_A, D), const(W_B, D), const(W_C, D), const(D, D),
        ],
        out_specs=pl.BlockSpec((tm, D), lambda i: (i, 0)),
        out_shape=jax.ShapeDtypeStruct((T, D), F32),
        scratch_shapes=[pltpu.VMEM((tm, W_A), BF16), pltpu.VMEM((tm, W_A), BF16),
                        pltpu.VMEM((tm, D), F32), pltpu.VMEM((tm, D), BF16)],
        compiler_params=_params("arbitrary"),
        name="merge_out",
    )(z, z, z, f, n, z, z, z, z, z, z, x2, mod, lng, lnb, ws, bst, wpa, wpf, wpn, wout)


def kernel(x, c, ctx, c_ctx, norm_g, w_ada, b_ada, w_in, gmlp_ln_g, gmlp_ln_b, gmlp_ws, gmlp_bs,
           q_norm_g, k_norm_g, rpb, w_pa, w_pf, w_pn, w_out):
    B, S, D = x.shape
    C = ctx.shape[1]
    L = w_in.shape[0]
    assert D == D_MODEL and C == CTX_LEN and B < MOD_ROWS
    assert S % (GRID_W * Q_ROWS) == 0 and S // GRID_W >= WIN_ROWS

    cc = jnp.concatenate([c, c_ctx[None], jnp.zeros((MOD_ROWS - B - 1, D), F32)], axis=0)
    mod = _modulation(cc, w_ada, b_ada).reshape(L * MOD_ROWS, 1, 3 * D)

    wpa, wpf, wpn, wout, ws = (t.astype(BF16) for t in (w_pa, w_pf, w_pn, w_out, gmlp_ws))
    bst = jnp.swapaxes(gmlp_bs, 1, 2)
    rpb_flat = rpb.reshape(-1)
    norm_g, gmlp_ln_g, gmlp_ln_b, q_norm_g, k_norm_g = (
        t[:, None, :] for t in (norm_g, gmlp_ln_g, gmlp_ln_b, q_norm_g, k_norm_g))
    dft_lat = _dft_consts(S)
    dft_ctx = _dft_consts(C)
    x2 = x.reshape(B * S, D)
    c2 = ctx.reshape(B * C, D)
    for l in range(L):
        last = l == L - 1
        if last:
            zc = _in_proj(c2, norm_g, mod, w_in, l, rows_per_mod=ONE_ROW, mod_row0=B,
                          col0=OFF_K // COL, ncol=2)
            ck, cv = 0, W_C // HEAD_DIM
        else:
            zc = _in_proj(c2, norm_g, mod, w_in, l, rows_per_mod=ONE_ROW, mod_row0=B, col0=0,
                          ncol=W_IN // COL)
            ck, cv = OFF_K // HEAD_DIM, OFF_V // HEAD_DIM
        z = _in_proj(x2, norm_g, mod, w_in, l, rows_per_mod=S // IN_PROJ_ROWS, mod_row0=0, col0=0,
                     ncol=W_IN // COL)

        n = _nbr_attention(z, zc, ck, cv, q_norm_g, k_norm_g, rpb_flat, l, S)
        f = _fourier(z, S, dft_lat)
        x_new = _merge(f, n, z, x2, mod, gmlp_ln_g, gmlp_ln_b, ws, bst, wpa, wpf, wpn, wout, l,
                       rows_per_mod=S // MERGE_ROWS, mod_row0=0)

        if not last:
            nc = _ctx_attention(zc, q_norm_g, k_norm_g, l)
            fc = _fourier(zc, C, dft_ctx)
            c2 = _merge(fc, nc, zc, c2, mod, gmlp_ln_g, gmlp_ln_b, ws, bst, wpa, wpf, wpn, wout, l,
                        rows_per_mod=ONE_ROW, mod_row0=B)
        x2 = x_new
    return x2.reshape(B, S, D)
```

```python
import functools

import jax
import jax.numpy as jnp
import numpy as np
from jax import lax
from jax.experimental import pallas as pl
from jax.experimental.pallas import tpu as pltpu

F32 = jnp.float32
BF16 = jnp.bfloat16

D_MODEL = 2048
CTX_LEN = 256
GRID_W = 64
EPS = 1e-6
CHUNK = 128
GMLP_GROUPS = 8
W_A = D_MODEL // 2
W_B = D_MODEL // 2
F_GROUPS = 4
F_GW = W_B // F_GROUPS
HEAD_DIM = 128
NA_HEADS = 8
W_C = NA_HEADS * HEAD_DIM
NA_KH = 8
NA_KW = 16

OFF_U = 0
OFF_VG = OFF_U + W_A
OFF_GA = OFF_VG + W_A
OFF_F = OFF_GA + W_A
OFF_GF = OFF_F + W_B
OFF_Q = OFF_GF + W_B
OFF_K = OFF_Q + W_C
OFF_V = OFF_K + W_C
OFF_GN = OFF_V + W_C
OFF_MERGE = OFF_GN + W_C
W_IN = OFF_MERGE + 3 * D_MODEL

LANES = 128
BF16_SUBLANES = 16
IN_PROJ_ROWS = 1024
MERGE_ROWS = 256
COL = 1024
MOD_ROWS = 16
ONE_ROW = 1 << 30
VMEM_LIMIT = 56 * 1024 * 1024
Z_DTYPE = jnp.bfloat16

Q_ROWS = 4
WIN_ROWS = 12
NEG = -1e30
LOG2E = float(np.log2(np.e))
ATT_SCALE = HEAD_DIM ** -0.5 * LOG2E


GELU_C1 = float(np.sqrt(2.0 / np.pi))
GELU_C2 = GELU_C1 * 0.044715


def _sigmoid(t):
    return 0.5 * jnp.tanh(0.5 * t) + 0.5


def _silu(t):
    h = 0.5 * t
    return h * jnp.tanh(h) + h


def _gelu(t):
    h = 0.5 * t
    return h * jnp.tanh(t * (GELU_C1 + GELU_C2 * (t * t))) + h


def _dot(a, b):
    return jnp.dot(a, b, preferred_element_type=F32)


def _dot_nt(a, b):
    return lax.dot_general(a, b, (((1,), (1,)), ((), ())), preferred_element_type=F32)


def _params(*sem):
    return pltpu.CompilerParams(dimension_semantics=sem, vmem_limit_bytes=VMEM_LIMIT)


def _mod_kernel(cc_ref, w_ref, b_ref, o_ref):
    s = _silu(cc_ref[...])
    s_hi = s.astype(BF16)
    s_lo = (s - s_hi.astype(F32)).astype(BF16)
    w = w_ref[0]
    w_hi = w.astype(BF16)
    w_lo = (w - w_hi.astype(F32)).astype(BF16)
    o_ref[0] = _dot(s_hi, w_hi) + _dot(s_lo, w_hi) + _dot(s_hi, w_lo) + b_ref[0]


def _modulation(cc, w_ada, b_ada):
    L, D, N = w_ada.shape
    tn = COL
    return pl.pallas_call(
        _mod_kernel,
        grid=(L, N // tn),
        in_specs=[
            pl.BlockSpec((MOD_ROWS, D), lambda l, j: (0, 0)),
            pl.BlockSpec((1, D, tn), lambda l, j: (l, 0, j)),
            pl.BlockSpec((1, 1, tn), lambda l, j: (l, 0, j)),
        ],
        out_specs=pl.BlockSpec((1, MOD_ROWS, tn), lambda l, j: (l, 0, j)),
        out_shape=jax.ShapeDtypeStruct((L, MOD_ROWS, N), F32),
        compiler_params=_params("arbitrary", "arbitrary"),
        name="adaln_mod",
    )(cc, w_ada, b_ada.reshape(L, 1, N))


def _in_proj_kernel(x_ref, g_ref, sh_ref, sc_ref, w_ref, *rest, col0, n_cast):
    cast_in, (o_ref, *cast_out, h_scr) = rest[:n_cast], rest[n_cast:]
    j = pl.program_id(1)

    for src, dst in zip(cast_in, cast_out):
        dst[...] = src[...].astype(dst.dtype)

    @pl.when(j == 0)
    def _():
        x = x_ref[...]
        gain = g_ref[0] * (1.0 + sc_ref[0])
        r = lax.rsqrt(jnp.mean(x * x, axis=-1, keepdims=True) + EPS)
        h_scr[...] = (x * r * gain + sh_ref[0]).astype(BF16)

    c = j + col0

    def project(act):
        o_ref[...] = act(_dot(h_scr[...], w_ref[0].astype(BF16))).astype(o_ref.dtype)

    @pl.when(c < OFF_GA // COL)
    def _():
        project(_gelu)

    @pl.when((c == OFF_GA // COL) | (c == OFF_GF // COL) | (c == OFF_GN // COL))
    def _():
        project(_silu)

    @pl.when(c >= OFF_MERGE // COL)
    def _():
        project(_sigmoid)

    @pl.when((c == OFF_F // COL) | ((c >= OFF_Q // COL) & (c < OFF_GN // COL)))
    def _():
        project(lambda z: z)


def _in_proj(x2, norm_g, mod, w_in, layer, *, rows_per_mod, mod_row0, col0, ncol, cast=()):
    T, D = x2.shape
    tm = min(IN_PROJ_ROWS, T)
    nsteps = (T // tm) * ncol
    mrow = lambda i: layer * MOD_ROWS + mod_row0 + i // rows_per_mod

    cast2d = [t.reshape(-1, t.shape[-1]) for t in cast]
    cast_specs = []
    for t in cast2d:
        rows = BF16_SUBLANES * pl.cdiv(t.shape[0], BF16_SUBLANES * nsteps)
        while t.shape[0] % rows:
            rows += BF16_SUBLANES
        nblk = t.shape[0] // rows
        cast_specs.append(pl.BlockSpec(
            (rows, t.shape[1]), lambda i, j, nblk=nblk: (jnp.minimum(i * ncol + j, nblk - 1), 0)))

    z, *casted = pl.pallas_call(
        functools.partial(_in_proj_kernel, col0=col0, n_cast=len(cast)),
        grid=(T // tm, ncol),
        in_specs=[
            pl.BlockSpec((tm, D), lambda i, j: (i, 0)),
            pl.BlockSpec((1, 1, D), lambda i, j: (layer, 0, 0)),
            pl.BlockSpec((1, 1, D), lambda i, j: (mrow(i), 0, 0)),
            pl.BlockSpec((1, 1, D), lambda i, j: (mrow(i), 0, 1)),
            pl.BlockSpec((1, D, COL), lambda i, j: (layer, 0, j + col0)),
        ] + cast_specs,
        out_specs=[pl.BlockSpec((tm, COL), lambda i, j: (i, j))] + cast_specs,
        out_shape=[jax.ShapeDtypeStruct((T, ncol * COL), Z_DTYPE)]
        + [jax.ShapeDtypeStruct(t.shape, BF16) for t in cast2d],
        scratch_shapes=[pltpu.VMEM((tm, D), BF16)],
        compiler_params=_params("arbitrary", "arbitrary"),
        name="in_proj",
    )(x2, norm_g, mod, mod, w_in, *cast2d)
    return (z, *(t.reshape(s.shape) for t, s in zip(casted, cast)))


def _spatial_gating(u_ref, vg_ref, ga_ref, lng_ref, lnb_ref, ws_ref, bst_ref, o_ref, vn_scr):
    v = vg_ref[...].astype(F32)
    mu = jnp.mean(v, axis=-1, keepdims=True)
    vc = v - mu
    vn = vc * lax.rsqrt(jnp.mean(vc * vc, axis=-1, keepdims=True) + EPS)
    vn_scr[...] = (vn * lng_ref[0] + lnb_ref[0]).astype(BF16)
    nchunk = u_ref.shape[0] // CHUNK
    for g in range(GMLP_GROUPS):
        cols = slice(g * CHUNK, (g + 1) * CHUNK)
        v_all = jnp.concatenate([vn_scr[ch * CHUNK:(ch + 1) * CHUNK, cols] for ch in range(nchunk)], axis=1)
        sv_all = _dot(ws_ref[0, g], v_all) + bst_ref[0, :, g:g + 1]
        for ch in range(nchunk):
            rows = slice(ch * CHUNK, (ch + 1) * CHUNK)
            sv = sv_all[:, ch * CHUNK:(ch + 1) * CHUNK]
            a = u_ref[rows, cols].astype(F32) * sv * ga_ref[rows, cols].astype(F32)
            o_ref[rows, cols] = a.astype(o_ref.dtype)


DFT_RADIX = 4


def _dft_consts(n):
    m = n // DFT_RADIX
    k = np.arange(m, dtype=np.int64)
    ang = 2.0 * np.pi * ((k[:, None] * k[None, :]) % m).astype(np.float64) / m
    cs = np.concatenate([np.cos(ang), np.sin(ang)], axis=0)
    th = 2.0 * np.pi * np.arange(n // 2, dtype=np.float64) / n
    twc = np.broadcast_to(np.cos(th)[:, None], (n // 2, LANES))
    tws = np.broadcast_to(np.sin(th)[:, None], (n // 2, LANES))
    c = np.arange(F_GW, dtype=np.int64)
    angc = 2.0 * np.pi * ((c[:, None] * c[None, :]) % F_GW).astype(np.float64) / F_GW
    bf = lambda t: jnp.asarray(t, dtype=F32).astype(BF16)
    return bf(cs), jnp.asarray(twc, dtype=F32), jnp.asarray(tws, dtype=F32), bf(np.cos(angc)), bf(np.sin(angc))


def _fourier_kernel(f_ref, gf_ref, cs_ref, twc_ref, tws_ref, cc_ref, sc_ref, o_ref, x_scr):
    n = f_ref.shape[0]
    m0 = n // DFT_RADIX
    lane_tiles = F_GW // LANES

    def position_dft(g):
        for t in range(lane_tiles):
            x_scr[g, t] = f_ref[:, g * F_GW + t * LANES:g * F_GW + (t + 1) * LANES].astype(F32)
        xcat = jnp.concatenate(
            [x_scr[g, t, pl.ds(r, m0, stride=DFT_RADIX), :].astype(BF16)
             for r in range(DFT_RADIX) for t in range(lane_tiles)], axis=1)
        return _dot(cs_ref[...], xcat)

    def butterflies(e):
        base = [(e[:m0, r * F_GW:(r + 1) * F_GW], e[m0:, r * F_GW:(r + 1) * F_GW])
                for r in range(DFT_RADIX)]

        def sub_dft(stride, offset):
            if stride == DFT_RADIX:
                return base[offset]
            ec0, es0 = sub_dft(2 * stride, offset)
            ec1, es1 = sub_dft(2 * stride, offset + stride)
            m = n // (2 * stride)
            c = jnp.concatenate([twc_ref[pl.ds(0, m, stride=stride), :]] * lane_tiles, axis=1)
            s = jnp.concatenate([tws_ref[pl.ds(0, m, stride=stride), :]] * lane_tiles, axis=1)
            a = c * ec1 - s * es1
            b = c * es1 + s * ec1
            return (jnp.concatenate([ec0 + a, ec0 - a], axis=0),
                    jnp.concatenate([es0 + b, es0 - b], axis=0))

        return sub_dft(1, 0)

    nxt = position_dft(0)
    for g in range(F_GROUPS):
        e = nxt
        if g + 1 < F_GROUPS:
            nxt = position_dft(g + 1)
        tc, ts = butterflies(e)
        cols = slice(g * F_GW, (g + 1) * F_GW)
        y = _dot(tc.astype(BF16), cc_ref[...]) - _dot(ts.astype(BF16), sc_ref[...])
        o_ref[:, cols] = (y * ((n * F_GW) ** -0.5) * gf_ref[:, cols].astype(F32)).astype(o_ref.dtype)


def _fourier(z, n, consts):
    T = z.shape[0]
    const = lambda t: pl.BlockSpec(t.shape, lambda b: (0, 0), pipeline_mode=pl.Buffered(1))
    return pl.pallas_call(
        _fourier_kernel,
        grid=(T // n,),
        in_specs=[
            pl.BlockSpec((n, W_B), lambda b: (b, OFF_F // W_B)),
            pl.BlockSpec((n, W_B), lambda b: (b, OFF_GF // W_B)),
        ] + [const(t) for t in consts],
        out_specs=pl.BlockSpec((n, W_B), lambda b: (b, 0)),
        out_shape=jax.ShapeDtypeStruct((T, W_B), BF16),
        scratch_shapes=[pltpu.VMEM((F_GROUPS, F_GW // LANES, n, LANES), F32)],
        compiler_params=_params("arbitrary"),
        name="fourier",
    )(z, z, *consts)


def _head_rms(t, g):
    t = t.astype(F32)
    return t * lax.rsqrt(jnp.mean(t * t, axis=-1, keepdims=True) + EPS) * g


def _block_geometry(rows):
    assert Q_ROWS == NA_KH // 2 and rows % Q_ROWS == 0 and rows // Q_ROWS >= 3
    assert WIN_ROWS % 2 == 0 and Q_ROWS + NA_KH - 1 <= WIN_ROWS <= rows
    geo = []
    nblk = rows // Q_ROWS
    for blk in range(nblk):
        base = min(max(Q_ROWS * blk - NA_KH // 2, 0), rows - WIN_ROWS)
        typ = 0 if blk == 0 else (2 if blk == nblk - 1 else 1)
        geo.append((base, typ))
    return geo


def _build_bias_tables(rpb_ref, tab_scr, head, geo, rows):
    n_dr, n_dc = 2 * NA_KH - 1, 2 * NA_KW - 1
    shape = (GRID_W, 2 * GRID_W)
    lane = lax.broadcasted_iota(jnp.int32, shape, 1)
    cq = lax.broadcasted_iota(jnp.int32, shape, 0)
    kc = lane & (GRID_W - 1)
    dc = kc - cq + (NA_KW - 1)
    cs = jnp.clip(cq - NA_KW // 2, 0, GRID_W - NA_KW)
    col_ok = (kc >= cs) & (kc < cs + NA_KW)
    neg = jnp.full(shape, NEG, F32)
    row_bias = []
    for d in range(n_dr):
        t = neg
        for b in range(n_dc):
            t = jnp.where(dc == b, rpb_ref[(head * n_dr + d) * n_dc + b] * LOG2E, t)
        row_bias.append(jnp.where(col_ok, t, NEG))
    nblk = len(geo)
    for typ, blk in enumerate((0, 1, nblk - 1)):
        base = geo[blk][0]
        for i in range(Q_ROWS):
            r = Q_ROWS * blk + i
            rs = min(max(r - NA_KH // 2, 0), rows - NA_KH)
            for wp in range(WIN_ROWS // 2):
                halves = []
                for kr in (base + 2 * wp, base + 2 * wp + 1):
                    halves.append(row_bias[kr - r + NA_KH - 1] if rs <= kr < rs + NA_KH else neg)
                tab_scr[typ, i * GRID_W:(i + 1) * GRID_W, wp * 2 * GRID_W:(wp + 1) * 2 * GRID_W] = (
                    jnp.where(lane < GRID_W, halves[0], halves[1]))


def _nbr_attn_kernel(rpb_ref, q_ref, k_ref, v_ref, gn_ref, kc_ref, vc_ref, qg_ref, kg_ref, o_ref,
                     qn_scr, kn_scr, kcn_scr, vx_scr, vcx_scr, tab_ref, *, geo, rows, layer):
    hd = HEAD_DIM

    @pl.when(pl.program_id(1) == 0)
    def _():
        _build_bias_tables(rpb_ref, tab_ref, layer * NA_HEADS + pl.program_id(0), geo, rows)
        vx_scr[:, hd:] = jnp.ones((vx_scr.shape[0], hd), BF16)
        vcx_scr[:, hd:] = jnp.ones((vcx_scr.shape[0], hd), BF16)

    qn_scr[...] = _head_rms(q_ref[...], qg_ref[0] * ATT_SCALE).astype(BF16)
    kn_scr[...] = _head_rms(k_ref[...], kg_ref[0]).astype(BF16)
    kcn_scr[...] = _head_rms(kc_ref[...], kg_ref[0]).astype(BF16)
    vx_scr[:, :hd] = v_ref[...].astype(BF16)
    vcx_scr[:, :hd] = vc_ref[...].astype(BF16)
    nq = Q_ROWS * GRID_W
    nk = WIN_ROWS * GRID_W

    def scores(blk):
        base, typ = geo[blk]
        q = qn_scr[blk * nq:(blk + 1) * nq, :]
        s_loc = _dot_nt(q, kn_scr[base * GRID_W:base * GRID_W + nk, :]) + tab_ref[typ]
        return s_loc, _dot_nt(q, kcn_scr[...])

    nxt = scores(0)
    for blk, (base, typ) in enumerate(geo):
        qs = slice(blk * nq, (blk + 1) * nq)
        ks = slice(base * GRID_W, base * GRID_W + nk)
        s_loc, s_ctx = nxt
        if blk + 1 < len(geo):
            nxt = scores(blk + 1)
        m = jnp.maximum(jnp.max(s_loc, axis=-1, keepdims=True), jnp.max(s_ctx, axis=-1, keepdims=True))
        p_loc = jnp.exp2(s_loc - m)
        p_ctx = jnp.exp2(s_ctx - m)
        ox = _dot(p_loc.astype(BF16), vx_scr[ks, :]) + _dot(p_ctx.astype(BF16), vcx_scr[...])
        o_ref[qs, :] = (ox[:, :hd] / ox[:, hd:] * gn_ref[qs, :].astype(F32)).astype(o_ref.dtype)


def _nbr_attention(z, zc, ck, cv, qg, kg, rpb_flat, layer, seq):
    T = z.shape[0]
    rows = seq // GRID_W
    geo = _block_geometry(rows)
    hd = HEAD_DIM
    lat = lambda off: pl.BlockSpec((seq, hd), lambda h, b: (b, off // hd + h))
    return pl.pallas_call(
        functools.partial(_nbr_attn_kernel, geo=geo, rows=rows, layer=layer),
        grid=(NA_HEADS, T // seq),
        in_specs=[
            pl.BlockSpec(memory_space=pltpu.SMEM),
            lat(OFF_Q), lat(OFF_K), lat(OFF_V), lat(OFF_GN),
            pl.BlockSpec((CTX_LEN, hd), lambda h, b: (b, ck + h)),
            pl.BlockSpec((CTX_LEN, hd), lambda h, b: (b, cv + h)),
            pl.BlockSpec((1, 1, hd), lambda h, b: (layer, 0, 0)),
            pl.BlockSpec((1, 1, hd), lambda h, b: (layer, 0, 0)),
        ],
        out_specs=pl.BlockSpec((seq, hd), lambda h, b: (b, h)),
        out_shape=jax.ShapeDtypeStruct((T, W_C), BF16),
        scratch_shapes=[pltpu.VMEM((seq, hd), BF16), pltpu.VMEM((seq, hd), BF16),
                        pltpu.VMEM((CTX_LEN, hd), BF16),
                        pltpu.VMEM((seq, 2 * hd), BF16), pltpu.VMEM((CTX_LEN, 2 * hd), BF16),
                        pltpu.VMEM((3, Q_ROWS * GRID_W, WIN_ROWS * GRID_W), F32)],
        compiler_params=_params("arbitrary", "arbitrary"),
        name="nbr_attn",
    )(rpb_flat, z, z, z, z, zc, zc, qg, kg)


def _ctx_attn_kernel(q_ref, k_ref, v_ref, gn_ref, qg_ref, kg_ref, o_ref):
    for h in range(NA_HEADS):
        cols = slice(h * HEAD_DIM, (h + 1) * HEAD_DIM)
        q = _head_rms(q_ref[:, cols], qg_ref[0] * ATT_SCALE).astype(BF16)
        k = _head_rms(k_ref[:, cols], kg_ref[0]).astype(BF16)
        s = _dot_nt(q, k)
        p = jnp.exp2(s - jnp.max(s, axis=-1, keepdims=True))
        o = _dot(p.astype(BF16), v_ref[:, cols].astype(BF16)) / jnp.sum(p, axis=-1, keepdims=True)
        o_ref[:, cols] = (o * gn_ref[:, cols].astype(F32)).astype(o_ref.dtype)


def _ctx_attention(zc, qg, kg, layer):
    T = zc.shape[0]
    blk = lambda off: pl.BlockSpec((CTX_LEN, W_C), lambda b: (b, off // W_C))
    vec = pl.BlockSpec((1, 1, HEAD_DIM), lambda b: (layer, 0, 0))
    return pl.pallas_call(
        _ctx_attn_kernel,
        grid=(T // CTX_LEN,),
        in_specs=[blk(OFF_Q), blk(OFF_K), blk(OFF_V), blk(OFF_GN), vec, vec],
        out_specs=pl.BlockSpec((CTX_LEN, W_C), lambda b: (b, 0)),
        out_shape=jax.ShapeDtypeStruct((T, W_C), BF16),
        compiler_params=_params("arbitrary"),
        name="ctx_attn",
    )(zc, zc, zc, zc, qg, kg)


def _merge_kernel(u_ref, vg_ref, ga_ref, f_ref, n_ref, ga0, ga1, gf0, gf1, gn0, gn1, x_ref, gate_ref,
                  lng_ref, lnb_ref, ws_ref, bst_ref, wpa_ref, wpf_ref, wpn_ref, wout_ref, o_ref,
                  a_scr, vn_scr, yfn_scr, y_scr):
    f = f_ref[...]
    n = n_ref[...]
    for half, (gf, gn) in enumerate(((gf0, gn0), (gf1, gn1))):
        cols = slice(half * COL, (half + 1) * COL)
        yfn_scr[:, cols] = (gf[...].astype(F32) * _dot(f, wpf_ref[0, :, cols])
                            + gn[...].astype(F32) * _dot(n, wpn_ref[0, :, cols]))
    _spatial_gating(u_ref, vg_ref, ga_ref, lng_ref, lnb_ref, ws_ref, bst_ref, a_scr, vn_scr)
    a = a_scr[...]
    for half, ga in enumerate((ga0, ga1)):
        cols = slice(half * COL, (half + 1) * COL)
        y = yfn_scr[:, cols] + ga[...].astype(F32) * _dot(a, wpa_ref[0, :, cols])
        y_scr[:, cols] = y.astype(BF16)
    o_ref[...] = x_ref[...] + gate_ref[0] * _dot(y_scr[...], wout_ref[0])


def _merge(f, n, z, x2, mod, lng, lnb, ws, bst, wpa, wpf, wpn, wout, layer, *, rows_per_mod, mod_row0):
    T, D = x2.shape
    tm = MERGE_ROWS
    mrow = lambda i: layer * MOD_ROWS + mod_row0 + i // rows_per_mod
    act = pl.BlockSpec((tm, COL), lambda i: (i, 0))
    zblk = lambda c: pl.BlockSpec((tm, COL), lambda i: (i, c))
    const = lambda *shape: pl.BlockSpec((1,) + shape, lambda i: (layer,) + (0,) * len(shape),
                                        pipeline_mode=pl.Buffered(1))
    return pl.pallas_call(
        _merge_kernel,
        grid=(T // tm,),
        in_specs=[
            zblk(OFF_U // COL), zblk(OFF_VG // COL), zblk(OFF_GA // COL),
            act, act,
        ] + [zblk(OFF_MERGE // COL + c) for c in range(3 * D // COL)] + [
            pl.BlockSpec((tm, D), lambda i: (i, 0)),
            pl.BlockSpec((1, 1, D), lambda i: (mrow(i), 0, 2)),
            const(1, W_A), const(1, W_A), const(GMLP_GROUPS, CHUNK, CHUNK), const(CHUNK, GMLP_GROUPS),
            const(W_A, D), const(W_B, D), const(W_C, D), const(D, D),
        ],
        out_specs=pl.BlockSpec((tm, D), lambda i: (i, 0)),
        out_shape=jax.ShapeDtypeStruct((T, D), F32),
        scratch_shapes=[pltpu.VMEM((tm, W_A), BF16), pltpu.VMEM((tm, W_A), BF16),
                        pltpu.VMEM((tm, D), F32), pltpu.VMEM((tm, D), BF16)],
        compiler_params=_params("arbitrary"),
        name="merge_out",
    )(z, z, z, f, n, z, z, z, z, z, z, x2, mod, lng, lnb, ws, bst, wpa, wpf, wpn, wout)


def kernel(x, c, ctx, c_ctx, norm_g, w_ada, b_ada, w_in, gmlp_ln_g, gmlp_ln_b, gmlp_ws, gmlp_bs,
           q_norm_g, k_norm_g, rpb, w_pa, w_pf, w_pn, w_out):
    B, S, D = x.shape
    C = ctx.shape[1]
    L = w_in.shape[0]
    assert D == D_MODEL and C == CTX_LEN and B < MOD_ROWS
    assert S % (GRID_W * Q_ROWS) == 0 and S // GRID_W >= WIN_ROWS

    cc = jnp.concatenate([c, c_ctx[None], jnp.zeros((MOD_ROWS - B - 1, D), F32)], axis=0)
    mod = _modulation(cc, w_ada, b_ada).reshape(L * MOD_ROWS, 1, 3 * D)

    ws = gmlp_ws.astype(BF16)
    bst = jnp.swapaxes(gmlp_bs, 1, 2)
    rpb_flat = rpb.reshape(-1)
    norm_g, gmlp_ln_g, gmlp_ln_b, q_norm_g, k_norm_g = (
        t[:, None, :] for t in (norm_g, gmlp_ln_g, gmlp_ln_b, q_norm_g, k_norm_g))
    dft_lat = _dft_consts(S)
    dft_ctx = _dft_consts(C)
    x2 = x.reshape(B * S, D)
    c2 = ctx.reshape(B * C, D)
    for l in range(L):
        last = l == L - 1
        if last:
            zc, = _in_proj(c2, norm_g, mod, w_in, l, rows_per_mod=ONE_ROW, mod_row0=B,
                           col0=OFF_K // COL, ncol=2)
            ck, cv = 0, W_C // HEAD_DIM
        else:
            zc, = _in_proj(c2, norm_g, mod, w_in, l, rows_per_mod=ONE_ROW, mod_row0=B, col0=0,
                           ncol=W_IN // COL)
            ck, cv = OFF_K // HEAD_DIM, OFF_V // HEAD_DIM
        z, *cast = _in_proj(x2, norm_g, mod, w_in, l, rows_per_mod=S // IN_PROJ_ROWS, mod_row0=0,
                            col0=0, ncol=W_IN // COL, cast=(w_pa, w_pf, w_pn, w_out) if l == 0 else ())
        if l == 0:
            wpa, wpf, wpn, wout = cast

        n = _nbr_attention(z, zc, ck, cv, q_norm_g, k_norm_g, rpb_flat, l, S)
        f = _fourier(z, S, dft_lat)
        x_new = _merge(f, n, z, x2, mod, gmlp_ln_g, gmlp_ln_b, ws, bst, wpa, wpf, wpn, wout, l,
                       rows_per_mod=S // MERGE_ROWS, mod_row0=0)

        if not last:
            nc = _ctx_attention(zc, q_norm_g, k_norm_g, l)
            fc = _fourier(zc, C, dft_ctx)
            c2 = _merge(fc, nc, zc, c2, mod, gmlp_ln_g, gmlp_ln_b, ws, bst, wpa, wpf, wpn, wout, l,
                        rows_per_mod=ONE_ROW, mod_row0=B)
        x2 = x_new
    return x2.reshape(B, S, D)
```

```python
import functools

import jax
import jax.numpy as jnp
import numpy as np
from jax import lax
from jax.experimental import pallas as pl
from jax.experimental.pallas import tpu as pltpu

F32 = jnp.float32
BF16 = jnp.bfloat16

D_MODEL = 2048
CTX_LEN = 256
GRID_W = 64
EPS = 1e-6
CHUNK = 128
GMLP_GROUPS = 8
W_A = D_MODEL // 2
W_B = D_MODEL // 2
F_GROUPS = 4
F_GW = W_B // F_GROUPS
HEAD_DIM = 128
NA_HEADS = 8
W_C = NA_HEADS * HEAD_DIM
NA_KH = 8
NA_KW = 16

OFF_U = 0
OFF_VG = OFF_U + W_A
OFF_GA = OFF_VG + W_A
OFF_F = OFF_GA + W_A
OFF_GF = OFF_F + W_B
OFF_Q = OFF_GF + W_B
OFF_K = OFF_Q + W_C
OFF_V = OFF_K + W_C
OFF_GN = OFF_V + W_C
OFF_MERGE = OFF_GN + W_C
W_IN = OFF_MERGE + 3 * D_MODEL

LANES = 128
BF16_SUBLANES = 16
IN_PROJ_ROWS = 1024
MERGE_ROWS = 256
COL = 1024
MOD_ROWS = 16
ONE_ROW = 1 << 30
VMEM_LIMIT = 56 * 1024 * 1024
Z_DTYPE = jnp.bfloat16

Q_ROWS = 4
WIN_ROWS = 12
NEG = -1e30
LOG2E = float(np.log2(np.e))
ATT_SCALE = HEAD_DIM ** -0.5 * LOG2E


GELU_C1 = float(np.sqrt(2.0 / np.pi))
GELU_C2 = GELU_C1 * 0.044715


def _sigmoid(t):
    return 0.5 * jnp.tanh(0.5 * t) + 0.5


def _silu(t):
    h = 0.5 * t
    return h * jnp.tanh(h) + h


def _gelu(t):
    h = 0.5 * t
    return h * jnp.tanh(t * (GELU_C1 + GELU_C2 * (t * t))) + h


def _dot(a, b):
    return jnp.dot(a, b, preferred_element_type=F32)


def _dot_nt(a, b):
    return lax.dot_general(a, b, (((1,), (1,)), ((), ())), preferred_element_type=F32)


def _params(*sem):
    return pltpu.CompilerParams(dimension_semantics=sem, vmem_limit_bytes=VMEM_LIMIT)


def _mod_kernel(cc_ref, w_ref, b_ref, o_ref):
    s = _silu(cc_ref[...])
    s_hi = s.astype(BF16)
    s_lo = (s - s_hi.astype(F32)).astype(BF16)
    w = w_ref[0]
    w_hi = w.astype(BF16)
    w_lo = (w - w_hi.astype(F32)).astype(BF16)
    o_ref[0] = _dot(s_hi, w_hi) + _dot(s_lo, w_hi) + _dot(s_hi, w_lo) + b_ref[0]


def _modulation(cc, w_ada, b_ada):
    L, D, N = w_ada.shape
    tn = COL
    return pl.pallas_call(
        _mod_kernel,
        grid=(L, N // tn),
        in_specs=[
            pl.BlockSpec((MOD_ROWS, D), lambda l, j: (0, 0)),
            pl.BlockSpec((1, D, tn), lambda l, j: (l, 0, j)),
            pl.BlockSpec((1, 1, tn), lambda l, j: (l, 0, j)),
        ],
        out_specs=pl.BlockSpec((1, MOD_ROWS, tn), lambda l, j: (l, 0, j)),
        out_shape=jax.ShapeDtypeStruct((L, MOD_ROWS, N), F32),
        compiler_params=_params("arbitrary", "arbitrary"),
        name="adaln_mod",
    )(cc, w_ada, b_ada.reshape(L, 1, N))


def _head_rms(t, g):
    return t * lax.rsqrt(jnp.mean(t * t, axis=-1, keepdims=True) + EPS) * g


def _head_norm(gain):
    def act(z):
        heads = [z[:, h * HEAD_DIM:(h + 1) * HEAD_DIM] for h in range(COL // HEAD_DIM)]
        return jnp.concatenate([_head_rms(t, gain) for t in heads], axis=1)
    return act


def _in_proj_kernel(x_ref, g_ref, sh_ref, sc_ref, qg_ref, kg_ref, w_ref, *rest, col0, n_cast):
    cast_in, (o_ref, *cast_out, h_scr) = rest[:n_cast], rest[n_cast:]
    j = pl.program_id(1)

    for src, dst in zip(cast_in, cast_out):
        dst[...] = src[...].astype(dst.dtype)

    @pl.when(j == 0)
    def _():
        x = x_ref[...]
        gain = g_ref[0] * (1.0 + sc_ref[0])
        r = lax.rsqrt(jnp.mean(x * x, axis=-1, keepdims=True) + EPS)
        h_scr[...] = (x * r * gain + sh_ref[0]).astype(BF16)

    c = j + col0

    def project(act):
        o_ref[...] = act(_dot(h_scr[...], w_ref[0].astype(BF16))).astype(o_ref.dtype)

    @pl.when(c < OFF_GA // COL)
    def _():
        project(_gelu)

    @pl.when((c == OFF_GA // COL) | (c == OFF_GF // COL) | (c == OFF_GN // COL))
    def _():
        project(_silu)

    @pl.when(c >= OFF_MERGE // COL)
    def _():
        project(_sigmoid)

    @pl.when((c == OFF_F // COL) | (c == OFF_V // COL))
    def _():
        project(lambda z: z)

    @pl.when(c == OFF_Q // COL)
    def _():
        project(_head_norm(qg_ref[0] * ATT_SCALE))

    @pl.when(c == OFF_K // COL)
    def _():
        project(_head_norm(kg_ref[0]))


def _in_proj(x2, norm_g, mod, qg, kg, w_in, layer, *, rows_per_mod, mod_row0, col0, ncol, cast=()):
    T, D = x2.shape
    tm = min(IN_PROJ_ROWS, T)
    nsteps = (T // tm) * ncol
    mrow = lambda i: layer * MOD_ROWS + mod_row0 + i // rows_per_mod

    cast2d = [t.reshape(-1, t.shape[-1]) for t in cast]
    cast_specs = []
    for t in cast2d:
        rows = BF16_SUBLANES * pl.cdiv(t.shape[0], BF16_SUBLANES * nsteps)
        while t.shape[0] % rows:
            rows += BF16_SUBLANES
        nblk = t.shape[0] // rows
        cast_specs.append(pl.BlockSpec(
            (rows, t.shape[1]), lambda i, j, nblk=nblk: (jnp.minimum(i * ncol + j, nblk - 1), 0)))

    z, *casted = pl.pallas_call(
        functools.partial(_in_proj_kernel, col0=col0, n_cast=len(cast)),
        grid=(T // tm, ncol),
        in_specs=[
            pl.BlockSpec((tm, D), lambda i, j: (i, 0)),
            pl.BlockSpec((1, 1, D), lambda i, j: (layer, 0, 0)),
            pl.BlockSpec((1, 1, D), lambda i, j: (mrow(i), 0, 0)),
            pl.BlockSpec((1, 1, D), lambda i, j: (mrow(i), 0, 1)),
            pl.BlockSpec((1, 1, HEAD_DIM), lambda i, j: (layer, 0, 0)),
            pl.BlockSpec((1, 1, HEAD_DIM), lambda i, j: (layer, 0, 0)),
            pl.BlockSpec((1, D, COL), lambda i, j: (layer, 0, j + col0)),
        ] + cast_specs,
        out_specs=[pl.BlockSpec((tm, COL), lambda i, j: (i, j))] + cast_specs,
        out_shape=[jax.ShapeDtypeStruct((T, ncol * COL), Z_DTYPE)]
        + [jax.ShapeDtypeStruct(t.shape, BF16) for t in cast2d],
        scratch_shapes=[pltpu.VMEM((tm, D), BF16)],
        compiler_params=_params("arbitrary", "arbitrary"),
        name="in_proj",
    )(x2, norm_g, mod, mod, qg, kg, w_in, *cast2d)
    return (z, *(t.reshape(s.shape) for t, s in zip(casted, cast)))


def _spatial_gating(u_ref, vg_ref, ga_ref, lng_ref, lnb_ref, ws_ref, bst_ref, o_ref, vn_scr):
    v = vg_ref[...].astype(F32)
    mu = jnp.mean(v, axis=-1, keepdims=True)
    vc = v - mu
    vn = vc * lax.rsqrt(jnp.mean(vc * vc, axis=-1, keepdims=True) + EPS)
    vn_scr[...] = (vn * lng_ref[0] + lnb_ref[0]).astype(BF16)
    nchunk = u_ref.shape[0] // CHUNK
    for g in range(GMLP_GROUPS):
        cols = slice(g * CHUNK, (g + 1) * CHUNK)
        v_all = jnp.concatenate([vn_scr[ch * CHUNK:(ch + 1) * CHUNK, cols] for ch in range(nchunk)], axis=1)
        sv_all = _dot(ws_ref[0, g], v_all) + bst_ref[0, :, g:g + 1]
        for ch in range(nchunk):
            rows = slice(ch * CHUNK, (ch + 1) * CHUNK)
            sv = sv_all[:, ch * CHUNK:(ch + 1) * CHUNK]
            a = u_ref[rows, cols].astype(F32) * sv * ga_ref[rows, cols].astype(F32)
            o_ref[rows, cols] = a.astype(o_ref.dtype)


DFT_RADIX = 4


def _dft_consts(n):
    m = n // DFT_RADIX
    k = np.arange(m, dtype=np.int64)
    ang = 2.0 * np.pi * ((k[:, None] * k[None, :]) % m).astype(np.float64) / m
    cs = np.concatenate([np.cos(ang), np.sin(ang)], axis=0)
    th = 2.0 * np.pi * np.arange(n // 2, dtype=np.float64) / n
    twc = np.broadcast_to(np.cos(th)[:, None], (n // 2, LANES))
    tws = np.broadcast_to(np.sin(th)[:, None], (n // 2, LANES))
    c = np.arange(F_GW, dtype=np.int64)
    angc = 2.0 * np.pi * ((c[:, None] * c[None, :]) % F_GW).astype(np.float64) / F_GW
    bf = lambda t: jnp.asarray(t, dtype=F32).astype(BF16)
    return bf(cs), jnp.asarray(twc, dtype=F32), jnp.asarray(tws, dtype=F32), bf(np.cos(angc)), bf(np.sin(angc))


def _fourier_kernel(f_ref, gf_ref, cs_ref, twc_ref, tws_ref, cc_ref, sc_ref, o_ref, x_scr):
    n = f_ref.shape[0]
    m0 = n // DFT_RADIX
    lane_tiles = F_GW // LANES

    def position_dft(g):
        for t in range(lane_tiles):
            x_scr[g, t] = f_ref[:, g * F_GW + t * LANES:g * F_GW + (t + 1) * LANES].astype(F32)
        xcat = jnp.concatenate(
            [x_scr[g, t, pl.ds(r, m0, stride=DFT_RADIX), :].astype(BF16)
             for r in range(DFT_RADIX) for t in range(lane_tiles)], axis=1)
        return _dot(cs_ref[...], xcat)

    def butterflies(e):
        base = [(e[:m0, r * F_GW:(r + 1) * F_GW], e[m0:, r * F_GW:(r + 1) * F_GW])
                for r in range(DFT_RADIX)]

        def sub_dft(stride, offset):
            if stride == DFT_RADIX:
                return base[offset]
            ec0, es0 = sub_dft(2 * stride, offset)
            ec1, es1 = sub_dft(2 * stride, offset + stride)
            m = n // (2 * stride)
            c = jnp.concatenate([twc_ref[pl.ds(0, m, stride=stride), :]] * lane_tiles, axis=1)
            s = jnp.concatenate([tws_ref[pl.ds(0, m, stride=stride), :]] * lane_tiles, axis=1)
            a = c * ec1 - s * es1
            b = c * es1 + s * ec1
            return (jnp.concatenate([ec0 + a, ec0 - a], axis=0),
                    jnp.concatenate([es0 + b, es0 - b], axis=0))

        return sub_dft(1, 0)

    nxt = position_dft(0)
    for g in range(F_GROUPS):
        e = nxt
        if g + 1 < F_GROUPS:
            nxt = position_dft(g + 1)
        tc, ts = butterflies(e)
        cols = slice(g * F_GW, (g + 1) * F_GW)
        y = _dot(tc.astype(BF16), cc_ref[...]) - _dot(ts.astype(BF16), sc_ref[...])
        o_ref[:, cols] = (y * ((n * F_GW) ** -0.5) * gf_ref[:, cols].astype(F32)).astype(o_ref.dtype)


def _fourier(z, n, consts):
    T = z.shape[0]
    const = lambda t: pl.BlockSpec(t.shape, lambda b: (0, 0), pipeline_mode=pl.Buffered(1))
    return pl.pallas_call(
        _fourier_kernel,
        grid=(T // n,),
        in_specs=[
            pl.BlockSpec((n, W_B), lambda b: (b, OFF_F // W_B)),
            pl.BlockSpec((n, W_B), lambda b: (b, OFF_GF // W_B)),
        ] + [const(t) for t in consts],
        out_specs=pl.BlockSpec((n, W_B), lambda b: (b, 0)),
        out_shape=jax.ShapeDtypeStruct((T, W_B), BF16),
        scratch_shapes=[pltpu.VMEM((F_GROUPS, F_GW // LANES, n, LANES), F32)],
        compiler_params=_params("arbitrary"),
        name="fourier",
    )(z, z, *consts)


def _block_geometry(rows):
    assert Q_ROWS == NA_KH // 2 and rows % Q_ROWS == 0 and rows // Q_ROWS >= 3
    assert WIN_ROWS % 2 == 0 and Q_ROWS + NA_KH - 1 <= WIN_ROWS <= rows
    geo = []
    nblk = rows // Q_ROWS
    for blk in range(nblk):
        base = min(max(Q_ROWS * blk - NA_KH // 2, 0), rows - WIN_ROWS)
        typ = 0 if blk == 0 else (2 if blk == nblk - 1 else 1)
        geo.append((base, typ))
    return geo


def _build_bias_tables(rpb_ref, tab_scr, head, geo, rows):
    n_dr, n_dc = 2 * NA_KH - 1, 2 * NA_KW - 1
    shape = (GRID_W, 2 * GRID_W)
    lane = lax.broadcasted_iota(jnp.int32, shape, 1)
    cq = lax.broadcasted_iota(jnp.int32, shape, 0)
    kc = lane & (GRID_W - 1)
    dc = kc - cq + (NA_KW - 1)
    cs = jnp.clip(cq - NA_KW // 2, 0, GRID_W - NA_KW)
    col_ok = (kc >= cs) & (kc < cs + NA_KW)
    neg = jnp.full(shape, NEG, F32)
    row_bias = []
    for d in range(n_dr):
        t = neg
        for b in range(n_dc):
            t = jnp.where(dc == b, rpb_ref[(head * n_dr + d) * n_dc + b] * LOG2E, t)
        row_bias.append(jnp.where(col_ok, t, NEG))
    nblk = len(geo)
    for typ, blk in enumerate((0, 1, nblk - 1)):
        base = geo[blk][0]
        for i in range(Q_ROWS):
            r = Q_ROWS * blk + i
            rs = min(max(r - NA_KH // 2, 0), rows - NA_KH)
            for wp in range(WIN_ROWS // 2):
                halves = []
                for kr in (base + 2 * wp, base + 2 * wp + 1):
                    halves.append(row_bias[kr - r + NA_KH - 1] if rs <= kr < rs + NA_KH else neg)
                tab_scr[typ, i * GRID_W:(i + 1) * GRID_W, wp * 2 * GRID_W:(wp + 1) * 2 * GRID_W] = (
                    jnp.where(lane < GRID_W, halves[0], halves[1]))


def _nbr_attn_kernel(rpb_ref, q_ref, k_ref, v_ref, gn_ref, kc_ref, vc_ref, o_ref,
                     vx_scr, vcx_scr, tab_ref, *, geo, rows, layer):
    hd = HEAD_DIM

    @pl.when(pl.program_id(1) == 0)
    def _():
        _build_bias_tables(rpb_ref, tab_ref, layer * NA_HEADS + pl.program_id(0), geo, rows)
        vx_scr[:, hd:] = jnp.ones((vx_scr.shape[0], hd), BF16)
        vcx_scr[:, hd:] = jnp.ones((vcx_scr.shape[0], hd), BF16)

    vx_scr[:, :hd] = v_ref[...].astype(BF16)
    vcx_scr[:, :hd] = vc_ref[...].astype(BF16)
    nq = Q_ROWS * GRID_W
    nk = WIN_ROWS * GRID_W

    def scores(blk):
        base, typ = geo[blk]
        q = q_ref[blk * nq:(blk + 1) * nq, :]
        s_loc = _dot_nt(q, k_ref[base * GRID_W:base * GRID_W + nk, :]) + tab_ref[typ]
        return s_loc, _dot_nt(q, kc_ref[...])

    nxt = scores(0)
    for blk, (base, typ) in enumerate(geo):
        qs = slice(blk * nq, (blk + 1) * nq)
        ks = slice(base * GRID_W, base * GRID_W + nk)
        s_loc, s_ctx = nxt
        if blk + 1 < len(geo):
            nxt = scores(blk + 1)
        m = jnp.maximum(jnp.max(s_loc, axis=-1, keepdims=True), jnp.max(s_ctx, axis=-1, keepdims=True))
        p_loc = jnp.exp2(s_loc - m)
        p_ctx = jnp.exp2(s_ctx - m)
        ox = _dot(p_loc.astype(BF16), vx_scr[ks, :]) + _dot(p_ctx.astype(BF16), vcx_scr[...])
        o_ref[qs, :] = (ox[:, :hd] / ox[:, hd:] * gn_ref[qs, :].astype(F32)).astype(o_ref.dtype)


def _nbr_attention(z, zc, ck, cv, rpb_flat, layer, seq):
    T = z.shape[0]
    rows = seq // GRID_W
    geo = _block_geometry(rows)
    hd = HEAD_DIM
    lat = lambda off: pl.BlockSpec((seq, hd), lambda h, b: (b, off // hd + h))
    return pl.pallas_call(
        functools.partial(_nbr_attn_kernel, geo=geo, rows=rows, layer=layer),
        grid=(NA_HEADS, T // seq),
        in_specs=[
            pl.BlockSpec(memory_space=pltpu.SMEM),
            lat(OFF_Q), lat(OFF_K), lat(OFF_V), lat(OFF_GN),
            pl.BlockSpec((CTX_LEN, hd), lambda h, b: (b, ck + h)),
            pl.BlockSpec((CTX_LEN, hd), lambda h, b: (b, cv + h)),
        ],
        out_specs=pl.BlockSpec((seq, hd), lambda h, b: (b, h)),
        out_shape=jax.ShapeDtypeStruct((T, W_C), BF16),
        scratch_shapes=[pltpu.VMEM((seq, 2 * hd), BF16), pltpu.VMEM((CTX_LEN, 2 * hd), BF16),
                        pltpu.VMEM((3, Q_ROWS * GRID_W, WIN_ROWS * GRID_W), F32)],
        compiler_params=_params("arbitrary", "arbitrary"),
        name="nbr_attn",
    )(rpb_flat, z, z, z, z, zc, zc)


def _ctx_attn_kernel(q_ref, k_ref, v_ref, gn_ref, o_ref):
    for h in range(NA_HEADS):
        cols = slice(h * HEAD_DIM, (h + 1) * HEAD_DIM)
        s = _dot_nt(q_ref[:, cols], k_ref[:, cols])
        p = jnp.exp2(s - jnp.max(s, axis=-1, keepdims=True))
        o = _dot(p.astype(BF16), v_ref[:, cols].astype(BF16)) / jnp.sum(p, axis=-1, keepdims=True)
        o_ref[:, cols] = (o * gn_ref[:, cols].astype(F32)).astype(o_ref.dtype)


def _ctx_attention(zc):
    T = zc.shape[0]
    blk = lambda off: pl.BlockSpec((CTX_LEN, W_C), lambda b: (b, off // W_C))
    return pl.pallas_call(
        _ctx_attn_kernel,
        grid=(T // CTX_LEN,),
        in_specs=[blk(OFF_Q), blk(OFF_K), blk(OFF_V), blk(OFF_GN)],
        out_specs=pl.BlockSpec((CTX_LEN, W_C), lambda b: (b, 0)),
        out_shape=jax.ShapeDtypeStruct((T, W_C), BF16),
        compiler_params=_params("arbitrary"),
        name="ctx_attn",
    )(zc, zc, zc, zc)


def _merge_kernel(u_ref, vg_ref, ga_ref, f_ref, n_ref, ga0, ga1, gf0, gf1, gn0, gn1, x_ref, gate_ref,
                  lng_ref, lnb_ref, ws_ref, bst_ref, wpa_ref, wpf_ref, wpn_ref, wout_ref, o_ref,
                  a_scr, vn_scr, yfn_scr, y_scr):
    f = f_ref[...]
    n = n_ref[...]
    for half, (gf, gn) in enumerate(((gf0, gn0), (gf1, gn1))):
        cols = slice(half * COL, (half + 1) * COL)
        yfn_scr[:, cols] = (gf[...].astype(F32) * _dot(f, wpf_ref[0, :, cols])
                            + gn[...].astype(F32) * _dot(n, wpn_ref[0, :, cols]))
    _spatial_gating(u_ref, vg_ref, ga_ref, lng_ref, lnb_ref, ws_ref, bst_ref, a_scr, vn_scr)
    a = a_scr[...]
    for half, ga in enumerate((ga0, ga1)):
        cols = slice(half * COL, (half + 1) * COL)
        y = yfn_scr[:, cols] + ga[...].astype(F32) * _dot(a, wpa_ref[0, :, cols])
        y_scr[:, cols] = y.astype(BF16)
    o_ref[...] = x_ref[...] + gate_ref[0] * _dot(y_scr[...], wout_ref[0])


def _merge(f, n, z, x2, mod, lng, lnb, ws, bst, wpa, wpf, wpn, wout, layer, *, rows_per_mod, mod_row0):
    T, D = x2.shape
    tm = MERGE_ROWS
    mrow = lambda i: layer * MOD_ROWS + mod_row0 + i // rows_per_mod
    act = pl.BlockSpec((tm, COL), lambda i: (i, 0))
    zblk = lambda c: pl.BlockSpec((tm, COL), lambda i: (i, c))
    const = lambda *shape: pl.BlockSpec((1,) + shape, lambda i: (layer,) + (0,) * len(shape),
                                        pipeline_mode=pl.Buffered(1))
    return pl.pallas_call(
        _merge_kernel,
        grid=(T // tm,),
        in_specs=[
            zblk(OFF_U // COL), zblk(OFF_VG // COL), zblk(OFF_GA // COL),
            act, act,
        ] + [zblk(OFF_MERGE // COL + c) for c in range(3 * D // COL)] + [
            pl.BlockSpec((tm, D), lambda i: (i, 0)),
            pl.BlockSpec((1, 1, D), lambda i: (mrow(i), 0, 2)),
            const(1, W_A), const(1, W_A), const(GMLP_GROUPS, CHUNK, CHUNK), const(CHUNK, GMLP_GROUPS),
            const(W_A, D), const(W_B, D), const(W_C, D), const(D, D),
        ],
        out_specs=pl.BlockSpec((tm, D), lambda i: (i, 0)),
        out_shape=jax.ShapeDtypeStruct((T, D), F32),
        scratch_shapes=[pltpu.VMEM((tm, W_A), BF16), pltpu.VMEM((tm, W_A), BF16),
                        pltpu.VMEM((tm, D), F32), pltpu.VMEM((tm, D), BF16)],
        compiler_params=_params("arbitrary"),
        name="merge_out",
    )(z, z, z, f, n, z, z, z, z, z, z, x2, mod, lng, lnb, ws, bst, wpa, wpf, wpn, wout)


def kernel(x, c, ctx, c_ctx, norm_g, w_ada, b_ada, w_in, gmlp_ln_g, gmlp_ln_b, gmlp_ws, gmlp_bs,
           q_norm_g, k_norm_g, rpb, w_pa, w_pf, w_pn, w_out):
    B, S, D = x.shape
    C = ctx.shape[1]
    L = w_in.shape[0]
    assert D == D_MODEL and C == CTX_LEN and B < MOD_ROWS
    assert S % (GRID_W * Q_ROWS) == 0 and S // GRID_W >= WIN_ROWS

    cc = jnp.concatenate([c, c_ctx[None], jnp.zeros((MOD_ROWS - B - 1, D), F32)], axis=0)
    mod = _modulation(cc, w_ada, b_ada).reshape(L * MOD_ROWS, 1, 3 * D)

    ws = gmlp_ws.astype(BF16)
    bst = jnp.swapaxes(gmlp_bs, 1, 2)
    rpb_flat = rpb.reshape(-1)
    norm_g, gmlp_ln_g, gmlp_ln_b, q_norm_g, k_norm_g = (
        t[:, None, :] for t in (norm_g, gmlp_ln_g, gmlp_ln_b, q_norm_g, k_norm_g))
    dft_lat = _dft_consts(S)
    dft_ctx = _dft_consts(C)
    x2 = x.reshape(B * S, D)
    c2 = ctx.reshape(B * C, D)
    for l in range(L):
        last = l == L - 1
        if last:
            zc, = _in_proj(c2, norm_g, mod, q_norm_g, k_norm_g, w_in, l, rows_per_mod=ONE_ROW, mod_row0=B,
                           col0=OFF_K // COL, ncol=2)
            ck, cv = 0, W_C // HEAD_DIM
        else:
            zc, = _in_proj(c2, norm_g, mod, q_norm_g, k_norm_g, w_in, l, rows_per_mod=ONE_ROW, mod_row0=B, col0=0,
                           ncol=W_IN // COL)
            ck, cv = OFF_K // HEAD_DIM, OFF_V // HEAD_DIM
        z, *cast = _in_proj(x2, norm_g, mod, q_norm_g, k_norm_g, w_in, l, rows_per_mod=S // IN_PROJ_ROWS, mod_row0=0,
                            col0=0, ncol=W_IN // COL, cast=(w_pa, w_pf, w_pn, w_out) if l == 0 else ())
        if l == 0:
            wpa, wpf, wpn, wout = cast

        n = _nbr_attention(z, zc, ck, cv, rpb_flat, l, S)
        f = _fourier(z, S, dft_lat)
        x_new = _merge(f, n, z, x2, mod, gmlp_ln_g, gmlp_ln_b, ws, bst, wpa, wpf, wpn, wout, l,
                       rows_per_mod=S // MERGE_ROWS, mod_row0=0)

        if not last:
            nc = _ctx_attention(zc)
            fc = _fourier(zc, C, dft_ctx)
            c2 = _merge(fc, nc, zc, c2, mod, gmlp_ln_g, gmlp_ln_b, ws, bst, wpa, wpf, wpn, wout, l,
                        rows_per_mod=ONE_ROW, mod_row0=B)
        x2 = x_new
    return x2.reshape(B, S, D)
```

```python
import functools

import jax
import jax.numpy as jnp
import numpy as np
from jax import lax
from jax.experimental import pallas as pl
from jax.experimental.pallas import tpu as pltpu

F32 = jnp.float32
BF16 = jnp.bfloat16

D_MODEL = 2048
CTX_LEN = 256
GRID_W = 64
EPS = 1e-6
CHUNK = 128
GMLP_GROUPS = 8
W_A = D_MODEL // 2
W_B = D_MODEL // 2
F_GROUPS = 4
F_GW = W_B // F_GROUPS
HEAD_DIM = 128
NA_HEADS = 8
W_C = NA_HEADS * HEAD_DIM
NA_KH = 8
NA_KW = 16

OFF_U = 0
OFF_VG = OFF_U + W_A
OFF_GA = OFF_VG + W_A
OFF_F = OFF_GA + W_A
OFF_GF = OFF_F + W_B
OFF_Q = OFF_GF + W_B
OFF_K = OFF_Q + W_C
OFF_V = OFF_K + W_C
OFF_GN = OFF_V + W_C
OFF_MERGE = OFF_GN + W_C
W_IN = OFF_MERGE + 3 * D_MODEL

LANES = 128
BF16_SUBLANES = 16
IN_PROJ_ROWS = 1024
MERGE_ROWS = 256
COL = 1024
MOD_ROWS = 16
ONE_ROW = 1 << 30
VMEM_LIMIT = 56 * 1024 * 1024
Z_DTYPE = jnp.bfloat16

Q_ROWS = 2
WIN_ROWS = 10
SCORE_LOOKAHEAD = 3
NEG = -1e30
LOG2E = float(np.log2(np.e))
ATT_SCALE = HEAD_DIM ** -0.5 * LOG2E


GELU_C1 = float(np.sqrt(2.0 / np.pi))
GELU_C2 = GELU_C1 * 0.044715


def _sigmoid(t):
    return 0.5 * jnp.tanh(0.5 * t) + 0.5


def _silu(t):
    h = 0.5 * t
    return h * jnp.tanh(h) + h


def _gelu(t):
    h = 0.5 * t
    return h * jnp.tanh(t * (GELU_C1 + GELU_C2 * (t * t))) + h


def _dot(a, b):
    return jnp.dot(a, b, preferred_element_type=F32)


def _dot_nt(a, b):
    return lax.dot_general(a, b, (((1,), (1,)), ((), ())), preferred_element_type=F32)


def _params(*sem):
    return pltpu.CompilerParams(dimension_semantics=sem, vmem_limit_bytes=VMEM_LIMIT)


def _mod_kernel(cc_ref, w_ref, b_ref, o_ref):
    s = _silu(cc_ref[...])
    s_hi = s.astype(BF16)
    s_lo = (s - s_hi.astype(F32)).astype(BF16)
    w = w_ref[0]
    w_hi = w.astype(BF16)
    w_lo = (w - w_hi.astype(F32)).astype(BF16)
    o_ref[0] = _dot(s_hi, w_hi) + _dot(s_lo, w_hi) + _dot(s_hi, w_lo) + b_ref[0]


def _modulation(cc, w_ada, b_ada):
    L, D, N = w_ada.shape
    tn = COL
    return pl.pallas_call(
        _mod_kernel,
        grid=(L, N // tn),
        in_specs=[
            pl.BlockSpec((MOD_ROWS, D), lambda l, j: (0, 0)),
            pl.BlockSpec((1, D, tn), lambda l, j: (l, 0, j)),
            pl.BlockSpec((1, 1, tn), lambda l, j: (l, 0, j)),
        ],
        out_specs=pl.BlockSpec((1, MOD_ROWS, tn), lambda l, j: (l, 0, j)),
        out_shape=jax.ShapeDtypeStruct((L, MOD_ROWS, N), F32),
        compiler_params=_params("arbitrary", "arbitrary"),
        name="adaln_mod",
    )(cc, w_ada, b_ada.reshape(L, 1, N))


def _in_proj_kernel(x_ref, g_ref, sh_ref, sc_ref, w_ref, *rest, col0, n_cast):
    cast_in, (o_ref, *cast_out, h_scr) = rest[:n_cast], rest[n_cast:]
    j = pl.program_id(1)

    for src, dst in zip(cast_in, cast_out):
        dst[...] = src[...].astype(dst.dtype)

    @pl.when(j == 0)
    def _():
        x = x_ref[...]
        gain = g_ref[0] * (1.0 + sc_ref[0])
        r = lax.rsqrt(jnp.mean(x * x, axis=-1, keepdims=True) + EPS)
        h_scr[...] = (x * r * gain + sh_ref[0]).astype(BF16)

    c = j + col0

    def project(act):
        o_ref[...] = act(_dot(h_scr[...], w_ref[0].astype(BF16))).astype(o_ref.dtype)

    @pl.when(c < OFF_GA // COL)
    def _():
        project(_gelu)

    @pl.when((c == OFF_GA // COL) | (c == OFF_GF // COL) | (c == OFF_GN // COL))
    def _():
        project(_silu)

    @pl.when(c >= OFF_MERGE // COL)
    def _():
        project(_sigmoid)

    @pl.when((c == OFF_F // COL) | ((c >= OFF_Q // COL) & (c < OFF_GN // COL)))
    def _():
        project(lambda z: z)


def _in_proj(x2, norm_g, mod, w_in, layer, *, rows_per_mod, mod_row0, col0, ncol, cast=()):
    T, D = x2.shape
    tm = min(IN_PROJ_ROWS, T)
    nsteps = (T // tm) * ncol
    mrow = lambda i: layer * MOD_ROWS + mod_row0 + i // rows_per_mod

    cast2d = [t.reshape(-1, t.shape[-1]) for t in cast]
    cast_specs = []
    for t in cast2d:
        rows = BF16_SUBLANES * pl.cdiv(t.shape[0], BF16_SUBLANES * nsteps)
        while t.shape[0] % rows:
            rows += BF16_SUBLANES
        nblk = t.shape[0] // rows
        cast_specs.append(pl.BlockSpec(
            (rows, t.shape[1]), lambda i, j, nblk=nblk: (jnp.minimum(i * ncol + j, nblk - 1), 0)))

    z, *casted = pl.pallas_call(
        functools.partial(_in_proj_kernel, col0=col0, n_cast=len(cast)),
        grid=(T // tm, ncol),
        in_specs=[
            pl.BlockSpec((tm, D), lambda i, j: (i, 0)),
            pl.BlockSpec((1, 1, D), lambda i, j: (layer, 0, 0)),
            pl.BlockSpec((1, 1, D), lambda i, j: (mrow(i), 0, 0)),
            pl.BlockSpec((1, 1, D), lambda i, j: (mrow(i), 0, 1)),
            pl.BlockSpec((1, D, COL), lambda i, j: (layer, 0, j + col0)),
        ] + cast_specs,
        out_specs=[pl.BlockSpec((tm, COL), lambda i, j: (i, j))] + cast_specs,
        out_shape=[jax.ShapeDtypeStruct((T, ncol * COL), Z_DTYPE)]
        + [jax.ShapeDtypeStruct(t.shape, BF16) for t in cast2d],
        scratch_shapes=[pltpu.VMEM((tm, D), BF16)],
        compiler_params=_params("arbitrary", "arbitrary"),
        name="in_proj",
    )(x2, norm_g, mod, mod, w_in, *cast2d)
    return (z, *(t.reshape(s.shape) for t, s in zip(casted, cast)))


def _spatial_gating(u_ref, vg_ref, ga_ref, lng_ref, lnb_ref, ws_ref, bst_ref, o_ref, vn_scr):
    v = vg_ref[...].astype(F32)
    mu = jnp.mean(v, axis=-1, keepdims=True)
    vc = v - mu
    vn = vc * lax.rsqrt(jnp.mean(vc * vc, axis=-1, keepdims=True) + EPS)
    vn_scr[...] = (vn * lng_ref[0] + lnb_ref[0]).astype(BF16)
    nchunk = u_ref.shape[0] // CHUNK
    for g in range(GMLP_GROUPS):
        cols = slice(g * CHUNK, (g + 1) * CHUNK)
        v_all = jnp.concatenate([vn_scr[ch * CHUNK:(ch + 1) * CHUNK, cols] for ch in range(nchunk)], axis=1)
        sv_all = _dot(ws_ref[0, g], v_all) + bst_ref[0, :, g:g + 1]
        for ch in range(nchunk):
            rows = slice(ch * CHUNK, (ch + 1) * CHUNK)
            sv = sv_all[:, ch * CHUNK:(ch + 1) * CHUNK]
            a = u_ref[rows, cols].astype(F32) * sv * ga_ref[rows, cols].astype(F32)
            o_ref[rows, cols] = a.astype(o_ref.dtype)


DFT_RADIX = 4


def _dft_consts(n):
    m = n // DFT_RADIX
    k = np.arange(m, dtype=np.int64)
    ang = 2.0 * np.pi * ((k[:, None] * k[None, :]) % m).astype(np.float64) / m
    cs = np.concatenate([np.cos(ang), np.sin(ang)], axis=0)
    th = 2.0 * np.pi * np.arange(n // 2, dtype=np.float64) / n
    twc = np.broadcast_to(np.cos(th)[:, None], (n // 2, LANES))
    tws = np.broadcast_to(np.sin(th)[:, None], (n // 2, LANES))
    c = np.arange(F_GW, dtype=np.int64)
    angc = 2.0 * np.pi * ((c[:, None] * c[None, :]) % F_GW).astype(np.float64) / F_GW
    bf = lambda t: jnp.asarray(t, dtype=F32).astype(BF16)
    return bf(cs), jnp.asarray(twc, dtype=F32), jnp.asarray(tws, dtype=F32), bf(np.cos(angc)), bf(np.sin(angc))


def _fourier_kernel(f_ref, gf_ref, cs_ref, twc_ref, tws_ref, cc_ref, sc_ref, o_ref, x_scr):
    n = f_ref.shape[0]
    m0 = n // DFT_RADIX
    lane_tiles = F_GW // LANES

    def position_dft(g):
        for t in range(lane_tiles):
            x_scr[g, t] = f_ref[:, g * F_GW + t * LANES:g * F_GW + (t + 1) * LANES].astype(F32)
        xcat = jnp.concatenate(
            [x_scr[g, t, pl.ds(r, m0, stride=DFT_RADIX), :].astype(BF16)
             for r in range(DFT_RADIX) for t in range(lane_tiles)], axis=1)
        return _dot(cs_ref[...], xcat)

    def butterflies(e):
        base = [(e[:m0, r * F_GW:(r + 1) * F_GW], e[m0:, r * F_GW:(r + 1) * F_GW])
                for r in range(DFT_RADIX)]

        def sub_dft(stride, offset):
            if stride == DFT_RADIX:
                return base[offset]
            ec0, es0 = sub_dft(2 * stride, offset)
            ec1, es1 = sub_dft(2 * stride, offset + stride)
            m = n // (2 * stride)
            c = jnp.concatenate([twc_ref[pl.ds(0, m, stride=stride), :]] * lane_tiles, axis=1)
            s = jnp.concatenate([tws_ref[pl.ds(0, m, stride=stride), :]] * lane_tiles, axis=1)
            a = c * ec1 - s * es1
            b = c * es1 + s * ec1
            return (jnp.concatenate([ec0 + a, ec0 - a], axis=0),
                    jnp.concatenate([es0 + b, es0 - b], axis=0))

        return sub_dft(1, 0)

    nxt = position_dft(0)
    for g in range(F_GROUPS):
        e = nxt
        if g + 1 < F_GROUPS:
            nxt = position_dft(g + 1)
        tc, ts = butterflies(e)
        cols = slice(g * F_GW, (g + 1) * F_GW)
        y = _dot(tc.astype(BF16), cc_ref[...]) - _dot(ts.astype(BF16), sc_ref[...])
        o_ref[:, cols] = (y * ((n * F_GW) ** -0.5) * gf_ref[:, cols].astype(F32)).astype(o_ref.dtype)


def _fourier(z, n, consts):
    T = z.shape[0]
    const = lambda t: pl.BlockSpec(t.shape, lambda b: (0, 0), pipeline_mode=pl.Buffered(1))
    return pl.pallas_call(
        _fourier_kernel,
        grid=(T // n,),
        in_specs=[
            pl.BlockSpec((n, W_B), lambda b: (b, OFF_F // W_B)),
            pl.BlockSpec((n, W_B), lambda b: (b, OFF_GF // W_B)),
        ] + [const(t) for t in consts],
        out_specs=pl.BlockSpec((n, W_B), lambda b: (b, 0)),
        out_shape=jax.ShapeDtypeStruct((T, W_B), BF16),
        scratch_shapes=[pltpu.VMEM((F_GROUPS, F_GW // LANES, n, LANES), F32)],
        compiler_params=_params("arbitrary"),
        name="fourier",
    )(z, z, *consts)


def _head_rms(t, g):
    t = t.astype(F32)
    return t * lax.rsqrt(jnp.mean(t * t, axis=-1, keepdims=True) + EPS) * g


def _block_geometry(rows):
    assert rows % Q_ROWS == 0 and WIN_ROWS % 2 == 0 and Q_ROWS + NA_KH - 1 <= WIN_ROWS <= rows
    geo, types = [], {}
    for blk in range(rows // Q_ROWS):
        base = min(max(Q_ROWS * blk - NA_KH // 2, 0), rows - WIN_ROWS)
        starts = tuple(_window_start(Q_ROWS * blk + i, rows) - base for i in range(Q_ROWS))
        signature = (base - Q_ROWS * blk, starts)
        geo.append((base, types.setdefault(signature, len(types))))
    return geo


def _window_start(r, rows):
    return min(max(r - NA_KH // 2, 0), rows - NA_KH)


def _build_bias_tables(rpb_ref, tab_scr, head, geo, rows):
    n_dr, n_dc = 2 * NA_KH - 1, 2 * NA_KW - 1
    shape = (GRID_W, 2 * GRID_W)
    lane = lax.broadcasted_iota(jnp.int32, shape, 1)
    cq = lax.broadcasted_iota(jnp.int32, shape, 0)
    kc = lane & (GRID_W - 1)
    dc = kc - cq + (NA_KW - 1)
    cs = jnp.clip(cq - NA_KW // 2, 0, GRID_W - NA_KW)
    col_ok = (kc >= cs) & (kc < cs + NA_KW)
    neg = jnp.full(shape, NEG, F32)
    row_bias = []
    for d in range(n_dr):
        t = neg
        for b in range(n_dc):
            t = jnp.where(dc == b, rpb_ref[(head * n_dr + d) * n_dc + b] * LOG2E, t)
        row_bias.append(jnp.where(col_ok, t, NEG))
    first_block = {typ: blk for blk, (_, typ) in reversed(list(enumerate(geo)))}
    for typ, blk in first_block.items():
        base = geo[blk][0]
        for i in range(Q_ROWS):
            r = Q_ROWS * blk + i
            rs = _window_start(r, rows)
            for wp in range(WIN_ROWS // 2):
                halves = []
                for kr in (base + 2 * wp, base + 2 * wp + 1):
                    halves.append(row_bias[kr - r + NA_KH - 1] if rs <= kr < rs + NA_KH else neg)
                tab_scr[typ, i * GRID_W:(i + 1) * GRID_W, wp * 2 * GRID_W:(wp + 1) * 2 * GRID_W] = (
                    jnp.where(lane < GRID_W, halves[0], halves[1]))


def _nbr_attn_kernel(rpb_ref, q_ref, k_ref, v_ref, gn_ref, kc_ref, vc_ref, qg_ref, kg_ref, o_ref,
                     qn_scr, kn_scr, kcn_scr, vx_scr, vcx_scr, tab_ref, *, geo, rows, layer):
    hd = HEAD_DIM

    @pl.when(pl.program_id(1) == 0)
    def _():
        _build_bias_tables(rpb_ref, tab_ref, layer * NA_HEADS + pl.program_id(0), geo, rows)
        vx_scr[:, hd:] = jnp.ones((vx_scr.shape[0], hd), BF16)
        vcx_scr[:, hd:] = jnp.ones((vcx_scr.shape[0], hd), BF16)

    qn_scr[...] = _head_rms(q_ref[...], qg_ref[0] * ATT_SCALE).astype(BF16)
    kn_scr[...] = _head_rms(k_ref[...], kg_ref[0]).astype(BF16)
    kcn_scr[...] = _head_rms(kc_ref[...], kg_ref[0]).astype(BF16)
    vx_scr[:, :hd] = v_ref[...].astype(BF16)
    vcx_scr[:, :hd] = vc_ref[...].astype(BF16)
    nq = Q_ROWS * GRID_W
    nk = WIN_ROWS * GRID_W

    def scores(blk):
        base, typ = geo[blk]
        q = qn_scr[blk * nq:(blk + 1) * nq, :]
        s_loc = _dot_nt(q, kn_scr[base * GRID_W:base * GRID_W + nk, :]) + tab_ref[typ]
        return s_loc, _dot_nt(q, kcn_scr[...])

    pending = [scores(b) for b in range(min(SCORE_LOOKAHEAD, len(geo)))]
    for blk, (base, typ) in enumerate(geo):
        qs = slice(blk * nq, (blk + 1) * nq)
        ks = slice(base * GRID_W, base * GRID_W + nk)
        s_loc, s_ctx = pending.pop(0)
        if blk + SCORE_LOOKAHEAD < len(geo):
            pending.append(scores(blk + SCORE_LOOKAHEAD))
        m = jnp.maximum(jnp.max(s_loc, axis=-1, keepdims=True), jnp.max(s_ctx, axis=-1, keepdims=True))
        p_loc = jnp.exp2(s_loc - m)
        p_ctx = jnp.exp2(s_ctx - m)
        ox = _dot(p_loc.astype(BF16), vx_scr[ks, :]) + _dot(p_ctx.astype(BF16), vcx_scr[...])
        o_ref[qs, :] = (ox[:, :hd] / ox[:, hd:] * gn_ref[qs, :].astype(F32)).astype(o_ref.dtype)


def _nbr_attention(z, zc, ck, cv, qg, kg, rpb_flat, layer, seq):
    T = z.shape[0]
    rows = seq // GRID_W
    geo = _block_geometry(rows)
    hd = HEAD_DIM
    lat = lambda off: pl.BlockSpec((seq, hd), lambda h, b: (b, off // hd + h))
    return pl.pallas_call(
        functools.partial(_nbr_attn_kernel, geo=geo, rows=rows, layer=layer),
        grid=(NA_HEADS, T // seq),
        in_specs=[
            pl.BlockSpec(memory_space=pltpu.SMEM),
            lat(OFF_Q), lat(OFF_K), lat(OFF_V), lat(OFF_GN),
            pl.BlockSpec((CTX_LEN, hd), lambda h, b: (b, ck + h)),
            pl.BlockSpec((CTX_LEN, hd), lambda h, b: (b, cv + h)),
            pl.BlockSpec((1, 1, hd), lambda h, b: (layer, 0, 0)),
            pl.BlockSpec((1, 1, hd), lambda h, b: (layer, 0, 0)),
        ],
        out_specs=pl.BlockSpec((seq, hd), lambda h, b: (b, h)),
        out_shape=jax.ShapeDtypeStruct((T, W_C), BF16),
        scratch_shapes=[pltpu.VMEM((seq, hd), BF16), pltpu.VMEM((seq, hd), BF16),
                        pltpu.VMEM((CTX_LEN, hd), BF16),
                        pltpu.VMEM((seq, 2 * hd), BF16), pltpu.VMEM((CTX_LEN, 2 * hd), BF16),
                        pltpu.VMEM((1 + max(t for _, t in geo), Q_ROWS * GRID_W, WIN_ROWS * GRID_W), F32)],
        compiler_params=_params("arbitrary", "arbitrary"),
        name="nbr_attn",
    )(rpb_flat, z, z, z, z, zc, zc, qg, kg)


def _ctx_attn_kernel(q_ref, k_ref, v_ref, gn_ref, qg_ref, kg_ref, o_ref):
    for h in range(NA_HEADS):
        cols = slice(h * HEAD_DIM, (h + 1) * HEAD_DIM)
        q = _head_rms(q_ref[:, cols], qg_ref[0] * ATT_SCALE).astype(BF16)
        k = _head_rms(k_ref[:, cols], kg_ref[0]).astype(BF16)
        s = _dot_nt(q, k)
        p = jnp.exp2(s - jnp.max(s, axis=-1, keepdims=True))
        o = _dot(p.astype(BF16), v_ref[:, cols].astype(BF16)) / jnp.sum(p, axis=-1, keepdims=True)
        o_ref[:, cols] = (o * gn_ref[:, cols].astype(F32)).astype(o_ref.dtype)


def _ctx_attention(zc, qg, kg, layer):
    T = zc.shape[0]
    blk = lambda off: pl.BlockSpec((CTX_LEN, W_C), lambda b: (b, off // W_C))
    vec = pl.BlockSpec((1, 1, HEAD_DIM), lambda b: (layer, 0, 0))
    return pl.pallas_call(
        _ctx_attn_kernel,
        grid=(T // CTX_LEN,),
        in_specs=[blk(OFF_Q), blk(OFF_K), blk(OFF_V), blk(OFF_GN), vec, vec],
        out_specs=pl.BlockSpec((CTX_LEN, W_C), lambda b: (b, 0)),
        out_shape=jax.ShapeDtypeStruct((T, W_C), BF16),
        compiler_params=_params("arbitrary"),
        name="ctx_attn",
    )(zc, zc, zc, zc, qg, kg)


def _merge_kernel(u_ref, vg_ref, ga_ref, f_ref, n_ref, ga0, ga1, gf0, gf1, gn0, gn1, x_ref, gate_ref,
                  lng_ref, lnb_ref, ws_ref, bst_ref, wpa_ref, wpf_ref, wpn_ref, wout_ref, o_ref,
                  a_scr, vn_scr, yfn_scr, y_scr):
    f = f_ref[...]
    n = n_ref[...]
    for half, (gf, gn) in enumerate(((gf0, gn0), (gf1, gn1))):
        cols = slice(half * COL, (half + 1) * COL)
        yfn_scr[:, cols] = (gf[...].astype(F32) * _dot(f, wpf_ref[0, :, cols])
                            + gn[...].astype(F32) * _dot(n, wpn_ref[0, :, cols]))
    _spatial_gating(u_ref, vg_ref, ga_ref, lng_ref, lnb_ref, ws_ref, bst_ref, a_scr, vn_scr)
    a = a_scr[...]
    for half, ga in enumerate((ga0, ga1)):
        cols = slice(half * COL, (half + 1) * COL)
        y = yfn_scr[:, cols] + ga[...].astype(F32) * _dot(a, wpa_ref[0, :, cols])
        y_scr[:, cols] = y.astype(BF16)
    o_ref[...] = x_ref[...] + gate_ref[0] * _dot(y_scr[...], wout_ref[0])


def _merge(f, n, z, x2, mod, lng, lnb, ws, bst, wpa, wpf, wpn, wout, layer, *, rows_per_mod, mod_row0):
    T, D = x2.shape
    tm = MERGE_ROWS
    mrow = lambda i: layer * MOD_ROWS + mod_row0 + i // rows_per_mod
    act = pl.BlockSpec((tm, COL), lambda i: (i, 0))
    zblk = lambda c: pl.BlockSpec((tm, COL), lambda i: (i, c))
    const = lambda *shape: pl.BlockSpec((1,) + shape, lambda i: (layer,) + (0,) * len(shape),
                                        pipeline_mode=pl.Buffered(1))
    return pl.pallas_call(
        _merge_kernel,
        grid=(T // tm,),
        in_specs=[
            zblk(OFF_U // COL), zblk(OFF_VG // COL), zblk(OFF_GA // COL),
            act, act,
        ] + [zblk(OFF_MERGE // COL + c) for c in range(3 * D // COL)] + [
            pl.BlockSpec((tm, D), lambda i: (i, 0)),
            pl.BlockSpec((1, 1, D), lambda i: (mrow(i), 0, 2)),
            const(1, W_A), const(1, W_A), const(GMLP_GROUPS, CHUNK, CHUNK), const(CHUNK, GMLP_GROUPS),
            const(W_A, D), const(W_B, D), const(W_C, D), const(D, D),
        ],
        out_specs=pl.BlockSpec((tm, D), lambda i: (i, 0)),
        out_shape=jax.ShapeDtypeStruct((T, D), F32),
        scratch_shapes=[pltpu.VMEM((tm, W_A), BF16), pltpu.VMEM((tm, W_A), BF16),
                        pltpu.VMEM((tm, D), F32), pltpu.VMEM((tm, D), BF16)],
        compiler_params=_params("arbitrary"),
        name="merge_out",
    )(z, z, z, f, n, z, z, z, z, z, z, x2, mod, lng, lnb, ws, bst, wpa, wpf, wpn, wout)


def kernel(x, c, ctx, c_ctx, norm_g, w_ada, b_ada, w_in, gmlp_ln_g, gmlp_ln_b, gmlp_ws, gmlp_bs,
           q_norm_g, k_norm_g, rpb, w_pa, w_pf, w_pn, w_out):
    B, S, D = x.shape
    C = ctx.shape[1]
    L = w_in.shape[0]
    assert D == D_MODEL and C == CTX_LEN and B < MOD_ROWS
    assert S % (GRID_W * Q_ROWS) == 0 and S // GRID_W >= WIN_ROWS

    cc = jnp.concatenate([c, c_ctx[None], jnp.zeros((MOD_ROWS - B - 1, D), F32)], axis=0)
    mod = _modulation(cc, w_ada, b_ada).reshape(L * MOD_ROWS, 1, 3 * D)

    ws = gmlp_ws.astype(BF16)
    bst = jnp.swapaxes(gmlp_bs, 1, 2)
    rpb_flat = rpb.reshape(-1)
    norm_g, gmlp_ln_g, gmlp_ln_b, q_norm_g, k_norm_g = (
        t[:, None, :] for t in (norm_g, gmlp_ln_g, gmlp_ln_b, q_norm_g, k_norm_g))
    dft_lat = _dft_consts(S)
    dft_ctx = _dft_consts(C)
    x2 = x.reshape(B * S, D)
    c2 = ctx.reshape(B * C, D)
    for l in range(L):
        last = l == L - 1
        if last:
            zc, = _in_proj(c2, norm_g, mod, w_in, l, rows_per_mod=ONE_ROW, mod_row0=B,
                           col0=OFF_K // COL, ncol=2)
            ck, cv = 0, W_C // HEAD_DIM
        else:
            zc, = _in_proj(c2, norm_g, mod, w_in, l, rows_per_mod=ONE_ROW, mod_row0=B, col0=0,
                           ncol=W_IN // COL)
            ck, cv = OFF_K // HEAD_DIM, OFF_V // HEAD_DIM
        z, *cast = _in_proj(x2, norm_g, mod, w_in, l, rows_per_mod=S // IN_PROJ_ROWS, mod_row0=0,
                            col0=0, ncol=W_IN // COL, cast=(w_pa, w_pf, w_pn, w_out) if l == 0 else ())
        if l == 0:
            wpa, wpf, wpn, wout = cast

        n = _nbr_attention(z, zc, ck, cv, q_norm_g, k_norm_g, rpb_flat, l, S)
        f = _fourier(z, S, dft_lat)
        x_new = _merge(f, n, z, x2, mod, gmlp_ln_g, gmlp_ln_b, ws, bst, wpa, wpf, wpn, wout, l,
                       rows_per_mod=S // MERGE_ROWS, mod_row0=0)

        if not last:
            nc = _ctx_attention(zc, q_norm_g, k_norm_g, l)
            fc = _fourier(zc, C, dft_ctx)
            c2 = _merge(fc, nc, zc, c2, mod, gmlp_ln_g, gmlp_ln_b, ws, bst, wpa, wpf, wpn, wout, l,
                        rows_per_mod=ONE_ROW, mod_row0=B)
        x2 = x_new
    return x2.reshape(B, S, D)
```

```python
import functools

import jax
import jax.numpy as jnp
import numpy as np
from jax import lax
from jax.experimental import pallas as pl
from jax.experimental.pallas import tpu as pltpu

F32 = jnp.float32
BF16 = jnp.bfloat16

D_MODEL = 2048
CTX_LEN = 256
GRID_W = 64
EPS = 1e-6
CHUNK = 128
GMLP_GROUPS = 8
W_A = D_MODEL // 2
W_B = D_MODEL // 2
F_GROUPS = 4
F_GW = W_B // F_GROUPS
HEAD_DIM = 128
NA_HEADS = 8
W_C = NA_HEADS * HEAD_DIM
NA_KH = 8
NA_KW = 16

OFF_U = 0
OFF_VG = OFF_U + W_A
OFF_GA = OFF_VG + W_A
OFF_F = OFF_GA + W_A
OFF_GF = OFF_F + W_B
OFF_Q = OFF_GF + W_B
OFF_K = OFF_Q + W_C
OFF_V = OFF_K + W_C
OFF_GN = OFF_V + W_C
OFF_MERGE = OFF_GN + W_C
W_IN = OFF_MERGE + 3 * D_MODEL

LANES = 128
BF16_SUBLANES = 16
IN_PROJ_ROWS = 1024
MERGE_ROWS = 256
COL = 1024
MOD_ROWS = 16
ONE_ROW = 1 << 30
VMEM_LIMIT = 56 * 1024 * 1024
Z_DTYPE = jnp.bfloat16

Q_ROWS = 2
WIN_ROWS = 10
SCORE_LOOKAHEAD = 3
HEADS_PER_STEP = 2
NEG = -1e30
LOG2E = float(np.log2(np.e))
ATT_SCALE = HEAD_DIM ** -0.5 * LOG2E


GELU_C1 = float(np.sqrt(2.0 / np.pi))
GELU_C2 = GELU_C1 * 0.044715


def _sigmoid(t):
    return 0.5 * jnp.tanh(0.5 * t) + 0.5


def _silu(t):
    h = 0.5 * t
    return h * jnp.tanh(h) + h


def _gelu(t):
    h = 0.5 * t
    return h * jnp.tanh(t * (GELU_C1 + GELU_C2 * (t * t))) + h


def _dot(a, b):
    return jnp.dot(a, b, preferred_element_type=F32)


def _dot_nt(a, b):
    return lax.dot_general(a, b, (((1,), (1,)), ((), ())), preferred_element_type=F32)


def _params(*sem):
    return pltpu.CompilerParams(dimension_semantics=sem, vmem_limit_bytes=VMEM_LIMIT)


def _mod_kernel(cc_ref, w_ref, b_ref, o_ref):
    s = _silu(cc_ref[...])
    s_hi = s.astype(BF16)
    s_lo = (s - s_hi.astype(F32)).astype(BF16)
    w = w_ref[0]
    w_hi = w.astype(BF16)
    w_lo = (w - w_hi.astype(F32)).astype(BF16)
    o_ref[0] = _dot(s_hi, w_hi) + _dot(s_lo, w_hi) + _dot(s_hi, w_lo) + b_ref[0]


def _modulation(cc, w_ada, b_ada):
    L, D, N = w_ada.shape
    tn = COL
    return pl.pallas_call(
        _mod_kernel,
        grid=(L, N // tn),
        in_specs=[
            pl.BlockSpec((MOD_ROWS, D), lambda l, j: (0, 0)),
            pl.BlockSpec((1, D, tn), lambda l, j: (l, 0, j)),
            pl.BlockSpec((1, 1, tn), lambda l, j: (l, 0, j)),
        ],
        out_specs=pl.BlockSpec((1, MOD_ROWS, tn), lambda l, j: (l, 0, j)),
        out_shape=jax.ShapeDtypeStruct((L, MOD_ROWS, N), F32),
        compiler_params=_params("arbitrary", "arbitrary"),
        name="adaln_mod",
    )(cc, w_ada, b_ada.reshape(L, 1, N))


def _in_proj_kernel(x_ref, g_ref, sh_ref, sc_ref, w_ref, *rest, col0, n_cast):
    cast_in, (o_ref, *cast_out, h_scr) = rest[:n_cast], rest[n_cast:]
    j = pl.program_id(1)

    for src, dst in zip(cast_in, cast_out):
        dst[...] = src[...].astype(dst.dtype)

    @pl.when(j == 0)
    def _():
        x = x_ref[...]
        gain = g_ref[0] * (1.0 + sc_ref[0])
        r = lax.rsqrt(jnp.mean(x * x, axis=-1, keepdims=True) + EPS)
        h_scr[...] = (x * r * gain + sh_ref[0]).astype(BF16)

    c = j + col0

    def project(act):
        o_ref[...] = act(_dot(h_scr[...], w_ref[0].astype(BF16))).astype(o_ref.dtype)

    @pl.when(c < OFF_GA // COL)
    def _():
        project(_gelu)

    @pl.when((c == OFF_GA // COL) | (c == OFF_GF // COL) | (c == OFF_GN // COL))
    def _():
        project(_silu)

    @pl.when(c >= OFF_MERGE // COL)
    def _():
        project(_sigmoid)

    @pl.when((c == OFF_F // COL) | ((c >= OFF_Q // COL) & (c < OFF_GN // COL)))
    def _():
        project(lambda z: z)


def _in_proj(x2, norm_g, mod, w_in, layer, *, rows_per_mod, mod_row0, col0, ncol, cast=()):
    T, D = x2.shape
    tm = min(IN_PROJ_ROWS, T)
    nsteps = (T // tm) * ncol
    mrow = lambda i: layer * MOD_ROWS + mod_row0 + i // rows_per_mod

    cast2d = [t.reshape(-1, t.shape[-1]) for t in cast]
    cast_specs = []
    for t in cast2d:
        rows = BF16_SUBLANES * pl.cdiv(t.shape[0], BF16_SUBLANES * nsteps)
        while t.shape[0] % rows:
            rows += BF16_SUBLANES
        nblk = t.shape[0] // rows
        cast_specs.append(pl.BlockSpec(
            (rows, t.shape[1]), lambda i, j, nblk=nblk: (jnp.minimum(i * ncol + j, nblk - 1), 0)))

    z, *casted = pl.pallas_call(
        functools.partial(_in_proj_kernel, col0=col0, n_cast=len(cast)),
        grid=(T // tm, ncol),
        in_specs=[
            pl.BlockSpec((tm, D), lambda i, j: (i, 0)),
            pl.BlockSpec((1, 1, D), lambda i, j: (layer, 0, 0)),
            pl.BlockSpec((1, 1, D), lambda i, j: (mrow(i), 0, 0)),
            pl.BlockSpec((1, 1, D), lambda i, j: (mrow(i), 0, 1)),
            pl.BlockSpec((1, D, COL), lambda i, j: (layer, 0, j + col0)),
        ] + cast_specs,
        out_specs=[pl.BlockSpec((tm, COL), lambda i, j: (i, j))] + cast_specs,
        out_shape=[jax.ShapeDtypeStruct((T, ncol * COL), Z_DTYPE)]
        + [jax.ShapeDtypeStruct(t.shape, BF16) for t in cast2d],
        scratch_shapes=[pltpu.VMEM((tm, D), BF16)],
        compiler_params=_params("arbitrary", "arbitrary"),
        name="in_proj",
    )(x2, norm_g, mod, mod, w_in, *cast2d)
    return (z, *(t.reshape(s.shape) for t, s in zip(casted, cast)))


def _spatial_gating(u_ref, vg_ref, ga_ref, lng_ref, lnb_ref, ws_ref, bst_ref, o_ref, vn_scr):
    v = vg_ref[...].astype(F32)
    mu = jnp.mean(v, axis=-1, keepdims=True)
    vc = v - mu
    vn = vc * lax.rsqrt(jnp.mean(vc * vc, axis=-1, keepdims=True) + EPS)
    vn_scr[...] = (vn * lng_ref[0] + lnb_ref[0]).astype(BF16)
    nchunk = u_ref.shape[0] // CHUNK
    for g in range(GMLP_GROUPS):
        cols = slice(g * CHUNK, (g + 1) * CHUNK)
        v_all = jnp.concatenate([vn_scr[ch * CHUNK:(ch + 1) * CHUNK, cols] for ch in range(nchunk)], axis=1)
        sv_all = _dot(ws_ref[0, g], v_all) + bst_ref[0, :, g:g + 1]
        for ch in range(nchunk):
            rows = slice(ch * CHUNK, (ch + 1) * CHUNK)
            sv = sv_all[:, ch * CHUNK:(ch + 1) * CHUNK]
            a = u_ref[rows, cols].astype(F32) * sv * ga_ref[rows, cols].astype(F32)
            o_ref[rows, cols] = a.astype(o_ref.dtype)


DFT_RADIX = 4


def _dft_consts(n):
    m = n // DFT_RADIX
    k = np.arange(m, dtype=np.int64)
    ang = 2.0 * np.pi * ((k[:, None] * k[None, :]) % m).astype(np.float64) / m
    cs = np.concatenate([np.cos(ang), np.sin(ang)], axis=0)
    th = 2.0 * np.pi * np.arange(n // 2, dtype=np.float64) / n
    twc = np.broadcast_to(np.cos(th)[:, None], (n // 2, LANES))
    tws = np.broadcast_to(np.sin(th)[:, None], (n // 2, LANES))
    c = np.arange(F_GW, dtype=np.int64)
    angc = 2.0 * np.pi * ((c[:, None] * c[None, :]) % F_GW).astype(np.float64) / F_GW
    bf = lambda t: jnp.asarray(t, dtype=F32).astype(BF16)
    return bf(cs), jnp.asarray(twc, dtype=F32), jnp.asarray(tws, dtype=F32), bf(np.cos(angc)), bf(np.sin(angc))


def _fourier_kernel(f_ref, gf_ref, cs_ref, twc_ref, tws_ref, cc_ref, sc_ref, o_ref, x_scr):
    n = f_ref.shape[0]
    m0 = n // DFT_RADIX
    lane_tiles = F_GW // LANES

    def position_dft(g):
        for t in range(lane_tiles):
            x_scr[g, t] = f_ref[:, g * F_GW + t * LANES:g * F_GW + (t + 1) * LANES].astype(F32)
        xcat = jnp.concatenate(
            [x_scr[g, t, pl.ds(r, m0, stride=DFT_RADIX), :].astype(BF16)
             for r in range(DFT_RADIX) for t in range(lane_tiles)], axis=1)
        return _dot(cs_ref[...], xcat)

    def butterflies(e):
        base = [(e[:m0, r * F_GW:(r + 1) * F_GW], e[m0:, r * F_GW:(r + 1) * F_GW])
                for r in range(DFT_RADIX)]

        def sub_dft(stride, offset):
            if stride == DFT_RADIX:
                return base[offset]
            ec0, es0 = sub_dft(2 * stride, offset)
            ec1, es1 = sub_dft(2 * stride, offset + stride)
            m = n // (2 * stride)
            c = jnp.concatenate([twc_ref[pl.ds(0, m, stride=stride), :]] * lane_tiles, axis=1)
            s = jnp.concatenate([tws_ref[pl.ds(0, m, stride=stride), :]] * lane_tiles, axis=1)
            a = c * ec1 - s * es1
            b = c * es1 + s * ec1
            return (jnp.concatenate([ec0 + a, ec0 - a], axis=0),
                    jnp.concatenate([es0 + b, es0 - b], axis=0))

        return sub_dft(1, 0)

    nxt = position_dft(0)
    for g in range(F_GROUPS):
        e = nxt
        if g + 1 < F_GROUPS:
            nxt = position_dft(g + 1)
        tc, ts = butterflies(e)
        cols = slice(g * F_GW, (g + 1) * F_GW)
        y = _dot(tc.astype(BF16), cc_ref[...]) - _dot(ts.astype(BF16), sc_ref[...])
        o_ref[:, cols] = (y * ((n * F_GW) ** -0.5) * gf_ref[:, cols].astype(F32)).astype(o_ref.dtype)


def _fourier(z, n, consts):
    T = z.shape[0]
    const = lambda t: pl.BlockSpec(t.shape, lambda b: (0, 0), pipeline_mode=pl.Buffered(1))
    return pl.pallas_call(
        _fourier_kernel,
        grid=(T // n,),
        in_specs=[
            pl.BlockSpec((n, W_B), lambda b: (b, OFF_F // W_B)),
            pl.BlockSpec((n, W_B), lambda b: (b, OFF_GF // W_B)),
        ] + [const(t) for t in consts],
        out_specs=pl.BlockSpec((n, W_B), lambda b: (b, 0)),
        out_shape=jax.ShapeDtypeStruct((T, W_B), BF16),
        scratch_shapes=[pltpu.VMEM((F_GROUPS, F_GW // LANES, n, LANES), F32)],
        compiler_params=_params("arbitrary"),
        name="fourier",
    )(z, z, *consts)


def _head_rms(t, g):
    t = t.astype(F32)
    return t * lax.rsqrt(jnp.mean(t * t, axis=-1, keepdims=True) + EPS) * g


def _block_geometry(rows):
    assert rows % Q_ROWS == 0 and WIN_ROWS % 2 == 0 and Q_ROWS + NA_KH - 1 <= WIN_ROWS <= rows
    geo, types = [], {}
    for blk in range(rows // Q_ROWS):
        base = min(max(Q_ROWS * blk - NA_KH // 2, 0), rows - WIN_ROWS)
        starts = tuple(_window_start(Q_ROWS * blk + i, rows) - base for i in range(Q_ROWS))
        signature = (base - Q_ROWS * blk, starts)
        geo.append((base, types.setdefault(signature, len(types))))
    return geo


def _window_start(r, rows):
    return min(max(r - NA_KH // 2, 0), rows - NA_KH)


def _build_bias_tables(rpb_ref, tab_scr, head, geo, rows):
    n_dr, n_dc = 2 * NA_KH - 1, 2 * NA_KW - 1
    shape = (GRID_W, 2 * GRID_W)
    lane = lax.broadcasted_iota(jnp.int32, shape, 1)
    cq = lax.broadcasted_iota(jnp.int32, shape, 0)
    kc = lane & (GRID_W - 1)
    dc = kc - cq + (NA_KW - 1)
    cs = jnp.clip(cq - NA_KW // 2, 0, GRID_W - NA_KW)
    col_ok = (kc >= cs) & (kc < cs + NA_KW)
    neg = jnp.full(shape, NEG, F32)
    row_bias = []
    for d in range(n_dr):
        t = neg
        for b in range(n_dc):
            t = jnp.where(dc == b, rpb_ref[(head * n_dr + d) * n_dc + b] * LOG2E, t)
        row_bias.append(jnp.where(col_ok, t, NEG))
    first_block = {typ: blk for blk, (_, typ) in reversed(list(enumerate(geo)))}
    for typ, blk in first_block.items():
        base = geo[blk][0]
        for i in range(Q_ROWS):
            r = Q_ROWS * blk + i
            rs = _window_start(r, rows)
            for wp in range(WIN_ROWS // 2):
                halves = []
                for kr in (base + 2 * wp, base + 2 * wp + 1):
                    halves.append(row_bias[kr - r + NA_KH - 1] if rs <= kr < rs + NA_KH else neg)
                tab_scr[typ, i * GRID_W:(i + 1) * GRID_W, wp * 2 * GRID_W:(wp + 1) * 2 * GRID_W] = (
                    jnp.where(lane < GRID_W, halves[0], halves[1]))


def _nbr_attn_kernel(rpb_ref, q_ref, k_ref, v_ref, gn_ref, kc_ref, vc_ref, qg_ref, kg_ref, o_ref,
                     qn_scr, kn_scr, kcn_scr, vx_scr, vcx_scr, tab_ref, *, geo, rows, layer):
    hd = HEAD_DIM

    @pl.when(pl.program_id(1) == 0)
    def _():
        for hh in range(HEADS_PER_STEP):
            head = layer * NA_HEADS + pl.program_id(0) * HEADS_PER_STEP + hh
            _build_bias_tables(rpb_ref, tab_ref.at[hh], head, geo, rows)
            vx_scr[hh, :, hd:] = jnp.ones((vx_scr.shape[1], hd), BF16)
            vcx_scr[hh, :, hd:] = jnp.ones((vcx_scr.shape[1], hd), BF16)

    nq = Q_ROWS * GRID_W
    nk = WIN_ROWS * GRID_W

    def prepare(hh):
        cols = slice(hh * hd, (hh + 1) * hd)
        qn_scr[hh] = _head_rms(q_ref[:, cols], qg_ref[0] * ATT_SCALE).astype(BF16)
        kn_scr[hh] = _head_rms(k_ref[:, cols], kg_ref[0]).astype(BF16)
        kcn_scr[hh] = _head_rms(kc_ref[:, cols], kg_ref[0]).astype(BF16)
        vx_scr[hh, :, :hd] = v_ref[:, cols].astype(BF16)
        vcx_scr[hh, :, :hd] = vc_ref[:, cols].astype(BF16)

    def scores(hh, blk):
        base, typ = geo[blk]
        q = qn_scr[hh, blk * nq:(blk + 1) * nq, :]
        s_loc = _dot_nt(q, kn_scr[hh, base * GRID_W:base * GRID_W + nk, :]) + tab_ref[hh, typ]
        return s_loc, _dot_nt(q, kcn_scr[hh])

    work = [(hh, blk) for hh in range(HEADS_PER_STEP) for blk in range(len(geo))]
    prepare(0)
    pending = [scores(*w) for w in work[:SCORE_LOOKAHEAD]]
    for n, (hh, blk) in enumerate(work):
        if blk == 0 and hh + 1 < HEADS_PER_STEP:
            prepare(hh + 1)
        base, typ = geo[blk]
        qs = slice(blk * nq, (blk + 1) * nq)
        ks = slice(base * GRID_W, base * GRID_W + nk)
        cols = slice(hh * hd, (hh + 1) * hd)
        s_loc, s_ctx = pending.pop(0)
        if n + SCORE_LOOKAHEAD < len(work):
            pending.append(scores(*work[n + SCORE_LOOKAHEAD]))
        m = jnp.maximum(jnp.max(s_loc, axis=-1, keepdims=True), jnp.max(s_ctx, axis=-1, keepdims=True))
        p_loc = jnp.exp2(s_loc - m)
        p_ctx = jnp.exp2(s_ctx - m)
        ox = _dot(p_loc.astype(BF16), vx_scr[hh, ks, :]) + _dot(p_ctx.astype(BF16), vcx_scr[hh])
        o_ref[qs, cols] = (ox[:, :hd] / ox[:, hd:] * gn_ref[qs, cols].astype(F32)).astype(o_ref.dtype)


def _nbr_attention(z, zc, ck, cv, qg, kg, rpb_flat, layer, seq):
    T = z.shape[0]
    rows = seq // GRID_W
    geo = _block_geometry(rows)
    hd = HEAD_DIM
    hps = HEADS_PER_STEP
    wide = hps * hd
    assert NA_HEADS % hps == 0 and ck % hps == 0 and cv % hps == 0
    ntypes = 1 + max(t for _, t in geo)
    lat = lambda off: pl.BlockSpec((seq, wide), lambda h, b: (b, off // wide + h))
    return pl.pallas_call(
        functools.partial(_nbr_attn_kernel, geo=geo, rows=rows, layer=layer),
        grid=(NA_HEADS // hps, T // seq),
        in_specs=[
            pl.BlockSpec(memory_space=pltpu.SMEM),
            lat(OFF_Q), lat(OFF_K), lat(OFF_V), lat(OFF_GN),
            pl.BlockSpec((CTX_LEN, wide), lambda h, b: (b, ck // hps + h)),
            pl.BlockSpec((CTX_LEN, wide), lambda h, b: (b, cv // hps + h)),
            pl.BlockSpec((1, 1, hd), lambda h, b: (layer, 0, 0)),
            pl.BlockSpec((1, 1, hd), lambda h, b: (layer, 0, 0)),
        ],
        out_specs=pl.BlockSpec((seq, wide), lambda h, b: (b, h)),
        out_shape=jax.ShapeDtypeStruct((T, W_C), BF16),
        scratch_shapes=[pltpu.VMEM((hps, seq, hd), BF16), pltpu.VMEM((hps, seq, hd), BF16),
                        pltpu.VMEM((hps, CTX_LEN, hd), BF16),
                        pltpu.VMEM((hps, seq, 2 * hd), BF16), pltpu.VMEM((hps, CTX_LEN, 2 * hd), BF16),
                        pltpu.VMEM((hps, ntypes, Q_ROWS * GRID_W, WIN_ROWS * GRID_W), F32)],
        compiler_params=_params("arbitrary", "arbitrary"),
        name="nbr_attn",
    )(rpb_flat, z, z, z, z, zc, zc, qg, kg)


def _ctx_attn_kernel(q_ref, k_ref, v_ref, gn_ref, qg_ref, kg_ref, o_ref):
    for h in range(NA_HEADS):
        cols = slice(h * HEAD_DIM, (h + 1) * HEAD_DIM)
        q = _head_rms(q_ref[:, cols], qg_ref[0] * ATT_SCALE).astype(BF16)
        k = _head_rms(k_ref[:, cols], kg_ref[0]).astype(BF16)
        s = _dot_nt(q, k)
        p = jnp.exp2(s - jnp.max(s, axis=-1, keepdims=True))
        o = _dot(p.astype(BF16), v_ref[:, cols].astype(BF16)) / jnp.sum(p, axis=-1, keepdims=True)
        o_ref[:, cols] = (o * gn_ref[:, cols].astype(F32)).astype(o_ref.dtype)


def _ctx_attention(zc, qg, kg, layer):
    T = zc.shape[0]
    blk = lambda off: pl.BlockSpec((CTX_LEN, W_C), lambda b: (b, off // W_C))
    vec = pl.BlockSpec((1, 1, HEAD_DIM), lambda b: (layer, 0, 0))
    return pl.pallas_call(
        _ctx_attn_kernel,
        grid=(T // CTX_LEN,),
        in_specs=[blk(OFF_Q), blk(OFF_K), blk(OFF_V), blk(OFF_GN), vec, vec],
        out_specs=pl.BlockSpec((CTX_LEN, W_C), lambda b: (b, 0)),
        out_shape=jax.ShapeDtypeStruct((T, W_C), BF16),
        compiler_params=_params("arbitrary"),
        name="ctx_attn",
    )(zc, zc, zc, zc, qg, kg)


def _merge_kernel(u_ref, vg_ref, ga_ref, f_ref, n_ref, ga0, ga1, gf0, gf1, gn0, gn1, x_ref, gate_ref,
                  lng_ref, lnb_ref, ws_ref, bst_ref, wpa_ref, wpf_ref, wpn_ref, wout_ref, o_ref,
                  a_scr, vn_scr, yfn_scr, y_scr):
    f = f_ref[...]
    n = n_ref[...]
    for half, (gf, gn) in enumerate(((gf0, gn0), (gf1, gn1))):
        cols = slice(half * COL, (half + 1) * COL)
        yfn_scr[:, cols] = (gf[...].astype(F32) * _dot(f, wpf_ref[0, :, cols])
                            + gn[...].astype(F32) * _dot(n, wpn_ref[0, :, cols]))
    _spatial_gating(u_ref, vg_ref, ga_ref, lng_ref, lnb_ref, ws_ref, bst_ref, a_scr, vn_scr)
    a = a_scr[...]
    for half, ga in enumerate((ga0, ga1)):
        cols = slice(half * COL, (half + 1) * COL)
        y = yfn_scr[:, cols] + ga[...].astype(F32) * _dot(a, wpa_ref[0, :, cols])
        y_scr[:, cols] = y.astype(BF16)
    o_ref[...] = x_ref[...] + gate_ref[0] * _dot(y_scr[...], wout_ref[0])


def _merge(f, n, z, x2, mod, lng, lnb, ws, bst, wpa, wpf, wpn, wout, layer, *, rows_per_mod, mod_row0):
    T, D = x2.shape
    tm = MERGE_ROWS
    mrow = lambda i: layer * MOD_ROWS + mod_row0 + i // rows_per_mod
    act = pl.BlockSpec((tm, COL), lambda i: (i, 0))
    zblk = lambda c: pl.BlockSpec((tm, COL), lambda i: (i, c))
    const = lambda *shape: pl.BlockSpec((1,) + shape, lambda i: (layer,) + (0,) * len(shape),
                                        pipeline_mode=pl.Buffered(1))
    return pl.pallas_call(
        _merge_kernel,
        grid=(T // tm,),
        in_specs=[
            zblk(OFF_U // COL), zblk(OFF_VG // COL), zblk(OFF_GA // COL),
            act, act,
        ] + [zblk(OFF_MERGE // COL + c) for c in range(3 * D // COL)] + [
            pl.BlockSpec((tm, D), lambda i: (i, 0)),
            pl.BlockSpec((1, 1, D), lambda i: (mrow(i), 0, 2)),
            const(1, W_A), const(1, W_A), const(GMLP_GROUPS, CHUNK, CHUNK), const(CHUNK, GMLP_GROUPS),
            const(W_A, D), const(W_B, D), const(W_C, D), const(D, D),
        ],
        out_specs=pl.BlockSpec((tm, D), lambda i: (i, 0)),
        out_shape=jax.ShapeDtypeStruct((T, D), F32),
        scratch_shapes=[pltpu.VMEM((tm, W_A), BF16), pltpu.VMEM((tm, W_A), BF16),
                        pltpu.VMEM((tm, D), F32), pltpu.VMEM((tm, D), BF16)],
        compiler_params=_params("arbitrary"),
        name="merge_out",
    )(z, z, z, f, n, z, z, z, z, z, z, x2, mod, lng, lnb, ws, bst, wpa, wpf, wpn, wout)


def kernel(x, c, ctx, c_ctx, norm_g, w_ada, b_ada, w_in, gmlp_ln_g, gmlp_ln_b, gmlp_ws, gmlp_bs,
           q_norm_g, k_norm_g, rpb, w_pa, w_pf, w_pn, w_out):
    B, S, D = x.shape
    C = ctx.shape[1]
    L = w_in.shape[0]
    assert D == D_MODEL and C == CTX_LEN and B < MOD_ROWS
    assert S % (GRID_W * Q_ROWS) == 0 and S // GRID_W >= WIN_ROWS

    cc = jnp.concatenate([c, c_ctx[None], jnp.zeros((MOD_ROWS - B - 1, D), F32)], axis=0)
    mod = _modulation(cc, w_ada, b_ada).reshape(L * MOD_ROWS, 1, 3 * D)

    ws = gmlp_ws.astype(BF16)
    bst = jnp.swapaxes(gmlp_bs, 1, 2)
    rpb_flat = rpb.reshape(-1)
    norm_g, gmlp_ln_g, gmlp_ln_b, q_norm_g, k_norm_g = (
        t[:, None, :] for t in (norm_g, gmlp_ln_g, gmlp_ln_b, q_norm_g, k_norm_g))
    dft_lat = _dft_consts(S)
    dft_ctx = _dft_consts(C)
    x2 = x.reshape(B * S, D)
    c2 = ctx.reshape(B * C, D)
    for l in range(L):
        last = l == L - 1
        if last:
            zc, = _in_proj(c2, norm_g, mod, w_in, l, rows_per_mod=ONE_ROW, mod_row0=B,
                           col0=OFF_K // COL, ncol=2)
            ck, cv = 0, W_C // HEAD_DIM
        else:
            zc, = _in_proj(c2, norm_g, mod, w_in, l, rows_per_mod=ONE_ROW, mod_row0=B, col0=0,
                           ncol=W_IN // COL)
            ck, cv = OFF_K // HEAD_DIM, OFF_V // HEAD_DIM
        z, *cast = _in_proj(x2, norm_g, mod, w_in, l, rows_per_mod=S // IN_PROJ_ROWS, mod_row0=0,
                            col0=0, ncol=W_IN // COL, cast=(w_pa, w_pf, w_pn, w_out) if l == 0 else ())
        if l == 0:
            wpa, wpf, wpn, wout = cast

        n = _nbr_attention(z, zc, ck, cv, q_norm_g, k_norm_g, rpb_flat, l, S)
        f = _fourier(z, S, dft_lat)
        x_new = _merge(f, n, z, x2, mod, gmlp_ln_g, gmlp_ln_b, ws, bst, wpa, wpf, wpn, wout, l,
                       rows_per_mod=S // MERGE_ROWS, mod_row0=0)

        if not last:
            nc = _ctx_attention(zc, q_norm_g, k_norm_g, l)
            fc = _fourier(zc, C, dft_ctx)
            c2 = _merge(fc, nc, zc, c2, mod, gmlp_ln_g, gmlp_ln_b, ws, bst, wpa, wpf, wpn, wout, l,
                        rows_per_mod=ONE_ROW, mod_row0=B)
        x2 = x_new
    return x2.reshape(B, S, D)
```

```python
import functools

import jax
import jax.numpy as jnp
import numpy as np
from jax import lax
from jax.experimental import pallas as pl
from jax.experimental.pallas import tpu as pltpu

F32 = jnp.float32
BF16 = jnp.bfloat16

D_MODEL = 2048
CTX_LEN = 256
GRID_W = 64
EPS = 1e-6
CHUNK = 128
GMLP_GROUPS = 8
W_A = D_MODEL // 2
W_B = D_MODEL // 2
F_GROUPS = 4
F_GW = W_B // F_GROUPS
HEAD_DIM = 128
NA_HEADS = 8
W_C = NA_HEADS * HEAD_DIM
NA_KH = 8
NA_KW = 16

OFF_U = 0
OFF_VG = OFF_U + W_A
OFF_GA = OFF_VG + W_A
OFF_F = OFF_GA + W_A
OFF_GF = OFF_F + W_B
OFF_Q = OFF_GF + W_B
OFF_K = OFF_Q + W_C
OFF_V = OFF_K + W_C
OFF_GN = OFF_V + W_C
OFF_MERGE = OFF_GN + W_C
W_IN = OFF_MERGE + 3 * D_MODEL

LANES = 128
BF16_SUBLANES = 16
IN_PROJ_ROWS = 1024
MERGE_ROWS = 256
COL = 1024
MOD_ROWS = 16
ONE_ROW = 1 << 30
VMEM_LIMIT = 56 * 1024 * 1024
Z_DTYPE = jnp.bfloat16

Q_ROWS = 2
WIN_ROWS = 10
SCORE_LOOKAHEAD = 3
HEADS_PER_STEP = 2
NEG = -1e30
LOG2E = float(np.log2(np.e))
ATT_SCALE = HEAD_DIM ** -0.5 * LOG2E


GELU_C1 = float(np.sqrt(2.0 / np.pi))
GELU_C2 = GELU_C1 * 0.044715


def _sigmoid(t):
    return 0.5 * jnp.tanh(0.5 * t) + 0.5


def _silu(t):
    h = 0.5 * t
    return h * jnp.tanh(h) + h


def _gelu(t):
    h = 0.5 * t
    return h * jnp.tanh(t * (GELU_C1 + GELU_C2 * (t * t))) + h


def _dot(a, b):
    return jnp.dot(a, b, preferred_element_type=F32)


def _dot_nt(a, b):
    return lax.dot_general(a, b, (((1,), (1,)), ((), ())), preferred_element_type=F32)


def _params(*sem):
    return pltpu.CompilerParams(dimension_semantics=sem, vmem_limit_bytes=VMEM_LIMIT)


def _mod_kernel(cc_ref, w_ref, b_ref, o_ref):
    s = _silu(cc_ref[...])
    s_hi = s.astype(BF16)
    s_lo = (s - s_hi.astype(F32)).astype(BF16)
    w = w_ref[0]
    w_hi = w.astype(BF16)
    w_lo = (w - w_hi.astype(F32)).astype(BF16)
    o_ref[0] = _dot(s_hi, w_hi) + _dot(s_lo, w_hi) + _dot(s_hi, w_lo) + b_ref[0]


def _modulation(cc, w_ada, b_ada):
    L, D, N = w_ada.shape
    tn = COL
    return pl.pallas_call(
        _mod_kernel,
        grid=(L, N // tn),
        in_specs=[
            pl.BlockSpec((MOD_ROWS, D), lambda l, j: (0, 0)),
            pl.BlockSpec((1, D, tn), lambda l, j: (l, 0, j)),
            pl.BlockSpec((1, 1, tn), lambda l, j: (l, 0, j)),
        ],
        out_specs=pl.BlockSpec((1, MOD_ROWS, tn), lambda l, j: (l, 0, j)),
        out_shape=jax.ShapeDtypeStruct((L, MOD_ROWS, N), F32),
        compiler_params=_params("arbitrary", "arbitrary"),
        name="adaln_mod",
    )(cc, w_ada, b_ada.reshape(L, 1, N))


def _in_proj_kernel(x_ref, g_ref, sh_ref, sc_ref, w_ref, *rest, col0, n_cast):
    cast_in, (o_ref, *cast_out, h_scr) = rest[:n_cast], rest[n_cast:]
    j = pl.program_id(1)

    for src, dst in zip(cast_in, cast_out):
        dst[...] = src[...].astype(dst.dtype)

    @pl.when(j == 0)
    def _():
        x = x_ref[...]
        gain = g_ref[0] * (1.0 + sc_ref[0])
        r = lax.rsqrt(jnp.mean(x * x, axis=-1, keepdims=True) + EPS)
        h_scr[...] = (x * r * gain + sh_ref[0]).astype(BF16)

    c = j + col0

    def project(act):
        o_ref[...] = act(_dot(h_scr[...], w_ref[0].astype(BF16))).astype(o_ref.dtype)

    @pl.when(c < OFF_GA // COL)
    def _():
        project(_gelu)

    @pl.when((c == OFF_GA // COL) | (c == OFF_GF // COL) | (c == OFF_GN // COL))
    def _():
        project(_silu)

    @pl.when(c >= OFF_MERGE // COL)
    def _():
        project(_sigmoid)

    @pl.when((c == OFF_F // COL) | ((c >= OFF_Q // COL) & (c < OFF_GN // COL)))
    def _():
        project(lambda z: z)


def _in_proj(x2, norm_g, mod, w_in, layer, *, rows_per_mod, mod_row0, col0, ncol, cast=()):
    T, D = x2.shape
    tm = min(IN_PROJ_ROWS, T)
    nsteps = (T // tm) * ncol
    mrow = lambda i: layer * MOD_ROWS + mod_row0 + i // rows_per_mod

    cast2d = [t.reshape(-1, t.shape[-1]) for t in cast]
    cast_specs = []
    for t in cast2d:
        rows = BF16_SUBLANES * pl.cdiv(t.shape[0], BF16_SUBLANES * nsteps)
        while t.shape[0] % rows:
            rows += BF16_SUBLANES
        nblk = t.shape[0] // rows
        cast_specs.append(pl.BlockSpec(
            (rows, t.shape[1]), lambda i, j, nblk=nblk: (jnp.minimum(i * ncol + j, nblk - 1), 0)))

    z, *casted = pl.pallas_call(
        functools.partial(_in_proj_kernel, col0=col0, n_cast=len(cast)),
        grid=(T // tm, ncol),
        in_specs=[
            pl.BlockSpec((tm, D), lambda i, j: (i, 0)),
            pl.BlockSpec((1, 1, D), lambda i, j: (layer, 0, 0)),
            pl.BlockSpec((1, 1, D), lambda i, j: (mrow(i), 0, 0)),
            pl.BlockSpec((1, 1, D), lambda i, j: (mrow(i), 0, 1)),
            pl.BlockSpec((1, D, COL), lambda i, j: (layer, 0, j + col0)),
        ] + cast_specs,
        out_specs=[pl.BlockSpec((tm, COL), lambda i, j: (i, j))] + cast_specs,
        out_shape=[jax.ShapeDtypeStruct((T, ncol * COL), Z_DTYPE)]
        + [jax.ShapeDtypeStruct(t.shape, BF16) for t in cast2d],
        scratch_shapes=[pltpu.VMEM((tm, D), BF16)],
        compiler_params=_params("arbitrary", "arbitrary"),
        name="in_proj",
    )(x2, norm_g, mod, mod, w_in, *cast2d)
    return (z, *(t.reshape(s.shape) for t, s in zip(casted, cast)))


def _spatial_gating(u_ref, vg_ref, ga_ref, lng_ref, lnb_ref, ws_ref, bst_ref, o_ref, vn_scr):
    v = vg_ref[...].astype(F32)
    mu = jnp.mean(v, axis=-1, keepdims=True)
    vc = v - mu
    vn = vc * lax.rsqrt(jnp.mean(vc * vc, axis=-1, keepdims=True) + EPS)
    vn_scr[...] = (vn * lng_ref[0] + lnb_ref[0]).astype(BF16)
    nchunk = u_ref.shape[0] // CHUNK
    for g in range(GMLP_GROUPS):
        cols = slice(g * CHUNK, (g + 1) * CHUNK)
        v_all = jnp.concatenate([vn_scr[ch * CHUNK:(ch + 1) * CHUNK, cols] for ch in range(nchunk)], axis=1)
        sv_all = _dot(ws_ref[0, g], v_all) + bst_ref[0, :, g:g + 1]
        for ch in range(nchunk):
            rows = slice(ch * CHUNK, (ch + 1) * CHUNK)
            sv = sv_all[:, ch * CHUNK:(ch + 1) * CHUNK]
            a = u_ref[rows, cols].astype(F32) * sv * ga_ref[rows, cols].astype(F32)
            o_ref[rows, cols] = a.astype(o_ref.dtype)


DFT_RADIX = 4


def _dft_consts(n):
    m = n // DFT_RADIX
    k = np.arange(m, dtype=np.int64)
    ang = 2.0 * np.pi * ((k[:, None] * k[None, :]) % m).astype(np.float64) / m
    cs = np.concatenate([np.cos(ang), np.sin(ang)], axis=0)
    th = 2.0 * np.pi * np.arange(n // 2, dtype=np.float64) / n
    twc = np.broadcast_to(np.cos(th)[:, None], (n // 2, LANES))
    tws = np.broadcast_to(np.sin(th)[:, None], (n // 2, LANES))
    c = np.arange(F_GW, dtype=np.int64)
    angc = 2.0 * np.pi * ((c[:, None] * c[None, :]) % F_GW).astype(np.float64) / F_GW
    bf = lambda t: jnp.asarray(t, dtype=F32).astype(BF16)
    return bf(cs), jnp.asarray(twc, dtype=F32), jnp.asarray(tws, dtype=F32), bf(np.cos(angc)), bf(np.sin(angc))


def _dft_stages(f_ref, gf_ref, cs_ref, twc_ref, tws_ref, cc_ref, sc_ref, o_ref, x_scr):
    n = f_ref.shape[0]
    m0 = n // DFT_RADIX
    lane_tiles = F_GW // LANES

    def position_dft(g):
        for t in range(lane_tiles):
            x_scr[g, t] = f_ref[:, g * F_GW + t * LANES:g * F_GW + (t + 1) * LANES].astype(F32)
        xcat = jnp.concatenate(
            [x_scr[g, t, pl.ds(r, m0, stride=DFT_RADIX), :].astype(BF16)
             for r in range(DFT_RADIX) for t in range(lane_tiles)], axis=1)
        return _dot(cs_ref[...], xcat)

    def butterflies(e):
        base = [(e[:m0, r * F_GW:(r + 1) * F_GW], e[m0:, r * F_GW:(r + 1) * F_GW])
                for r in range(DFT_RADIX)]

        def sub_dft(stride, offset):
            if stride == DFT_RADIX:
                return base[offset]
            ec0, es0 = sub_dft(2 * stride, offset)
            ec1, es1 = sub_dft(2 * stride, offset + stride)
            m = n // (2 * stride)
            c = jnp.concatenate([twc_ref[pl.ds(0, m, stride=stride), :]] * lane_tiles, axis=1)
            s = jnp.concatenate([tws_ref[pl.ds(0, m, stride=stride), :]] * lane_tiles, axis=1)
            a = c * ec1 - s * es1
            b = c * es1 + s * ec1
            return (jnp.concatenate([ec0 + a, ec0 - a], axis=0),
                    jnp.concatenate([es0 + b, es0 - b], axis=0))

        return sub_dft(1, 0)

    def channel_dft(g, tc, ts):
        cols = slice(g * F_GW, (g + 1) * F_GW)
        y = _dot(tc.astype(BF16), cc_ref[...]) - _dot(ts.astype(BF16), sc_ref[...])
        o_ref[:, cols] = (y * ((n * F_GW) ** -0.5) * gf_ref[:, cols].astype(F32)).astype(o_ref.dtype)

    return position_dft, butterflies, channel_dft


def _fourier_kernel(*refs):
    position_dft, butterflies, channel_dft = _dft_stages(*refs)
    nxt = position_dft(0)
    for g in range(F_GROUPS):
        e = nxt
        if g + 1 < F_GROUPS:
            nxt = position_dft(g + 1)
        channel_dft(g, *butterflies(e))


def _fourier(z, n, consts):
    T = z.shape[0]
    const = lambda t: pl.BlockSpec(t.shape, lambda b: (0, 0), pipeline_mode=pl.Buffered(1))
    return pl.pallas_call(
        _fourier_kernel,
        grid=(T // n,),
        in_specs=[
            pl.BlockSpec((n, W_B), lambda b: (b, OFF_F // W_B)),
            pl.BlockSpec((n, W_B), lambda b: (b, OFF_GF // W_B)),
        ] + [const(t) for t in consts],
        out_specs=pl.BlockSpec((n, W_B), lambda b: (b, 0)),
        out_shape=jax.ShapeDtypeStruct((T, W_B), BF16),
        scratch_shapes=[pltpu.VMEM((F_GROUPS, F_GW // LANES, n, LANES), F32)],
        compiler_params=_params("arbitrary"),
        name="fourier",
    )(z, z, *consts)


def _head_rms(t, g):
    t = t.astype(F32)
    return t * lax.rsqrt(jnp.mean(t * t, axis=-1, keepdims=True) + EPS) * g


def _block_geometry(rows):
    assert rows % Q_ROWS == 0 and WIN_ROWS % 2 == 0 and Q_ROWS + NA_KH - 1 <= WIN_ROWS <= rows
    geo, types = [], {}
    for blk in range(rows // Q_ROWS):
        base = min(max(Q_ROWS * blk - NA_KH // 2, 0), rows - WIN_ROWS)
        starts = tuple(_window_start(Q_ROWS * blk + i, rows) - base for i in range(Q_ROWS))
        signature = (base - Q_ROWS * blk, starts)
        geo.append((base, types.setdefault(signature, len(types))))
    return geo


def _window_start(r, rows):
    return min(max(r - NA_KH // 2, 0), rows - NA_KH)


def _build_bias_tables(rpb_ref, tab_scr, head, geo, rows):
    n_dr, n_dc = 2 * NA_KH - 1, 2 * NA_KW - 1
    shape = (GRID_W, 2 * GRID_W)
    lane = lax.broadcasted_iota(jnp.int32, shape, 1)
    cq = lax.broadcasted_iota(jnp.int32, shape, 0)
    kc = lane & (GRID_W - 1)
    dc = kc - cq + (NA_KW - 1)
    cs = jnp.clip(cq - NA_KW // 2, 0, GRID_W - NA_KW)
    col_ok = (kc >= cs) & (kc < cs + NA_KW)
    neg = jnp.full(shape, NEG, F32)
    row_bias = []
    for d in range(n_dr):
        t = neg
        for b in range(n_dc):
            t = jnp.where(dc == b, rpb_ref[(head * n_dr + d) * n_dc + b] * LOG2E, t)
        row_bias.append(jnp.where(col_ok, t, NEG))
    first_block = {typ: blk for blk, (_, typ) in reversed(list(enumerate(geo)))}
    for typ, blk in first_block.items():
        base = geo[blk][0]
        for i in range(Q_ROWS):
            r = Q_ROWS * blk + i
            rs = _window_start(r, rows)
            for wp in range(WIN_ROWS // 2):
                halves = []
                for kr in (base + 2 * wp, base + 2 * wp + 1):
                    halves.append(row_bias[kr - r + NA_KH - 1] if rs <= kr < rs + NA_KH else neg)
                tab_scr[typ, i * GRID_W:(i + 1) * GRID_W, wp * 2 * GRID_W:(wp + 1) * 2 * GRID_W] = (
                    jnp.where(lane < GRID_W, halves[0], halves[1]))


def _nbr_attn_kernel(rpb_ref, q_ref, k_ref, v_ref, gn_ref, kc_ref, vc_ref, qg_ref, kg_ref,
                     f_ref, gf_ref, cs_ref, twc_ref, tws_ref, cc_ref, sc_ref, o_ref, fo_ref,
                     qn_scr, kn_scr, kcn_scr, vx_scr, vcx_scr, tab_ref, x_scr, *, geo, rows, layer):
    hd = HEAD_DIM
    position_dft, butterflies, channel_dft = _dft_stages(
        f_ref, gf_ref, cs_ref, twc_ref, tws_ref, cc_ref, sc_ref, fo_ref, x_scr)

    @pl.when(pl.program_id(1) == 0)
    def _():
        for hh in range(HEADS_PER_STEP):
            head = layer * NA_HEADS + pl.program_id(0) * HEADS_PER_STEP + hh
            _build_bias_tables(rpb_ref, tab_ref.at[hh], head, geo, rows)
            vx_scr[hh, :, hd:] = jnp.ones((vx_scr.shape[1], hd), BF16)
            vcx_scr[hh, :, hd:] = jnp.ones((vcx_scr.shape[1], hd), BF16)

    nq = Q_ROWS * GRID_W
    nk = WIN_ROWS * GRID_W

    def prepare(hh):
        cols = slice(hh * hd, (hh + 1) * hd)
        qn_scr[hh] = _head_rms(q_ref[:, cols], qg_ref[0] * ATT_SCALE).astype(BF16)
        kn_scr[hh] = _head_rms(k_ref[:, cols], kg_ref[0]).astype(BF16)
        kcn_scr[hh] = _head_rms(kc_ref[:, cols], kg_ref[0]).astype(BF16)
        vx_scr[hh, :, :hd] = v_ref[:, cols].astype(BF16)
        vcx_scr[hh, :, :hd] = vc_ref[:, cols].astype(BF16)

    def scores(hh, blk):
        base, typ = geo[blk]
        q = qn_scr[hh, blk * nq:(blk + 1) * nq, :]
        s_loc = _dot_nt(q, kn_scr[hh, base * GRID_W:base * GRID_W + nk, :]) + tab_ref[hh, typ]
        return s_loc, _dot_nt(q, kcn_scr[hh])

    work = [(hh, blk) for hh in range(HEADS_PER_STEP) for blk in range(len(geo))]
    dft = position_dft(0)
    prepare(0)
    pending = [scores(*w) for w in work[:SCORE_LOOKAHEAD]]
    dft = butterflies(dft)
    for n, (hh, blk) in enumerate(work):
        if blk == 0 and hh + 1 < HEADS_PER_STEP:
            prepare(hh + 1)
        if n == 1:
            channel_dft(0, *dft)
        base, typ = geo[blk]
        qs = slice(blk * nq, (blk + 1) * nq)
        ks = slice(base * GRID_W, base * GRID_W + nk)
        cols = slice(hh * hd, (hh + 1) * hd)
        s_loc, s_ctx = pending.pop(0)
        if n + SCORE_LOOKAHEAD < len(work):
            pending.append(scores(*work[n + SCORE_LOOKAHEAD]))
        m = jnp.maximum(jnp.max(s_loc, axis=-1, keepdims=True), jnp.max(s_ctx, axis=-1, keepdims=True))
        p_loc = jnp.exp2(s_loc - m)
        p_ctx = jnp.exp2(s_ctx - m)
        ox = _dot(p_loc.astype(BF16), vx_scr[hh, ks, :]) + _dot(p_ctx.astype(BF16), vcx_scr[hh])
        o_ref[qs, cols] = (ox[:, :hd] / ox[:, hd:] * gn_ref[qs, cols].astype(F32)).astype(o_ref.dtype)


def _nbr_attention(z, zc, ck, cv, qg, kg, rpb_flat, dft_consts, layer, seq):
    T = z.shape[0]
    rows = seq // GRID_W
    geo = _block_geometry(rows)
    hd = HEAD_DIM
    hps = HEADS_PER_STEP
    wide = hps * hd
    assert NA_HEADS % hps == 0 and ck % hps == 0 and cv % hps == 0
    assert NA_HEADS // hps == F_GROUPS
    const = lambda t: pl.BlockSpec(t.shape, lambda h, b: (0, 0), pipeline_mode=pl.Buffered(1))
    ntypes = 1 + max(t for _, t in geo)
    lat = lambda off: pl.BlockSpec((seq, wide), lambda h, b: (b, off // wide + h))
    return pl.pallas_call(
        functools.partial(_nbr_attn_kernel, geo=geo, rows=rows, layer=layer),
        grid=(NA_HEADS // hps, T // seq),
        in_specs=[
            pl.BlockSpec(memory_space=pltpu.SMEM),
            lat(OFF_Q), lat(OFF_K), lat(OFF_V), lat(OFF_GN),
            pl.BlockSpec((CTX_LEN, wide), lambda h, b: (b, ck // hps + h)),
            pl.BlockSpec((CTX_LEN, wide), lambda h, b: (b, cv // hps + h)),
            pl.BlockSpec((1, 1, hd), lambda h, b: (layer, 0, 0)),
            pl.BlockSpec((1, 1, hd), lambda h, b: (layer, 0, 0)),
            pl.BlockSpec((seq, F_GW), lambda h, b: (b, OFF_F // F_GW + h)),
            pl.BlockSpec((seq, F_GW), lambda h, b: (b, OFF_GF // F_GW + h)),
        ] + [const(t) for t in dft_consts],
        out_specs=[pl.BlockSpec((seq, wide), lambda h, b: (b, h)),
                   pl.BlockSpec((seq, F_GW), lambda h, b: (b, h))],
        out_shape=[jax.ShapeDtypeStruct((T, W_C), BF16), jax.ShapeDtypeStruct((T, W_B), BF16)],
        scratch_shapes=[pltpu.VMEM((hps, seq, hd), BF16), pltpu.VMEM((hps, seq, hd), BF16),
                        pltpu.VMEM((hps, CTX_LEN, hd), BF16),
                        pltpu.VMEM((hps, seq, 2 * hd), BF16), pltpu.VMEM((hps, CTX_LEN, 2 * hd), BF16),
                        pltpu.VMEM((hps, ntypes, Q_ROWS * GRID_W, WIN_ROWS * GRID_W), F32),
                        pltpu.VMEM((1, F_GW // LANES, seq, LANES), F32)],
        compiler_params=_params("arbitrary", "arbitrary"),
        name="nbr_attn",
    )(rpb_flat, z, z, z, z, zc, zc, qg, kg, z, z, *dft_consts)


def _ctx_attn_kernel(q_ref, k_ref, v_ref, gn_ref, qg_ref, kg_ref, o_ref):
    for h in range(NA_HEADS):
        cols = slice(h * HEAD_DIM, (h + 1) * HEAD_DIM)
        q = _head_rms(q_ref[:, cols], qg_ref[0] * ATT_SCALE).astype(BF16)
        k = _head_rms(k_ref[:, cols], kg_ref[0]).astype(BF16)
        s = _dot_nt(q, k)
        p = jnp.exp2(s - jnp.max(s, axis=-1, keepdims=True))
        o = _dot(p.astype(BF16), v_ref[:, cols].astype(BF16)) / jnp.sum(p, axis=-1, keepdims=True)
        o_ref[:, cols] = (o * gn_ref[:, cols].astype(F32)).astype(o_ref.dtype)


def _ctx_attention(zc, qg, kg, layer):
    T = zc.shape[0]
    blk = lambda off: pl.BlockSpec((CTX_LEN, W_C), lambda b: (b, off // W_C))
    vec = pl.BlockSpec((1, 1, HEAD_DIM), lambda b: (layer, 0, 0))
    return pl.pallas_call(
        _ctx_attn_kernel,
        grid=(T // CTX_LEN,),
        in_specs=[blk(OFF_Q), blk(OFF_K), blk(OFF_V), blk(OFF_GN), vec, vec],
        out_specs=pl.BlockSpec((CTX_LEN, W_C), lambda b: (b, 0)),
        out_shape=jax.ShapeDtypeStruct((T, W_C), BF16),
        compiler_params=_params("arbitrary"),
        name="ctx_attn",
    )(zc, zc, zc, zc, qg, kg)


def _merge_kernel(u_ref, vg_ref, ga_ref, f_ref, n_ref, ga0, ga1, gf0, gf1, gn0, gn1, x_ref, gate_ref,
                  lng_ref, lnb_ref, ws_ref, bst_ref, wpa_ref, wpf_ref, wpn_ref, wout_ref, o_ref,
                  a_scr, vn_scr, yfn_scr, y_scr):
    f = f_ref[...]
    n = n_ref[...]
    for half, (gf, gn) in enumerate(((gf0, gn0), (gf1, gn1))):
        cols = slice(half * COL, (half + 1) * COL)
        yfn_scr[:, cols] = (gf[...].astype(F32) * _dot(f, wpf_ref[0, :, cols])
                            + gn[...].astype(F32) * _dot(n, wpn_ref[0, :, cols]))
    _spatial_gating(u_ref, vg_ref, ga_ref, lng_ref, lnb_ref, ws_ref, bst_ref, a_scr, vn_scr)
    a = a_scr[...]
    for half, ga in enumerate((ga0, ga1)):
        cols = slice(half * COL, (half + 1) * COL)
        y = yfn_scr[:, cols] + ga[...].astype(F32) * _dot(a, wpa_ref[0, :, cols])
        y_scr[:, cols] = y.astype(BF16)
    o_ref[...] = x_ref[...] + gate_ref[0] * _dot(y_scr[...], wout_ref[0])


def _merge(f, n, z, x2, mod, lng, lnb, ws, bst, wpa, wpf, wpn, wout, layer, *, rows_per_mod, mod_row0):
    T, D = x2.shape
    tm = MERGE_ROWS
    mrow = lambda i: layer * MOD_ROWS + mod_row0 + i // rows_per_mod
    act = pl.BlockSpec((tm, COL), lambda i: (i, 0))
    zblk = lambda c: pl.BlockSpec((tm, COL), lambda i: (i, c))
    const = lambda *shape: pl.BlockSpec((1,) + shape, lambda i: (layer,) + (0,) * len(shape),
                                        pipeline_mode=pl.Buffered(1))
    return pl.pallas_call(
        _merge_kernel,
        grid=(T // tm,),
        in_specs=[
            zblk(OFF_U // COL), zblk(OFF_VG // COL), zblk(OFF_GA // COL),
            act, act,
        ] + [zblk(OFF_MERGE // COL + c) for c in range(3 * D // COL)] + [
            pl.BlockSpec((tm, D), lambda i: (i, 0)),
            pl.BlockSpec((1, 1, D), lambda i: (mrow(i), 0, 2)),
            const(1, W_A), const(1, W_A), const(GMLP_GROUPS, CHUNK, CHUNK), const(CHUNK, GMLP_GROUPS),
            const(W_A, D), const(W_B, D), const(W_C, D), const(D, D),
        ],
        out_specs=pl.BlockSpec((tm, D), lambda i: (i, 0)),
        out_shape=jax.ShapeDtypeStruct((T, D), F32),
        scratch_shapes=[pltpu.VMEM((tm, W_A), BF16), pltpu.VMEM((tm, W_A), BF16),
                        pltpu.VMEM((tm, D), F32), pltpu.VMEM((tm, D), BF16)],
        compiler_params=_params("arbitrary"),
        name="merge_out",
    )(z, z, z, f, n, z, z, z, z, z, z, x2, mod, lng, lnb, ws, bst, wpa, wpf, wpn, wout)


def kernel(x, c, ctx, c_ctx, norm_g, w_ada, b_ada, w_in, gmlp_ln_g, gmlp_ln_b, gmlp_ws, gmlp_bs,
           q_norm_g, k_norm_g, rpb, w_pa, w_pf, w_pn, w_out):
    B, S, D = x.shape
    C = ctx.shape[1]
    L = w_in.shape[0]
    assert D == D_MODEL and C == CTX_LEN and B < MOD_ROWS
    assert S % (GRID_W * Q_ROWS) == 0 and S // GRID_W >= WIN_ROWS

    cc = jnp.concatenate([c, c_ctx[None], jnp.zeros((MOD_ROWS - B - 1, D), F32)], axis=0)
    mod = _modulation(cc, w_ada, b_ada).reshape(L * MOD_ROWS, 1, 3 * D)

    ws = gmlp_ws.astype(BF16)
    bst = jnp.swapaxes(gmlp_bs, 1, 2)
    rpb_flat = rpb.reshape(-1)
    norm_g, gmlp_ln_g, gmlp_ln_b, q_norm_g, k_norm_g = (
        t[:, None, :] for t in (norm_g, gmlp_ln_g, gmlp_ln_b, q_norm_g, k_norm_g))
    dft_lat = _dft_consts(S)
    dft_ctx = _dft_consts(C)
    x2 = x.reshape(B * S, D)
    c2 = ctx.reshape(B * C, D)
    for l in range(L):
        last = l == L - 1
        if last:
            zc, = _in_proj(c2, norm_g, mod, w_in, l, rows_per_mod=ONE_ROW, mod_row0=B,
                           col0=OFF_K // COL, ncol=2)
            ck, cv = 0, W_C // HEAD_DIM
        else:
            zc, = _in_proj(c2, norm_g, mod, w_in, l, rows_per_mod=ONE_ROW, mod_row0=B, col0=0,
                           ncol=W_IN // COL)
            ck, cv = OFF_K // HEAD_DIM, OFF_V // HEAD_DIM
        z, *cast = _in_proj(x2, norm_g, mod, w_in, l, rows_per_mod=S // IN_PROJ_ROWS, mod_row0=0,
                            col0=0, ncol=W_IN // COL, cast=(w_pa, w_pf, w_pn, w_out) if l == 0 else ())
        if l == 0:
            wpa, wpf, wpn, wout = cast

        n, f = _nbr_attention(z, zc, ck, cv, q_norm_g, k_norm_g, rpb_flat, dft_lat, l, S)
        x_new = _merge(f, n, z, x2, mod, gmlp_ln_g, gmlp_ln_b, ws, bst, wpa, wpf, wpn, wout, l,
                       rows_per_mod=S // MERGE_ROWS, mod_row0=0)

        if not last:
            nc = _ctx_attention(zc, q_norm_g, k_norm_g, l)
            fc = _fourier(zc, C, dft_ctx)
            c2 = _merge(fc, nc, zc, c2, mod, gmlp_ln_g, gmlp_ln_b, ws, bst, wpa, wpf, wpn, wout, l,
                        rows_per_mod=ONE_ROW, mod_row0=B)
        x2 = x_new
    return x2.reshape(B, S, D)
```

```python
import functools

import jax
import jax.numpy as jnp
import numpy as np
from jax import lax
from jax.experimental import pallas as pl
from jax.experimental.pallas import tpu as pltpu

F32 = jnp.float32
BF16 = jnp.bfloat16

D_MODEL = 2048
CTX_LEN = 256
GRID_W = 64
EPS = 1e-6
CHUNK = 128
GMLP_GROUPS = 8
W_A = D_MODEL // 2
W_B = D_MODEL // 2
F_GROUPS = 4
F_GW = W_B // F_GROUPS
HEAD_DIM = 128
NA_HEADS = 8
W_C = NA_HEADS * HEAD_DIM
NA_KH = 8
NA_KW = 16

OFF_U = 0
OFF_VG = OFF_U + W_A
OFF_GA = OFF_VG + W_A
OFF_F = OFF_GA + W_A
OFF_GF = OFF_F + W_B
OFF_Q = OFF_GF + W_B
OFF_K = OFF_Q + W_C
OFF_V = OFF_K + W_C
OFF_GN = OFF_V + W_C
OFF_MERGE = OFF_GN + W_C
W_IN = OFF_MERGE + 3 * D_MODEL

LANES = 128
BF16_SUBLANES = 16
IN_PROJ_ROWS = 1024
MERGE_ROWS = 256
COL = 1024
MOD_ROWS = 16
ONE_ROW = 1 << 30
VMEM_LIMIT = 56 * 1024 * 1024
Z_DTYPE = jnp.bfloat16

Q_ROWS = 2
WIN_ROWS = 10
SCORE_LOOKAHEAD = 3
HEADS_PER_STEP = 2
NEG = -1e30
LOG2E = float(np.log2(np.e))
ATT_SCALE = HEAD_DIM ** -0.5 * LOG2E


GELU_C1 = float(np.sqrt(2.0 / np.pi))
GELU_C2 = GELU_C1 * 0.044715


def _sigmoid(t):
    return 0.5 * jnp.tanh(0.5 * t) + 0.5


def _silu(t):
    h = 0.5 * t
    return h * jnp.tanh(h) + h


def _gelu(t):
    h = 0.5 * t
    return h * jnp.tanh(t * (GELU_C1 + GELU_C2 * (t * t))) + h


def _dot(a, b):
    return jnp.dot(a, b, preferred_element_type=F32)


def _dot_nt(a, b):
    return lax.dot_general(a, b, (((1,), (1,)), ((), ())), preferred_element_type=F32)


def _params(*sem):
    return pltpu.CompilerParams(dimension_semantics=sem, vmem_limit_bytes=VMEM_LIMIT)


def _mod_kernel(cc_ref, w_ref, b_ref, o_ref):
    s = _silu(cc_ref[...])
    s_hi = s.astype(BF16)
    s_lo = (s - s_hi.astype(F32)).astype(BF16)
    w = w_ref[0]
    w_hi = w.astype(BF16)
    w_lo = (w - w_hi.astype(F32)).astype(BF16)
    o_ref[0] = _dot(s_hi, w_hi) + _dot(s_lo, w_hi) + _dot(s_hi, w_lo) + b_ref[0]


def _modulation(cc, w_ada, b_ada):
    L, D, N = w_ada.shape
    tn = COL
    return pl.pallas_call(
        _mod_kernel,
        grid=(L, N // tn),
        in_specs=[
            pl.BlockSpec((MOD_ROWS, D), lambda l, j: (0, 0)),
            pl.BlockSpec((1, D, tn), lambda l, j: (l, 0, j)),
            pl.BlockSpec((1, 1, tn), lambda l, j: (l, 0, j)),
        ],
        out_specs=pl.BlockSpec((1, MOD_ROWS, tn), lambda l, j: (l, 0, j)),
        out_shape=jax.ShapeDtypeStruct((L, MOD_ROWS, N), F32),
        compiler_params=_params("arbitrary", "arbitrary"),
        name="adaln_mod",
    )(cc, w_ada, b_ada.reshape(L, 1, N))


def _column_activation(c):
    if c < OFF_GA // COL:
        return _gelu
    if c in (OFF_GA // COL, OFF_GF // COL, OFF_GN // COL):
        return _silu
    if c >= OFF_MERGE // COL:
        return _sigmoid
    return lambda z: z


def _in_proj_kernel(x_ref, g_ref, sh_ref, sc_ref, w_ref, *rest, col0, ncol, n_cast):
    cast_in, (o_ref, *cast_out, h_scr) = rest[:n_cast], rest[n_cast:]
    i = pl.program_id(0)
    j = pl.program_id(1)
    slot = i % 2
    next_norm = (j == ncol - 1) & (i + 1 < pl.num_programs(0))

    for src, dst in zip(cast_in, cast_out):
        dst[...] = src[...].astype(dst.dtype)

    def norm_into(dst):
        x = x_ref[...]
        gain = g_ref[0] * (1.0 + sc_ref[0])
        r = lax.rsqrt(jnp.mean(x * x, axis=-1, keepdims=True) + EPS)
        h_scr[dst] = (x * r * gain + sh_ref[0]).astype(BF16)

    @pl.when((i == 0) & (j == 0))
    def _():
        norm_into(0)

    c = j + col0
    plain = jnp.logical_not(next_norm)

    def project(act):
        o_ref[...] = act(_dot(h_scr[slot], w_ref[0].astype(BF16))).astype(o_ref.dtype)

    @pl.when(plain & (c < OFF_GA // COL))
    def _():
        project(_gelu)

    @pl.when(plain & ((c == OFF_GA // COL) | (c == OFF_GF // COL) | (c == OFF_GN // COL)))
    def _():
        project(_silu)

    @pl.when(plain & (c >= OFF_MERGE // COL))
    def _():
        project(_sigmoid)

    @pl.when(plain & ((c == OFF_F // COL) | ((c >= OFF_Q // COL) & (c < OFF_GN // COL))))
    def _():
        project(lambda z: z)

    @pl.when(next_norm)
    def _():
        norm_into(1 - slot)
        project(_column_activation(col0 + ncol - 1))


def _in_proj(x2, norm_g, mod, w_in, layer, *, rows_per_mod, mod_row0, col0, ncol, cast=()):
    T, D = x2.shape
    tm = min(IN_PROJ_ROWS, T)
    nrow = T // tm
    nsteps = nrow * ncol
    norm_tile = lambda i, j: jnp.minimum(i + (j == ncol - 1).astype(jnp.int32), nrow - 1)
    mrow = lambda i, j: layer * MOD_ROWS + mod_row0 + norm_tile(i, j) // rows_per_mod

    cast2d = [t.reshape(-1, t.shape[-1]) for t in cast]
    cast_specs = []
    for t in cast2d:
        rows = BF16_SUBLANES * pl.cdiv(t.shape[0], BF16_SUBLANES * nsteps)
        while t.shape[0] % rows:
            rows += BF16_SUBLANES
        nblk = t.shape[0] // rows
        cast_specs.append(pl.BlockSpec(
            (rows, t.shape[1]), lambda i, j, nblk=nblk: (jnp.minimum(i * ncol + j, nblk - 1), 0)))

    z, *casted = pl.pallas_call(
        functools.partial(_in_proj_kernel, col0=col0, ncol=ncol, n_cast=len(cast)),
        grid=(nrow, ncol),
        in_specs=[
            pl.BlockSpec((tm, D), lambda i, j: (norm_tile(i, j), 0)),
            pl.BlockSpec((1, 1, D), lambda i, j: (layer, 0, 0)),
            pl.BlockSpec((1, 1, D), lambda i, j: (mrow(i, j), 0, 0)),
            pl.BlockSpec((1, 1, D), lambda i, j: (mrow(i, j), 0, 1)),
            pl.BlockSpec((1, D, COL), lambda i, j: (layer, 0, j + col0)),
        ] + cast_specs,
        out_specs=[pl.BlockSpec((tm, COL), lambda i, j: (i, j))] + cast_specs,
        out_shape=[jax.ShapeDtypeStruct((T, ncol * COL), Z_DTYPE)]
        + [jax.ShapeDtypeStruct(t.shape, BF16) for t in cast2d],
        scratch_shapes=[pltpu.VMEM((2, tm, D), BF16)],
        compiler_params=_params("arbitrary", "arbitrary"),
        name="in_proj",
    )(x2, norm_g, mod, mod, w_in, *cast2d)
    return (z, *(t.reshape(s.shape) for t, s in zip(casted, cast)))


def _spatial_gating(u_ref, vg_ref, ga_ref, lng_ref, lnb_ref, ws_ref, bst_ref, o_ref, vn_scr):
    v = vg_ref[...].astype(F32)
    mu = jnp.mean(v, axis=-1, keepdims=True)
    vc = v - mu
    vn = vc * lax.rsqrt(jnp.mean(vc * vc, axis=-1, keepdims=True) + EPS)
    vn_scr[...] = (vn * lng_ref[0] + lnb_ref[0]).astype(BF16)
    nchunk = u_ref.shape[0] // CHUNK
    for g in range(GMLP_GROUPS):
        cols = slice(g * CHUNK, (g + 1) * CHUNK)
        v_all = jnp.concatenate([vn_scr[ch * CHUNK:(ch + 1) * CHUNK, cols] for ch in range(nchunk)], axis=1)
        sv_all = _dot(ws_ref[0, g], v_all) + bst_ref[0, :, g:g + 1]
        for ch in range(nchunk):
            rows = slice(ch * CHUNK, (ch + 1) * CHUNK)
            sv = sv_all[:, ch * CHUNK:(ch + 1) * CHUNK]
            a = u_ref[rows, cols].astype(F32) * sv * ga_ref[rows, cols].astype(F32)
            o_ref[rows, cols] = a.astype(o_ref.dtype)


DFT_RADIX = 4


def _dft_consts(n):
    m = n // DFT_RADIX
    k = np.arange(m, dtype=np.int64)
    ang = 2.0 * np.pi * ((k[:, None] * k[None, :]) % m).astype(np.float64) / m
    cs = np.concatenate([np.cos(ang), np.sin(ang)], axis=0)
    th = 2.0 * np.pi * np.arange(n // 2, dtype=np.float64) / n
    twc = np.broadcast_to(np.cos(th)[:, None], (n // 2, LANES))
    tws = np.broadcast_to(np.sin(th)[:, None], (n // 2, LANES))
    c = np.arange(F_GW, dtype=np.int64)
    angc = 2.0 * np.pi * ((c[:, None] * c[None, :]) % F_GW).astype(np.float64) / F_GW
    bf = lambda t: jnp.asarray(t, dtype=F32).astype(BF16)
    return bf(cs), jnp.asarray(twc, dtype=F32), jnp.asarray(tws, dtype=F32), bf(np.cos(angc)), bf(np.sin(angc))


def _dft_stages(f_ref, gf_ref, cs_ref, twc_ref, tws_ref, cc_ref, sc_ref, o_ref, x_scr):
    n = f_ref.shape[0]
    m0 = n // DFT_RADIX
    lane_tiles = F_GW // LANES

    def position_dft(g):
        for t in range(lane_tiles):
            x_scr[g, t] = f_ref[:, g * F_GW + t * LANES:g * F_GW + (t + 1) * LANES].astype(F32)
        xcat = jnp.concatenate(
            [x_scr[g, t, pl.ds(r, m0, stride=DFT_RADIX), :].astype(BF16)
             for r in range(DFT_RADIX) for t in range(lane_tiles)], axis=1)
        return _dot(cs_ref[...], xcat)

    def butterflies(e):
        base = [(e[:m0, r * F_GW:(r + 1) * F_GW], e[m0:, r * F_GW:(r + 1) * F_GW])
                for r in range(DFT_RADIX)]

        def sub_dft(stride, offset):
            if stride == DFT_RADIX:
                return base[offset]
            ec0, es0 = sub_dft(2 * stride, offset)
            ec1, es1 = sub_dft(2 * stride, offset + stride)
            m = n // (2 * stride)
            c = jnp.concatenate([twc_ref[pl.ds(0, m, stride=stride), :]] * lane_tiles, axis=1)
            s = jnp.concatenate([tws_ref[pl.ds(0, m, stride=stride), :]] * lane_tiles, axis=1)
            a = c * ec1 - s * es1
            b = c * es1 + s * ec1
            return (jnp.concatenate([ec0 + a, ec0 - a], axis=0),
                    jnp.concatenate([es0 + b, es0 - b], axis=0))

        return sub_dft(1, 0)

    def channel_dft(g, tc, ts):
        cols = slice(g * F_GW, (g + 1) * F_GW)
        y = _dot(tc.astype(BF16), cc_ref[...]) - _dot(ts.astype(BF16), sc_ref[...])
        o_ref[:, cols] = (y * ((n * F_GW) ** -0.5) * gf_ref[:, cols].astype(F32)).astype(o_ref.dtype)

    return position_dft, butterflies, channel_dft


def _fourier_kernel(*refs):
    position_dft, butterflies, channel_dft = _dft_stages(*refs)
    nxt = position_dft(0)
    for g in range(F_GROUPS):
        e = nxt
        if g + 1 < F_GROUPS:
            nxt = position_dft(g + 1)
        channel_dft(g, *butterflies(e))


def _fourier(z, n, consts):
    T = z.shape[0]
    const = lambda t: pl.BlockSpec(t.shape, lambda b: (0, 0), pipeline_mode=pl.Buffered(1))
    return pl.pallas_call(
        _fourier_kernel,
        grid=(T // n,),
        in_specs=[
            pl.BlockSpec((n, W_B), lambda b: (b, OFF_F // W_B)),
            pl.BlockSpec((n, W_B), lambda b: (b, OFF_GF // W_B)),
        ] + [const(t) for t in consts],
        out_specs=pl.BlockSpec((n, W_B), lambda b: (b, 0)),
        out_shape=jax.ShapeDtypeStruct((T, W_B), BF16),
        scratch_shapes=[pltpu.VMEM((F_GROUPS, F_GW // LANES, n, LANES), F32)],
        compiler_params=_params("arbitrary"),
        name="fourier",
    )(z, z, *consts)


def _head_rms(t, g):
    t = t.astype(F32)
    return t * lax.rsqrt(jnp.mean(t * t, axis=-1, keepdims=True) + EPS) * g


def _block_geometry(rows):
    assert rows % Q_ROWS == 0 and WIN_ROWS % 2 == 0 and Q_ROWS + NA_KH - 1 <= WIN_ROWS <= rows
    geo, types = [], {}
    for blk in range(rows // Q_ROWS):
        base = min(max(Q_ROWS * blk - NA_KH // 2, 0), rows - WIN_ROWS)
        starts = tuple(_window_start(Q_ROWS * blk + i, rows) - base for i in range(Q_ROWS))
        signature = (base - Q_ROWS * blk, starts)
        geo.append((base, types.setdefault(signature, len(types))))
    return geo


def _window_start(r, rows):
    return min(max(r - NA_KH // 2, 0), rows - NA_KH)


def _build_bias_tables(rpb_ref, tab_scr, head, geo, rows):
    n_dr, n_dc = 2 * NA_KH - 1, 2 * NA_KW - 1
    shape = (GRID_W, 2 * GRID_W)
    lane = lax.broadcasted_iota(jnp.int32, shape, 1)
    cq = lax.broadcasted_iota(jnp.int32, shape, 0)
    kc = lane & (GRID_W - 1)
    dc = kc - cq + (NA_KW - 1)
    cs = jnp.clip(cq - NA_KW // 2, 0, GRID_W - NA_KW)
    col_ok = (kc >= cs) & (kc < cs + NA_KW)
    neg = jnp.full(shape, NEG, F32)
    row_bias = []
    for d in range(n_dr):
        t = neg
        for b in range(n_dc):
            t = jnp.where(dc == b, rpb_ref[(head * n_dr + d) * n_dc + b] * LOG2E, t)
        row_bias.append(jnp.where(col_ok, t, NEG))
    first_block = {typ: blk for blk, (_, typ) in reversed(list(enumerate(geo)))}
    for typ, blk in first_block.items():
        base = geo[blk][0]
        for i in range(Q_ROWS):
            r = Q_ROWS * blk + i
            rs = _window_start(r, rows)
            for wp in range(WIN_ROWS // 2):
                halves = []
                for kr in (base + 2 * wp, base + 2 * wp + 1):
                    halves.append(row_bias[kr - r + NA_KH - 1] if rs <= kr < rs + NA_KH else neg)
                tab_scr[typ, i * GRID_W:(i + 1) * GRID_W, wp * 2 * GRID_W:(wp + 1) * 2 * GRID_W] = (
                    jnp.where(lane < GRID_W, halves[0], halves[1]))


def _nbr_attn_kernel(rpb_ref, q_ref, k_ref, v_ref, gn_ref, kc_ref, vc_ref, qg_ref, kg_ref,
                     f_ref, gf_ref, cs_ref, twc_ref, tws_ref, cc_ref, sc_ref, o_ref, fo_ref,
                     qn_scr, kn_scr, kcn_scr, vx_scr, vcx_scr, tab_ref, x_scr, *, geo, rows, layer):
    hd = HEAD_DIM
    position_dft, butterflies, channel_dft = _dft_stages(
        f_ref, gf_ref, cs_ref, twc_ref, tws_ref, cc_ref, sc_ref, fo_ref, x_scr)

    @pl.when(pl.program_id(1) == 0)
    def _():
        for hh in range(HEADS_PER_STEP):
            head = layer * NA_HEADS + pl.program_id(0) * HEADS_PER_STEP + hh
            _build_bias_tables(rpb_ref, tab_ref.at[hh], head, geo, rows)
            vx_scr[hh, :, hd:] = jnp.ones((vx_scr.shape[1], hd), BF16)
            vcx_scr[hh, :, hd:] = jnp.ones((vcx_scr.shape[1], hd), BF16)

    nq = Q_ROWS * GRID_W
    nk = WIN_ROWS * GRID_W

    def prepare(hh):
        cols = slice(hh * hd, (hh + 1) * hd)
        qn_scr[hh] = _head_rms(q_ref[:, cols], qg_ref[0] * ATT_SCALE).astype(BF16)
        kn_scr[hh] = _head_rms(k_ref[:, cols], kg_ref[0]).astype(BF16)
        kcn_scr[hh] = _head_rms(kc_ref[:, cols], kg_ref[0]).astype(BF16)
        vx_scr[hh, :, :hd] = v_ref[:, cols].astype(BF16)
        vcx_scr[hh, :, :hd] = vc_ref[:, cols].astype(BF16)

    def scores(hh, blk):
        base, typ = geo[blk]
        q = qn_scr[hh, blk * nq:(blk + 1) * nq, :]
        s_loc = _dot_nt(q, kn_scr[hh, base * GRID_W:base * GRID_W + nk, :]) + tab_ref[hh, typ]
        return s_loc, _dot_nt(q, kcn_scr[hh])

    work = [(hh, blk) for hh in range(HEADS_PER_STEP) for blk in range(len(geo))]
    dft = position_dft(0)
    prepare(0)
    pending = [scores(*w) for w in work[:SCORE_LOOKAHEAD]]
    dft = butterflies(dft)
    for n, (hh, blk) in enumerate(work):
        if blk == 0 and hh + 1 < HEADS_PER_STEP:
            prepare(hh + 1)
        if n == 1:
            channel_dft(0, *dft)
        base, typ = geo[blk]
        qs = slice(blk * nq, (blk + 1) * nq)
        ks = slice(base * GRID_W, base * GRID_W + nk)
        cols = slice(hh * hd, (hh + 1) * hd)
        s_loc, s_ctx = pending.pop(0)
        if n + SCORE_LOOKAHEAD < len(work):
            pending.append(scores(*work[n + SCORE_LOOKAHEAD]))
        m = jnp.maximum(jnp.max(s_loc, axis=-1, keepdims=True), jnp.max(s_ctx, axis=-1, keepdims=True))
        p_loc = jnp.exp2(s_loc - m)
        p_ctx = jnp.exp2(s_ctx - m)
        ox = _dot(p_loc.astype(BF16), vx_scr[hh, ks, :]) + _dot(p_ctx.astype(BF16), vcx_scr[hh])
        o_ref[qs, cols] = (ox[:, :hd] / ox[:, hd:] * gn_ref[qs, cols].astype(F32)).astype(o_ref.dtype)


def _nbr_attention(z, zc, ck, cv, qg, kg, rpb_flat, dft_consts, layer, seq):
    T = z.shape[0]
    rows = seq // GRID_W
    geo = _block_geometry(rows)
    hd = HEAD_DIM
    hps = HEADS_PER_STEP
    wide = hps * hd
    assert NA_HEADS % hps == 0 and ck % hps == 0 and cv % hps == 0
    assert NA_HEADS // hps == F_GROUPS
    const = lambda t: pl.BlockSpec(t.shape, lambda h, b: (0, 0), pipeline_mode=pl.Buffered(1))
    ntypes = 1 + max(t for _, t in geo)
    lat = lambda off: pl.BlockSpec((seq, wide), lambda h, b: (b, off // wide + h))
    return pl.pallas_call(
        functools.partial(_nbr_attn_kernel, geo=geo, rows=rows, layer=layer),
        grid=(NA_HEADS // hps, T // seq),
        in_specs=[
            pl.BlockSpec(memory_space=pltpu.SMEM),
            lat(OFF_Q), lat(OFF_K), lat(OFF_V), lat(OFF_GN),
            pl.BlockSpec((CTX_LEN, wide), lambda h, b: (b, ck // hps + h)),
            pl.BlockSpec((CTX_LEN, wide), lambda h, b: (b, cv // hps + h)),
            pl.BlockSpec((1, 1, hd), lambda h, b: (layer, 0, 0)),
            pl.BlockSpec((1, 1, hd), lambda h, b: (layer, 0, 0)),
            pl.BlockSpec((seq, F_GW), lambda h, b: (b, OFF_F // F_GW + h)),
            pl.BlockSpec((seq, F_GW), lambda h, b: (b, OFF_GF // F_GW + h)),
        ] + [const(t) for t in dft_consts],
        out_specs=[pl.BlockSpec((seq, wide), lambda h, b: (b, h)),
                   pl.BlockSpec((seq, F_GW), lambda h, b: (b, h))],
        out_shape=[jax.ShapeDtypeStruct((T, W_C), BF16), jax.ShapeDtypeStruct((T, W_B), BF16)],
        scratch_shapes=[pltpu.VMEM((hps, seq, hd), BF16), pltpu.VMEM((hps, seq, hd), BF16),
                        pltpu.VMEM((hps, CTX_LEN, hd), BF16),
                        pltpu.VMEM((hps, seq, 2 * hd), BF16), pltpu.VMEM((hps, CTX_LEN, 2 * hd), BF16),
                        pltpu.VMEM((hps, ntypes, Q_ROWS * GRID_W, WIN_ROWS * GRID_W), F32),
                        pltpu.VMEM((1, F_GW // LANES, seq, LANES), F32)],
        compiler_params=_params("arbitrary", "arbitrary"),
        name="nbr_attn",
    )(rpb_flat, z, z, z, z, zc, zc, qg, kg, z, z, *dft_consts)


def _ctx_attn_kernel(q_ref, k_ref, v_ref, gn_ref, qg_ref, kg_ref, o_ref):
    for h in range(NA_HEADS):
        cols = slice(h * HEAD_DIM, (h + 1) * HEAD_DIM)
        q = _head_rms(q_ref[:, cols], qg_ref[0] * ATT_SCALE).astype(BF16)
        k = _head_rms(k_ref[:, cols], kg_ref[0]).astype(BF16)
        s = _dot_nt(q, k)
        p = jnp.exp2(s - jnp.max(s, axis=-1, keepdims=True))
        o = _dot(p.astype(BF16), v_ref[:, cols].astype(BF16)) / jnp.sum(p, axis=-1, keepdims=True)
        o_ref[:, cols] = (o * gn_ref[:, cols].astype(F32)).astype(o_ref.dtype)


def _ctx_attention(zc, qg, kg, layer):
    T = zc.shape[0]
    blk = lambda off: pl.BlockSpec((CTX_LEN, W_C), lambda b: (b, off // W_C))
    vec = pl.BlockSpec((1, 1, HEAD_DIM), lambda b: (layer, 0, 0))
    return pl.pallas_call(
        _ctx_attn_kernel,
        grid=(T // CTX_LEN,),
        in_specs=[blk(OFF_Q), blk(OFF_K), blk(OFF_V), blk(OFF_GN), vec, vec],
        out_specs=pl.BlockSpec((CTX_LEN, W_C), lambda b: (b, 0)),
        out_shape=jax.ShapeDtypeStruct((T, W_C), BF16),
        compiler_params=_params("arbitrary"),
        name="ctx_attn",
    )(zc, zc, zc, zc, qg, kg)


def _merge_kernel(u_ref, vg_ref, ga_ref, f_ref, n_ref, ga0, ga1, gf0, gf1, gn0, gn1, x_ref, gate_ref,
                  lng_ref, lnb_ref, ws_ref, bst_ref, wpa_ref, wpf_ref, wpn_ref, wout_ref, o_ref,
                  a_scr, vn_scr, yfn_scr, y_scr):
    f = f_ref[...]
    n = n_ref[...]
    for half, (gf, gn) in enumerate(((gf0, gn0), (gf1, gn1))):
        cols = slice(half * COL, (half + 1) * COL)
        yfn_scr[:, cols] = (gf[...].astype(F32) * _dot(f, wpf_ref[0, :, cols])
                            + gn[...].astype(F32) * _dot(n, wpn_ref[0, :, cols]))
    _spatial_gating(u_ref, vg_ref, ga_ref, lng_ref, lnb_ref, ws_ref, bst_ref, a_scr, vn_scr)
    a = a_scr[...]
    for half, ga in enumerate((ga0, ga1)):
        cols = slice(half * COL, (half + 1) * COL)
        y = yfn_scr[:, cols] + ga[...].astype(F32) * _dot(a, wpa_ref[0, :, cols])
        y_scr[:, cols] = y.astype(BF16)
    o_ref[...] = x_ref[...] + gate_ref[0] * _dot(y_scr[...], wout_ref[0])


def _merge(f, n, z, x2, mod, lng, lnb, ws, bst, wpa, wpf, wpn, wout, layer, *, rows_per_mod, mod_row0):
    T, D = x2.shape
    tm = MERGE_ROWS
    mrow = lambda i: layer * MOD_ROWS + mod_row0 + i // rows_per_mod
    act = pl.BlockSpec((tm, COL), lambda i: (i, 0))
    zblk = lambda c: pl.BlockSpec((tm, COL), lambda i: (i, c))
    const = lambda *shape: pl.BlockSpec((1,) + shape, lambda i: (layer,) + (0,) * len(shape),
                                        pipeline_mode=pl.Buffered(1))
    return pl.pallas_call(
        _merge_kernel,
        grid=(T // tm,),
        in_specs=[
            zblk(OFF_U // COL), zblk(OFF_VG // COL), zblk(OFF_GA // COL),
            act, act,
        ] + [zblk(OFF_MERGE // COL + c) for c in range(3 * D // COL)] + [
            pl.BlockSpec((tm, D), lambda i: (i, 0)),
            pl.BlockSpec((1, 1, D), lambda i: (mrow(i), 0, 2)),
            const(1, W_A), const(1, W_A), const(GMLP_GROUPS, CHUNK, CHUNK), const(CHUNK, GMLP_GROUPS),
            const(W_A, D), const(W_B, D), const(W_C, D), const(D, D),
        ],
        out_specs=pl.BlockSpec((tm, D), lambda i: (i, 0)),
        out_shape=jax.ShapeDtypeStruct((T, D), F32),
        scratch_shapes=[pltpu.VMEM((tm, W_A), BF16), pltpu.VMEM((tm, W_A), BF16),
                        pltpu.VMEM((tm, D), F32), pltpu.VMEM((tm, D), BF16)],
        compiler_params=_params("arbitrary"),
        name="merge_out",
    )(z, z, z, f, n, z, z, z, z, z, z, x2, mod, lng, lnb, ws, bst, wpa, wpf, wpn, wout)


def kernel(x, c, ctx, c_ctx, norm_g, w_ada, b_ada, w_in, gmlp_ln_g, gmlp_ln_b, gmlp_ws, gmlp_bs,
           q_norm_g, k_norm_g, rpb, w_pa, w_pf, w_pn, w_out):
    B, S, D = x.shape
    C = ctx.shape[1]
    L = w_in.shape[0]
    assert D == D_MODEL and C == CTX_LEN and B < MOD_ROWS
    assert S % (GRID_W * Q_ROWS) == 0 and S // GRID_W >= WIN_ROWS

    cc = jnp.concatenate([c, c_ctx[None], jnp.zeros((MOD_ROWS - B - 1, D), F32)], axis=0)
    mod = _modulation(cc, w_ada, b_ada).reshape(L * MOD_ROWS, 1, 3 * D)

    ws = gmlp_ws.astype(BF16)
    bst = jnp.swapaxes(gmlp_bs, 1, 2)
    rpb_flat = rpb.reshape(-1)
    norm_g, gmlp_ln_g, gmlp_ln_b, q_norm_g, k_norm_g = (
        t[:, None, :] for t in (norm_g, gmlp_ln_g, gmlp_ln_b, q_norm_g, k_norm_g))
    dft_lat = _dft_consts(S)
    dft_ctx = _dft_consts(C)
    x2 = x.reshape(B * S, D)
    c2 = ctx.reshape(B * C, D)
    for l in range(L):
        last = l == L - 1
        if last:
            zc, = _in_proj(c2, norm_g, mod, w_in, l, rows_per_mod=ONE_ROW, mod_row0=B,
                           col0=OFF_K // COL, ncol=2)
            ck, cv = 0, W_C // HEAD_DIM
        else:
            zc, = _in_proj(c2, norm_g, mod, w_in, l, rows_per_mod=ONE_ROW, mod_row0=B, col0=0,
                           ncol=W_IN // COL)
            ck, cv = OFF_K // HEAD_DIM, OFF_V // HEAD_DIM
        z, *cast = _in_proj(x2, norm_g, mod, w_in, l, rows_per_mod=S // IN_PROJ_ROWS, mod_row0=0,
                            col0=0, ncol=W_IN // COL, cast=(w_pa, w_pf, w_pn, w_out) if l == 0 else ())
        if l == 0:
            wpa, wpf, wpn, wout = cast

        n, f = _nbr_attention(z, zc, ck, cv, q_norm_g, k_norm_g, rpb_flat, dft_lat, l, S)
        x_new = _merge(f, n, z, x2, mod, gmlp_ln_g, gmlp_ln_b, ws, bst, wpa, wpf, wpn, wout, l,
                       rows_per_mod=S // MERGE_ROWS, mod_row0=0)

        if not last:
            nc = _ctx_attention(zc, q_norm_g, k_norm_g, l)
            fc = _fourier(zc, C, dft_ctx)
            c2 = _merge(fc, nc, zc, c2, mod, gmlp_ln_g, gmlp_ln_b, ws, bst, wpa, wpf, wpn, wout, l,
                        rows_per_mod=ONE_ROW, mod_row0=B)
        x2 = x_new
    return x2.reshape(B, S, D)
```

```python
import functools

import jax
import jax.numpy as jnp
import numpy as np
from jax import lax
from jax.experimental import pallas as pl
from jax.experimental.pallas import tpu as pltpu

F32 = jnp.float32
BF16 = jnp.bfloat16

D_MODEL = 2048
CTX_LEN = 256
GRID_W = 64
EPS = 1e-6
CHUNK = 128
GMLP_GROUPS = 8
W_A = D_MODEL // 2
W_B = D_MODEL // 2
F_GROUPS = 4
F_GW = W_B // F_GROUPS
HEAD_DIM = 128
NA_HEADS = 8
W_C = NA_HEADS * HEAD_DIM
NA_KH = 8
NA_KW = 16

OFF_U = 0
OFF_VG = OFF_U + W_A
OFF_GA = OFF_VG + W_A
OFF_F = OFF_GA + W_A
OFF_GF = OFF_F + W_B
OFF_Q = OFF_GF + W_B
OFF_K = OFF_Q + W_C
OFF_V = OFF_K + W_C
OFF_GN = OFF_V + W_C
OFF_MERGE = OFF_GN + W_C
W_IN = OFF_MERGE + 3 * D_MODEL

LANES = 128
BF16_SUBLANES = 16
IN_PROJ_ROWS = 1024
MERGE_ROWS = 256
COL = 1024
MOD_ROWS = 16
ONE_ROW = 1 << 30
VMEM_LIMIT = 56 * 1024 * 1024
Z_DTYPE = jnp.bfloat16

Q_ROWS = 2
WIN_ROWS = 10
SCORE_LOOKAHEAD = 3
HEADS_PER_STEP = 2
NEG = -1e30
LOG2E = float(np.log2(np.e))
ATT_SCALE = HEAD_DIM ** -0.5 * LOG2E


GELU_C1 = float(np.sqrt(2.0 / np.pi))
GELU_C2 = GELU_C1 * 0.044715


def _sigmoid(t):
    return 0.5 * jnp.tanh(0.5 * t) + 0.5


def _silu(t):
    h = 0.5 * t
    return h * jnp.tanh(h) + h


def _gelu(t):
    h = 0.5 * t
    return h * jnp.tanh(t * (GELU_C1 + GELU_C2 * (t * t))) + h


def _dot(a, b):
    return jnp.dot(a, b, preferred_element_type=F32)


def _dot_nt(a, b):
    return lax.dot_general(a, b, (((1,), (1,)), ((), ())), preferred_element_type=F32)


def _params(*sem):
    return pltpu.CompilerParams(dimension_semantics=sem, vmem_limit_bytes=VMEM_LIMIT)


def _mod_kernel(cc_ref, w_ref, b_ref, o_ref):
    s = _silu(cc_ref[...])
    s_hi = s.astype(BF16)
    s_lo = (s - s_hi.astype(F32)).astype(BF16)
    w = w_ref[0]
    w_hi = w.astype(BF16)
    w_lo = (w - w_hi.astype(F32)).astype(BF16)
    o_ref[0] = _dot(s_hi, w_hi) + _dot(s_lo, w_hi) + _dot(s_hi, w_lo) + b_ref[0]


def _modulation(cc, w_ada, b_ada):
    L, D, N = w_ada.shape
    tn = COL
    return pl.pallas_call(
        _mod_kernel,
        grid=(L, N // tn),
        in_specs=[
            pl.BlockSpec((MOD_ROWS, D), lambda l, j: (0, 0)),
            pl.BlockSpec((1, D, tn), lambda l, j: (l, 0, j)),
            pl.BlockSpec((1, 1, tn), lambda l, j: (l, 0, j)),
        ],
        out_specs=pl.BlockSpec((1, MOD_ROWS, tn), lambda l, j: (l, 0, j)),
        out_shape=jax.ShapeDtypeStruct((L, MOD_ROWS, N), F32),
        compiler_params=_params("arbitrary", "arbitrary"),
        name="adaln_mod",
    )(cc, w_ada, b_ada.reshape(L, 1, N))


def _column_activation(c):
    if c < OFF_GA // COL:
        return _gelu
    if c in (OFF_GA // COL, OFF_GF // COL, OFF_GN // COL):
        return _silu
    if c >= OFF_MERGE // COL:
        return _sigmoid
    return lambda z: z


def _in_proj_kernel(x_ref, g_ref, sh_ref, sc_ref, w_ref, o_ref, h_scr, *, col0, ncol):
    i = pl.program_id(0)
    j = pl.program_id(1)
    slot = i % 2
    next_norm = (j == ncol - 1) & (i + 1 < pl.num_programs(0))

    def norm_into(dst):
        x = x_ref[...]
        gain = g_ref[0] * (1.0 + sc_ref[0])
        r = lax.rsqrt(jnp.mean(x * x, axis=-1, keepdims=True) + EPS)
        h_scr[dst] = (x * r * gain + sh_ref[0]).astype(BF16)

    @pl.when((i == 0) & (j == 0))
    def _():
        norm_into(0)

    c = j + col0
    plain = jnp.logical_not(next_norm)

    def project(act):
        o_ref[...] = act(_dot(h_scr[slot], w_ref[0].astype(BF16))).astype(o_ref.dtype)

    @pl.when(plain & (c < OFF_GA // COL))
    def _():
        project(_gelu)

    @pl.when(plain & ((c == OFF_GA // COL) | (c == OFF_GF // COL) | (c == OFF_GN // COL)))
    def _():
        project(_silu)

    @pl.when(plain & (c >= OFF_MERGE // COL))
    def _():
        project(_sigmoid)

    @pl.when(plain & ((c == OFF_F // COL) | ((c >= OFF_Q // COL) & (c < OFF_GN // COL))))
    def _():
        project(lambda z: z)

    @pl.when(next_norm)
    def _():
        norm_into(1 - slot)
        project(_column_activation(col0 + ncol - 1))


def _side_cast_specs(arrays, nsteps, step_of):
    flat = [t.reshape(-1, t.shape[-1]) for t in arrays]
    specs = []
    for t in flat:
        rows = BF16_SUBLANES * pl.cdiv(t.shape[0], BF16_SUBLANES * nsteps)
        while t.shape[0] % rows:
            rows += BF16_SUBLANES
        nblk = t.shape[0] // rows
        specs.append(pl.BlockSpec(
            (rows, t.shape[1]), lambda *idx, nblk=nblk: (jnp.minimum(step_of(*idx), nblk - 1), 0)))
    return flat, specs


def _in_proj(x2, norm_g, mod, w_in, layer, *, rows_per_mod, mod_row0, col0, ncol):
    T, D = x2.shape
    tm = min(IN_PROJ_ROWS, T)
    nrow = T // tm
    norm_tile = lambda i, j: jnp.minimum(i + (j == ncol - 1).astype(jnp.int32), nrow - 1)
    mrow = lambda i, j: layer * MOD_ROWS + mod_row0 + norm_tile(i, j) // rows_per_mod

    return pl.pallas_call(
        functools.partial(_in_proj_kernel, col0=col0, ncol=ncol),
        grid=(nrow, ncol),
        in_specs=[
            pl.BlockSpec((tm, D), lambda i, j: (norm_tile(i, j), 0)),
            pl.BlockSpec((1, 1, D), lambda i, j: (layer, 0, 0)),
            pl.BlockSpec((1, 1, D), lambda i, j: (mrow(i, j), 0, 0)),
            pl.BlockSpec((1, 1, D), lambda i, j: (mrow(i, j), 0, 1)),
            pl.BlockSpec((1, D, COL), lambda i, j: (layer, 0, j + col0)),
        ],
        out_specs=pl.BlockSpec((tm, COL), lambda i, j: (i, j)),
        out_shape=jax.ShapeDtypeStruct((T, ncol * COL), Z_DTYPE),
        scratch_shapes=[pltpu.VMEM((2, tm, D), BF16)],
        compiler_params=_params("arbitrary", "arbitrary"),
        name="in_proj",
    )(x2, norm_g, mod, mod, w_in)


def _spatial_gating(u_ref, vg_ref, ga_ref, lng_ref, lnb_ref, ws_ref, bst_ref, o_ref, vn_scr):
    v = vg_ref[...].astype(F32)
    mu = jnp.mean(v, axis=-1, keepdims=True)
    vc = v - mu
    vn = vc * lax.rsqrt(jnp.mean(vc * vc, axis=-1, keepdims=True) + EPS)
    vn_scr[...] = (vn * lng_ref[0] + lnb_ref[0]).astype(BF16)
    nchunk = u_ref.shape[0] // CHUNK
    for g in range(GMLP_GROUPS):
        cols = slice(g * CHUNK, (g + 1) * CHUNK)
        v_all = jnp.concatenate([vn_scr[ch * CHUNK:(ch + 1) * CHUNK, cols] for ch in range(nchunk)], axis=1)
        sv_all = _dot(ws_ref[0, g], v_all) + bst_ref[0, :, g:g + 1]
        for ch in range(nchunk):
            rows = slice(ch * CHUNK, (ch + 1) * CHUNK)
            sv = sv_all[:, ch * CHUNK:(ch + 1) * CHUNK]
            a = u_ref[rows, cols].astype(F32) * sv * ga_ref[rows, cols].astype(F32)
            o_ref[rows, cols] = a.astype(o_ref.dtype)


DFT_RADIX = 4


def _dft_consts(n):
    m = n // DFT_RADIX
    k = np.arange(m, dtype=np.int64)
    ang = 2.0 * np.pi * ((k[:, None] * k[None, :]) % m).astype(np.float64) / m
    cs = np.concatenate([np.cos(ang), np.sin(ang)], axis=0)
    th = 2.0 * np.pi * np.arange(n // 2, dtype=np.float64) / n
    twc = np.broadcast_to(np.cos(th)[:, None], (n // 2, LANES))
    tws = np.broadcast_to(np.sin(th)[:, None], (n // 2, LANES))
    c = np.arange(F_GW, dtype=np.int64)
    angc = 2.0 * np.pi * ((c[:, None] * c[None, :]) % F_GW).astype(np.float64) / F_GW
    bf = lambda t: jnp.asarray(t, dtype=F32).astype(BF16)
    return bf(cs), jnp.asarray(twc, dtype=F32), jnp.asarray(tws, dtype=F32), bf(np.cos(angc)), bf(np.sin(angc))


def _dft_stages(f_ref, gf_ref, cs_ref, twc_ref, tws_ref, cc_ref, sc_ref, o_ref, x_scr):
    n = f_ref.shape[0]
    m0 = n // DFT_RADIX
    lane_tiles = F_GW // LANES

    def position_dft(g):
        for t in range(lane_tiles):
            x_scr[g, t] = f_ref[:, g * F_GW + t * LANES:g * F_GW + (t + 1) * LANES].astype(F32)
        xcat = jnp.concatenate(
            [x_scr[g, t, pl.ds(r, m0, stride=DFT_RADIX), :].astype(BF16)
             for r in range(DFT_RADIX) for t in range(lane_tiles)], axis=1)
        return _dot(cs_ref[...], xcat)

    def butterflies(e):
        base = [(e[:m0, r * F_GW:(r + 1) * F_GW], e[m0:, r * F_GW:(r + 1) * F_GW])
                for r in range(DFT_RADIX)]

        def sub_dft(stride, offset):
            if stride == DFT_RADIX:
                return base[offset]
            ec0, es0 = sub_dft(2 * stride, offset)
            ec1, es1 = sub_dft(2 * stride, offset + stride)
            m = n // (2 * stride)
            c = jnp.concatenate([twc_ref[pl.ds(0, m, stride=stride), :]] * lane_tiles, axis=1)
            s = jnp.concatenate([tws_ref[pl.ds(0, m, stride=stride), :]] * lane_tiles, axis=1)
            a = c * ec1 - s * es1
            b = c * es1 + s * ec1
            return (jnp.concatenate([ec0 + a, ec0 - a], axis=0),
                    jnp.concatenate([es0 + b, es0 - b], axis=0))

        return sub_dft(1, 0)

    def channel_dft(g, tc, ts):
        cols = slice(g * F_GW, (g + 1) * F_GW)
        y = _dot(tc.astype(BF16), cc_ref[...]) - _dot(ts.astype(BF16), sc_ref[...])
        o_ref[:, cols] = (y * ((n * F_GW) ** -0.5) * gf_ref[:, cols].astype(F32)).astype(o_ref.dtype)

    return position_dft, butterflies, channel_dft


def _fourier_kernel(*refs):
    position_dft, butterflies, channel_dft = _dft_stages(*refs)
    nxt = position_dft(0)
    for g in range(F_GROUPS):
        e = nxt
        if g + 1 < F_GROUPS:
            nxt = position_dft(g + 1)
        channel_dft(g, *butterflies(e))


def _fourier(z, n, consts):
    T = z.shape[0]
    const = lambda t: pl.BlockSpec(t.shape, lambda b: (0, 0), pipeline_mode=pl.Buffered(1))
    return pl.pallas_call(
        _fourier_kernel,
        grid=(T // n,),
        in_specs=[
            pl.BlockSpec((n, W_B), lambda b: (b, OFF_F // W_B)),
            pl.BlockSpec((n, W_B), lambda b: (b, OFF_GF // W_B)),
        ] + [const(t) for t in consts],
        out_specs=pl.BlockSpec((n, W_B), lambda b: (b, 0)),
        out_shape=jax.ShapeDtypeStruct((T, W_B), BF16),
        scratch_shapes=[pltpu.VMEM((F_GROUPS, F_GW // LANES, n, LANES), F32)],
        compiler_params=_params("arbitrary"),
        name="fourier",
    )(z, z, *consts)


def _head_rms(t, g):
    t = t.astype(F32)
    return t * lax.rsqrt(jnp.mean(t * t, axis=-1, keepdims=True) + EPS) * g


def _block_geometry(rows):
    assert rows % Q_ROWS == 0 and WIN_ROWS % 2 == 0 and Q_ROWS + NA_KH - 1 <= WIN_ROWS <= rows
    geo, types = [], {}
    for blk in range(rows // Q_ROWS):
        base = min(max(Q_ROWS * blk - NA_KH // 2, 0), rows - WIN_ROWS)
        starts = tuple(_window_start(Q_ROWS * blk + i, rows) - base for i in range(Q_ROWS))
        signature = (base - Q_ROWS * blk, starts)
        geo.append((base, types.setdefault(signature, len(types))))
    return geo


def _window_start(r, rows):
    return min(max(r - NA_KH // 2, 0), rows - NA_KH)


def _build_bias_tables(rpb_ref, tab_scr, head, geo, rows):
    n_dr, n_dc = 2 * NA_KH - 1, 2 * NA_KW - 1
    shape = (GRID_W, 2 * GRID_W)
    lane = lax.broadcasted_iota(jnp.int32, shape, 1)
    cq = lax.broadcasted_iota(jnp.int32, shape, 0)
    kc = lane & (GRID_W - 1)
    dc = kc - cq + (NA_KW - 1)
    cs = jnp.clip(cq - NA_KW // 2, 0, GRID_W - NA_KW)
    col_ok = (kc >= cs) & (kc < cs + NA_KW)
    neg = jnp.full(shape, NEG, F32)
    row_bias = []
    for d in range(n_dr):
        t = neg
        for b in range(n_dc):
            t = jnp.where(dc == b, rpb_ref[(head * n_dr + d) * n_dc + b] * LOG2E, t)
        row_bias.append(jnp.where(col_ok, t, NEG))
    first_block = {typ: blk for blk, (_, typ) in reversed(list(enumerate(geo)))}
    for typ, blk in first_block.items():
        base = geo[blk][0]
        for i in range(Q_ROWS):
            r = Q_ROWS * blk + i
            rs = _window_start(r, rows)
            for wp in range(WIN_ROWS // 2):
                halves = []
                for kr in (base + 2 * wp, base + 2 * wp + 1):
                    halves.append(row_bias[kr - r + NA_KH - 1] if rs <= kr < rs + NA_KH else neg)
                tab_scr[typ, i * GRID_W:(i + 1) * GRID_W, wp * 2 * GRID_W:(wp + 1) * 2 * GRID_W] = (
                    jnp.where(lane < GRID_W, halves[0], halves[1]))


def _nbr_attn_kernel(rpb_ref, q_ref, k_ref, v_ref, gn_ref, kc_ref, vc_ref, qg_ref, kg_ref,
                     f_ref, gf_ref, cs_ref, twc_ref, tws_ref, cc_ref, sc_ref, *rest,
                     geo, rows, layer, n_cast):
    cast_in, (o_ref, fo_ref, *rest) = rest[:n_cast], rest[n_cast:]
    cast_out, (qn_scr, kn_scr, kcn_scr, vx_scr, vcx_scr, tab_ref, x_scr) = rest[:n_cast], rest[n_cast:]
    hd = HEAD_DIM
    position_dft, butterflies, channel_dft = _dft_stages(
        f_ref, gf_ref, cs_ref, twc_ref, tws_ref, cc_ref, sc_ref, fo_ref, x_scr)

    @pl.when(pl.program_id(1) == 0)
    def _():
        for hh in range(HEADS_PER_STEP):
            head = layer * NA_HEADS + pl.program_id(0) * HEADS_PER_STEP + hh
            _build_bias_tables(rpb_ref, tab_ref.at[hh], head, geo, rows)
            vx_scr[hh, :, hd:] = jnp.ones((vx_scr.shape[1], hd), BF16)
            vcx_scr[hh, :, hd:] = jnp.ones((vcx_scr.shape[1], hd), BF16)

    for src, dst in zip(cast_in, cast_out):
        dst[...] = src[...].astype(dst.dtype)

    nq = Q_ROWS * GRID_W
    nk = WIN_ROWS * GRID_W

    def prepare(hh):
        cols = slice(hh * hd, (hh + 1) * hd)
        qn_scr[hh] = _head_rms(q_ref[:, cols], qg_ref[0] * ATT_SCALE).astype(BF16)
        kn_scr[hh] = _head_rms(k_ref[:, cols], kg_ref[0]).astype(BF16)
        kcn_scr[hh] = _head_rms(kc_ref[:, cols], kg_ref[0]).astype(BF16)
        vx_scr[hh, :, :hd] = v_ref[:, cols].astype(BF16)
        vcx_scr[hh, :, :hd] = vc_ref[:, cols].astype(BF16)

    def scores(hh, blk):
        base, typ = geo[blk]
        q = qn_scr[hh, blk * nq:(blk + 1) * nq, :]
        s_loc = _dot_nt(q, kn_scr[hh, base * GRID_W:base * GRID_W + nk, :]) + tab_ref[hh, typ]
        return s_loc, _dot_nt(q, kcn_scr[hh])

    work = [(hh, blk) for hh in range(HEADS_PER_STEP) for blk in range(len(geo))]
    dft = position_dft(0)
    prepare(0)
    pending = [scores(*w) for w in work[:SCORE_LOOKAHEAD]]
    dft = butterflies(dft)
    for n, (hh, blk) in enumerate(work):
        if blk == 0 and hh + 1 < HEADS_PER_STEP:
            prepare(hh + 1)
        if n == 1:
            channel_dft(0, *dft)
        base, typ = geo[blk]
        qs = slice(blk * nq, (blk + 1) * nq)
        ks = slice(base * GRID_W, base * GRID_W + nk)
        cols = slice(hh * hd, (hh + 1) * hd)
        s_loc, s_ctx = pending.pop(0)
        if n + SCORE_LOOKAHEAD < len(work):
            pending.append(scores(*work[n + SCORE_LOOKAHEAD]))
        m = jnp.maximum(jnp.max(s_loc, axis=-1, keepdims=True), jnp.max(s_ctx, axis=-1, keepdims=True))
        p_loc = jnp.exp2(s_loc - m)
        p_ctx = jnp.exp2(s_ctx - m)
        ox = _dot(p_loc.astype(BF16), vx_scr[hh, ks, :]) + _dot(p_ctx.astype(BF16), vcx_scr[hh])
        o_ref[qs, cols] = (ox[:, :hd] / ox[:, hd:] * gn_ref[qs, cols].astype(F32)).astype(o_ref.dtype)


def _nbr_attention(z, zc, ck, cv, qg, kg, rpb_flat, dft_consts, layer, seq, cast=()):
    T = z.shape[0]
    rows = seq // GRID_W
    geo = _block_geometry(rows)
    hd = HEAD_DIM
    hps = HEADS_PER_STEP
    wide = hps * hd
    assert NA_HEADS % hps == 0 and ck % hps == 0 and cv % hps == 0
    assert NA_HEADS // hps == F_GROUPS
    const = lambda t: pl.BlockSpec(t.shape, lambda h, b: (0, 0), pipeline_mode=pl.Buffered(1))
    ntypes = 1 + max(t for _, t in geo)
    nb = T // seq
    cast2d, cast_specs = _side_cast_specs(cast, (NA_HEADS // hps) * nb, lambda h, b: h * nb + b)
    lat = lambda off: pl.BlockSpec((seq, wide), lambda h, b: (b, off // wide + h))
    n, f, *casted = pl.pallas_call(
        functools.partial(_nbr_attn_kernel, geo=geo, rows=rows, layer=layer, n_cast=len(cast)),
        grid=(NA_HEADS // hps, nb),
        in_specs=[
            pl.BlockSpec(memory_space=pltpu.SMEM),
            lat(OFF_Q), lat(OFF_K), lat(OFF_V), lat(OFF_GN),
            pl.BlockSpec((CTX_LEN, wide), lambda h, b: (b, ck // hps + h)),
            pl.BlockSpec((CTX_LEN, wide), lambda h, b: (b, cv // hps + h)),
            pl.BlockSpec((1, 1, hd), lambda h, b: (layer, 0, 0)),
            pl.BlockSpec((1, 1, hd), lambda h, b: (layer, 0, 0)),
            pl.BlockSpec((seq, F_GW), lambda h, b: (b, OFF_F // F_GW + h)),
            pl.BlockSpec((seq, F_GW), lambda h, b: (b, OFF_GF // F_GW + h)),
        ] + [const(t) for t in dft_consts] + cast_specs,
        out_specs=[pl.BlockSpec((seq, wide), lambda h, b: (b, h)),
                   pl.BlockSpec((seq, F_GW), lambda h, b: (b, h))] + cast_specs,
        out_shape=[jax.ShapeDtypeStruct((T, W_C), BF16), jax.ShapeDtypeStruct((T, W_B), BF16)]
        + [jax.ShapeDtypeStruct(t.shape, BF16) for t in cast2d],
        scratch_shapes=[pltpu.VMEM((hps, seq, hd), BF16), pltpu.VMEM((hps, seq, hd), BF16),
                        pltpu.VMEM((hps, CTX_LEN, hd), BF16),
                        pltpu.VMEM((hps, seq, 2 * hd), BF16), pltpu.VMEM((hps, CTX_LEN, 2 * hd), BF16),
                        pltpu.VMEM((hps, ntypes, Q_ROWS * GRID_W, WIN_ROWS * GRID_W), F32),
                        pltpu.VMEM((1, F_GW // LANES, seq, LANES), F32)],
        compiler_params=_params("arbitrary", "arbitrary"),
        name="nbr_attn",
    )(rpb_flat, z, z, z, z, zc, zc, qg, kg, z, z, *dft_consts, *cast2d)
    return (n, f, *(t.reshape(s.shape) for t, s in zip(casted, cast)))


def _ctx_attn_kernel(q_ref, k_ref, v_ref, gn_ref, qg_ref, kg_ref, o_ref):
    for h in range(NA_HEADS):
        cols = slice(h * HEAD_DIM, (h + 1) * HEAD_DIM)
        q = _head_rms(q_ref[:, cols], qg_ref[0] * ATT_SCALE).astype(BF16)
        k = _head_rms(k_ref[:, cols], kg_ref[0]).astype(BF16)
        s = _dot_nt(q, k)
        p = jnp.exp2(s - jnp.max(s, axis=-1, keepdims=True))
        o = _dot(p.astype(BF16), v_ref[:, cols].astype(BF16)) / jnp.sum(p, axis=-1, keepdims=True)
        o_ref[:, cols] = (o * gn_ref[:, cols].astype(F32)).astype(o_ref.dtype)


def _ctx_attention(zc, qg, kg, layer):
    T = zc.shape[0]
    blk = lambda off: pl.BlockSpec((CTX_LEN, W_C), lambda b: (b, off // W_C))
    vec = pl.BlockSpec((1, 1, HEAD_DIM), lambda b: (layer, 0, 0))
    return pl.pallas_call(
        _ctx_attn_kernel,
        grid=(T // CTX_LEN,),
        in_specs=[blk(OFF_Q), blk(OFF_K), blk(OFF_V), blk(OFF_GN), vec, vec],
        out_specs=pl.BlockSpec((CTX_LEN, W_C), lambda b: (b, 0)),
        out_shape=jax.ShapeDtypeStruct((T, W_C), BF16),
        compiler_params=_params("arbitrary"),
        name="ctx_attn",
    )(zc, zc, zc, zc, qg, kg)


def _merge_kernel(u_ref, vg_ref, ga_ref, f_ref, n_ref, ga0, ga1, gf0, gf1, gn0, gn1, x_ref, gate_ref,
                  lng_ref, lnb_ref, ws_ref, bst_ref, wpa_ref, wpf_ref, wpn_ref, wout_ref, o_ref,
                  a_scr, vn_scr, yfn_scr, y_scr):
    f = f_ref[...]
    n = n_ref[...]
    for half, (gf, gn) in enumerate(((gf0, gn0), (gf1, gn1))):
        cols = slice(half * COL, (half + 1) * COL)
        yfn_scr[:, cols] = (gf[...].astype(F32) * _dot(f, wpf_ref[0, :, cols])
                            + gn[...].astype(F32) * _dot(n, wpn_ref[0, :, cols]))
    _spatial_gating(u_ref, vg_ref, ga_ref, lng_ref, lnb_ref, ws_ref, bst_ref, a_scr, vn_scr)
    a = a_scr[...]
    for half, ga in enumerate((ga0, ga1)):
        cols = slice(half * COL, (half + 1) * COL)
        y = yfn_scr[:, cols] + ga[...].astype(F32) * _dot(a, wpa_ref[0, :, cols])
        y_scr[:, cols] = y.astype(BF16)
    o_ref[...] = x_ref[...] + gate_ref[0] * _dot(y_scr[...], wout_ref[0])


def _merge(f, n, z, x2, mod, lng, lnb, ws, bst, wpa, wpf, wpn, wout, layer, *, rows_per_mod, mod_row0):
    T, D = x2.shape
    tm = MERGE_ROWS
    mrow = lambda i: layer * MOD_ROWS + mod_row0 + i // rows_per_mod
    act = pl.BlockSpec((tm, COL), lambda i: (i, 0))
    zblk = lambda c: pl.BlockSpec((tm, COL), lambda i: (i, c))
    const = lambda *shape: pl.BlockSpec((1,) + shape, lambda i: (layer,) + (0,) * len(shape),
                                        pipeline_mode=pl.Buffered(1))
    return pl.pallas_call(
        _merge_kernel,
        grid=(T // tm,),
        in_specs=[
            zblk(OFF_U // COL), zblk(OFF_VG // COL), zblk(OFF_GA // COL),
            act, act,
        ] + [zblk(OFF_MERGE // COL + c) for c in range(3 * D // COL)] + [
            pl.BlockSpec((tm, D), lambda i: (i, 0)),
            pl.BlockSpec((1, 1, D), lambda i: (mrow(i), 0, 2)),
            const(1, W_A), const(1, W_A), const(GMLP_GROUPS, CHUNK, CHUNK), const(CHUNK, GMLP_GROUPS),
            const(W_A, D), const(W_B, D), const(W_C, D), const(D, D),
        ],
        out_specs=pl.BlockSpec((tm, D), lambda i: (i, 0)),
        out_shape=jax.ShapeDtypeStruct((T, D), F32),
        scratch_shapes=[pltpu.VMEM((tm, W_A), BF16), pltpu.VMEM((tm, W_A), BF16),
                        pltpu.VMEM((tm, D), F32), pltpu.VMEM((tm, D), BF16)],
        compiler_params=_params("arbitrary"),
        name="merge_out",
    )(z, z, z, f, n, z, z, z, z, z, z, x2, mod, lng, lnb, ws, bst, wpa, wpf, wpn, wout)


def kernel(x, c, ctx, c_ctx, norm_g, w_ada, b_ada, w_in, gmlp_ln_g, gmlp_ln_b, gmlp_ws, gmlp_bs,
           q_norm_g, k_norm_g, rpb, w_pa, w_pf, w_pn, w_out):
    B, S, D = x.shape
    C = ctx.shape[1]
    L = w_in.shape[0]
    assert D == D_MODEL and C == CTX_LEN and B < MOD_ROWS
    assert S % (GRID_W * Q_ROWS) == 0 and S // GRID_W >= WIN_ROWS

    cc = jnp.concatenate([c, c_ctx[None], jnp.zeros((MOD_ROWS - B - 1, D), F32)], axis=0)
    mod = _modulation(cc, w_ada, b_ada).reshape(L * MOD_ROWS, 1, 3 * D)

    ws = gmlp_ws.astype(BF16)
    bst = jnp.swapaxes(gmlp_bs, 1, 2)
    rpb_flat = rpb.reshape(-1)
    norm_g, gmlp_ln_g, gmlp_ln_b, q_norm_g, k_norm_g = (
        t[:, None, :] for t in (norm_g, gmlp_ln_g, gmlp_ln_b, q_norm_g, k_norm_g))
    dft_lat = _dft_consts(S)
    dft_ctx = _dft_consts(C)
    x2 = x.reshape(B * S, D)
    c2 = ctx.reshape(B * C, D)
    for l in range(L):
        last = l == L - 1
        if last:
            zc = _in_proj(c2, norm_g, mod, w_in, l, rows_per_mod=ONE_ROW, mod_row0=B,
                          col0=OFF_K // COL, ncol=2)
            ck, cv = 0, W_C // HEAD_DIM
        else:
            zc = _in_proj(c2, norm_g, mod, w_in, l, rows_per_mod=ONE_ROW, mod_row0=B, col0=0,
                          ncol=W_IN // COL)
            ck, cv = OFF_K // HEAD_DIM, OFF_V // HEAD_DIM
        z = _in_proj(x2, norm_g, mod, w_in, l, rows_per_mod=S // IN_PROJ_ROWS, mod_row0=0, col0=0,
                     ncol=W_IN // COL)

        n, f, *cast = _nbr_attention(z, zc, ck, cv, q_norm_g, k_norm_g, rpb_flat, dft_lat, l, S,
                                     cast=(w_pa, w_pf, w_pn, w_out) if l == 0 else ())
        if l == 0:
            wpa, wpf, wpn, wout = cast
        x_new = _merge(f, n, z, x2, mod, gmlp_ln_g, gmlp_ln_b, ws, bst, wpa, wpf, wpn, wout, l,
                       rows_per_mod=S // MERGE_ROWS, mod_row0=0)

        if not last:
            nc = _ctx_attention(zc, q_norm_g, k_norm_g, l)
            fc = _fourier(zc, C, dft_ctx)
            c2 = _merge(fc, nc, zc, c2, mod, gmlp_ln_g, gmlp_ln_b, ws, bst, wpa, wpf, wpn, wout, l,
                        rows_per_mod=ONE_ROW, mod_row0=B)
        x2 = x_new
    return x2.reshape(B, S, D)
```

```python
import functools

import jax
import jax.numpy as jnp
import numpy as np
from jax import lax
from jax.experimental import pallas as pl
from jax.experimental.pallas import tpu as pltpu

F32 = jnp.float32
BF16 = jnp.bfloat16

D_MODEL = 2048
CTX_LEN = 256
GRID_W = 64
EPS = 1e-6
CHUNK = 128
GMLP_GROUPS = 8
W_A = D_MODEL // 2
W_B = D_MODEL // 2
F_GROUPS = 4
F_GW = W_B // F_GROUPS
HEAD_DIM = 128
NA_HEADS = 8
W_C = NA_HEADS * HEAD_DIM
NA_KH = 8
NA_KW = 16

OFF_U = 0
OFF_VG = OFF_U + W_A
OFF_GA = OFF_VG + W_A
OFF_F = OFF_GA + W_A
OFF_GF = OFF_F + W_B
OFF_Q = OFF_GF + W_B
OFF_K = OFF_Q + W_C
OFF_V = OFF_K + W_C
OFF_GN = OFF_V + W_C
OFF_MERGE = OFF_GN + W_C
W_IN = OFF_MERGE + 3 * D_MODEL

LANES = 128
BF16_SUBLANES = 16
IN_PROJ_ROWS = 1024
MERGE_ROWS = 256
COL = 1024
MOD_ROWS = 16
ONE_ROW = 1 << 30
VMEM_LIMIT = 56 * 1024 * 1024
Z_DTYPE = jnp.bfloat16

Q_ROWS = 2
WIN_ROWS = 10
SCORE_LOOKAHEAD = 3
HEADS_PER_STEP = 2
NEG = -1e30
LOG2E = float(np.log2(np.e))
ATT_SCALE = HEAD_DIM ** -0.5 * LOG2E


GELU_C1 = float(np.sqrt(2.0 / np.pi))
GELU_C2 = GELU_C1 * 0.044715


def _sigmoid(t):
    return 0.5 * jnp.tanh(0.5 * t) + 0.5


def _silu(t):
    h = 0.5 * t
    return h * jnp.tanh(h) + h


def _gelu(t):
    h = 0.5 * t
    return h * jnp.tanh(t * (GELU_C1 + GELU_C2 * (t * t))) + h


def _dot(a, b):
    return jnp.dot(a, b, preferred_element_type=F32)


def _dot_nt(a, b):
    return lax.dot_general(a, b, (((1,), (1,)), ((), ())), preferred_element_type=F32)


def _params(*sem):
    return pltpu.CompilerParams(dimension_semantics=sem, vmem_limit_bytes=VMEM_LIMIT)


def _mod_kernel(cc_ref, w_ref, b_ref, o_ref):
    s = _silu(cc_ref[...])
    s_hi = s.astype(BF16)
    s_lo = (s - s_hi.astype(F32)).astype(BF16)
    w = w_ref[0]
    w_hi = w.astype(BF16)
    w_lo = (w - w_hi.astype(F32)).astype(BF16)
    o_ref[0] = _dot(s_hi, w_hi) + _dot(s_lo, w_hi) + _dot(s_hi, w_lo) + b_ref[0]


def _modulation(cc, w_ada, b_ada):
    L, D, N = w_ada.shape
    tn = COL
    return pl.pallas_call(
        _mod_kernel,
        grid=(L, N // tn),
        in_specs=[
            pl.BlockSpec((MOD_ROWS, D), lambda l, j: (0, 0)),
            pl.BlockSpec((1, D, tn), lambda l, j: (l, 0, j)),
            pl.BlockSpec((1, 1, tn), lambda l, j: (l, 0, j)),
        ],
        out_specs=pl.BlockSpec((1, MOD_ROWS, tn), lambda l, j: (l, 0, j)),
        out_shape=jax.ShapeDtypeStruct((L, MOD_ROWS, N), F32),
        compiler_params=_params("arbitrary", "arbitrary"),
        name="adaln_mod",
    )(cc, w_ada, b_ada.reshape(L, 1, N))


def _column_activation(c):
    if c < OFF_GA // COL:
        return _gelu
    if c in (OFF_GA // COL, OFF_GF // COL, OFF_GN // COL):
        return _silu
    if c >= OFF_MERGE // COL:
        return _sigmoid
    return lambda z: z


def _in_proj_kernel(x_ref, g_ref, sh_ref, sc_ref, w_ref, o_ref, h_scr, *, col0, ncol):
    i = pl.program_id(0)
    j = pl.program_id(1)
    slot = i % 2
    next_norm = (j == ncol - 1) & (i + 1 < pl.num_programs(0))

    def norm_into(dst):
        x = x_ref[...]
        gain = g_ref[0] * (1.0 + sc_ref[0])
        r = lax.rsqrt(jnp.mean(x * x, axis=-1, keepdims=True) + EPS)
        h_scr[dst] = (x * r * gain + sh_ref[0]).astype(BF16)

    @pl.when((i == 0) & (j == 0))
    def _():
        norm_into(0)

    c = j + col0
    plain = jnp.logical_not(next_norm)

    def project(act):
        o_ref[...] = act(_dot(h_scr[slot], w_ref[0].astype(BF16))).astype(o_ref.dtype)

    @pl.when(plain & (c < OFF_GA // COL))
    def _():
        project(_gelu)

    @pl.when(plain & ((c == OFF_GA // COL) | (c == OFF_GF // COL) | (c == OFF_GN // COL)))
    def _():
        project(_silu)

    @pl.when(plain & (c >= OFF_MERGE // COL))
    def _():
        project(_sigmoid)

    @pl.when(plain & ((c == OFF_F // COL) | ((c >= OFF_Q // COL) & (c < OFF_GN // COL))))
    def _():
        project(lambda z: z)

    @pl.when(next_norm)
    def _():
        norm_into(1 - slot)
        project(_column_activation(col0 + ncol - 1))


def _side_cast_specs(arrays, nsteps, step_of):
    flat = [t.reshape(-1, t.shape[-1]) for t in arrays]
    specs = []
    for t in flat:
        rows = BF16_SUBLANES * pl.cdiv(t.shape[0], BF16_SUBLANES * nsteps)
        while t.shape[0] % rows:
            rows += BF16_SUBLANES
        nblk = t.shape[0] // rows
        specs.append(pl.BlockSpec(
            (rows, t.shape[1]), lambda *idx, nblk=nblk: (jnp.minimum(step_of(*idx), nblk - 1), 0)))
    return flat, specs


def _in_proj(x2, norm_g, mod, w_in, layer, *, rows_per_mod, mod_row0, col0, ncol):
    T, D = x2.shape
    tm = min(IN_PROJ_ROWS, T)
    nrow = T // tm
    norm_tile = lambda i, j: jnp.minimum(i + (j == ncol - 1).astype(jnp.int32), nrow - 1)
    mrow = lambda i, j: layer * MOD_ROWS + mod_row0 + norm_tile(i, j) // rows_per_mod

    return pl.pallas_call(
        functools.partial(_in_proj_kernel, col0=col0, ncol=ncol),
        grid=(nrow, ncol),
        in_specs=[
            pl.BlockSpec((tm, D), lambda i, j: (norm_tile(i, j), 0)),
            pl.BlockSpec((1, 1, D), lambda i, j: (layer, 0, 0)),
            pl.BlockSpec((1, 1, D), lambda i, j: (mrow(i, j), 0, 0)),
            pl.BlockSpec((1, 1, D), lambda i, j: (mrow(i, j), 0, 1)),
            pl.BlockSpec((1, D, COL), lambda i, j: (layer, 0, j + col0)),
        ],
        out_specs=pl.BlockSpec((tm, COL), lambda i, j: (i, j)),
        out_shape=jax.ShapeDtypeStruct((T, ncol * COL), Z_DTYPE),
        scratch_shapes=[pltpu.VMEM((2, tm, D), BF16)],
        compiler_params=_params("arbitrary", "arbitrary"),
        name="in_proj",
    )(x2, norm_g, mod, mod, w_in)


def _spatial_gating(u_ref, vg_ref, ga_ref, lng_ref, lnb_ref, ws_ref, bst_ref, o_ref, vn_scr):
    v = vg_ref[...].astype(F32)
    mu = jnp.mean(v, axis=-1, keepdims=True)
    vc = v - mu
    vn = vc * lax.rsqrt(jnp.mean(vc * vc, axis=-1, keepdims=True) + EPS)
    vn_scr[...] = (vn * lng_ref[0] + lnb_ref[0]).astype(BF16)
    nchunk = u_ref.shape[0] // CHUNK
    for g in range(GMLP_GROUPS):
        cols = slice(g * CHUNK, (g + 1) * CHUNK)
        v_all = jnp.concatenate([vn_scr[ch * CHUNK:(ch + 1) * CHUNK, cols] for ch in range(nchunk)], axis=1)
        sv_all = _dot(ws_ref[0, g], v_all) + bst_ref[0, :, g:g + 1]
        for ch in range(nchunk):
            rows = slice(ch * CHUNK, (ch + 1) * CHUNK)
            sv = sv_all[:, ch * CHUNK:(ch + 1) * CHUNK]
            a = u_ref[rows, cols].astype(F32) * sv * ga_ref[rows, cols].astype(F32)
            o_ref[rows, cols] = a.astype(o_ref.dtype)


DFT_RADIX = 4


def _dft_consts(n):
    m = n // DFT_RADIX
    k = np.arange(m, dtype=np.int64)
    ang = 2.0 * np.pi * ((k[:, None] * k[None, :]) % m).astype(np.float64) / m
    cs = np.concatenate([np.cos(ang), np.sin(ang)], axis=0)
    th = 2.0 * np.pi * np.arange(n // 2, dtype=np.float64) / n
    twc = np.broadcast_to(np.cos(th)[:, None], (n // 2, LANES))
    tws = np.broadcast_to(np.sin(th)[:, None], (n // 2, LANES))
    c = np.arange(F_GW, dtype=np.int64)
    angc = 2.0 * np.pi * ((c[:, None] * c[None, :]) % F_GW).astype(np.float64) / F_GW
    bf = lambda t: jnp.asarray(t, dtype=F32).astype(BF16)
    return bf(cs), jnp.asarray(twc, dtype=F32), jnp.asarray(tws, dtype=F32), bf(np.cos(angc)), bf(np.sin(angc))


def _dft_stages(f_ref, gf_ref, cs_ref, twc_ref, tws_ref, cc_ref, sc_ref, o_ref, x_scr):
    n = f_ref.shape[0]
    m0 = n // DFT_RADIX
    lane_tiles = F_GW // LANES

    def position_dft(g):
        for t in range(lane_tiles):
            x_scr[g, t] = f_ref[:, g * F_GW + t * LANES:g * F_GW + (t + 1) * LANES].astype(F32)
        xcat = jnp.concatenate(
            [x_scr[g, t, pl.ds(r, m0, stride=DFT_RADIX), :].astype(BF16)
             for r in range(DFT_RADIX) for t in range(lane_tiles)], axis=1)
        return _dot(cs_ref[...], xcat)

    def butterflies(e):
        base = [(e[:m0, r * F_GW:(r + 1) * F_GW], e[m0:, r * F_GW:(r + 1) * F_GW])
                for r in range(DFT_RADIX)]

        def sub_dft(stride, offset):
            if stride == DFT_RADIX:
                return base[offset]
            ec0, es0 = sub_dft(2 * stride, offset)
            ec1, es1 = sub_dft(2 * stride, offset + stride)
            m = n // (2 * stride)
            c = jnp.concatenate([twc_ref[pl.ds(0, m, stride=stride), :]] * lane_tiles, axis=1)
            s = jnp.concatenate([tws_ref[pl.ds(0, m, stride=stride), :]] * lane_tiles, axis=1)
            a = c * ec1 - s * es1
            b = c * es1 + s * ec1
            return (jnp.concatenate([ec0 + a, ec0 - a], axis=0),
                    jnp.concatenate([es0 + b, es0 - b], axis=0))

        return sub_dft(1, 0)

    def channel_dft(g, tc, ts):
        cols = slice(g * F_GW, (g + 1) * F_GW)
        y = _dot(tc.astype(BF16), cc_ref[...]) - _dot(ts.astype(BF16), sc_ref[...])
        o_ref[:, cols] = (y * ((n * F_GW) ** -0.5) * gf_ref[:, cols].astype(F32)).astype(o_ref.dtype)

    return position_dft, butterflies, channel_dft


def _fourier_kernel(*refs):
    position_dft, butterflies, channel_dft = _dft_stages(*refs)
    nxt = position_dft(0)
    for g in range(F_GROUPS):
        e = nxt
        if g + 1 < F_GROUPS:
            nxt = position_dft(g + 1)
        channel_dft(g, *butterflies(e))


def _fourier(z, n, consts):
    T = z.shape[0]
    const = lambda t: pl.BlockSpec(t.shape, lambda b: (0, 0), pipeline_mode=pl.Buffered(1))
    return pl.pallas_call(
        _fourier_kernel,
        grid=(T // n,),
        in_specs=[
            pl.BlockSpec((n, W_B), lambda b: (b, OFF_F // W_B)),
            pl.BlockSpec((n, W_B), lambda b: (b, OFF_GF // W_B)),
        ] + [const(t) for t in consts],
        out_specs=pl.BlockSpec((n, W_B), lambda b: (b, 0)),
        out_shape=jax.ShapeDtypeStruct((T, W_B), BF16),
        scratch_shapes=[pltpu.VMEM((F_GROUPS, F_GW // LANES, n, LANES), F32)],
        compiler_params=_params("arbitrary"),
        name="fourier",
    )(z, z, *consts)


def _head_rms(t, g):
    t = t.astype(F32)
    return t * lax.rsqrt(jnp.mean(t * t, axis=-1, keepdims=True) + EPS) * g


def _block_geometry(rows):
    assert rows % Q_ROWS == 0 and WIN_ROWS % 2 == 0 and Q_ROWS + NA_KH - 1 <= WIN_ROWS <= rows
    geo, types = [], {}
    for blk in range(rows // Q_ROWS):
        base = min(max(Q_ROWS * blk - NA_KH // 2, 0), rows - WIN_ROWS)
        starts = tuple(_window_start(Q_ROWS * blk + i, rows) - base for i in range(Q_ROWS))
        signature = (base - Q_ROWS * blk, starts)
        geo.append((base, types.setdefault(signature, len(types))))
    return geo


def _window_start(r, rows):
    return min(max(r - NA_KH // 2, 0), rows - NA_KH)


def _build_bias_tables(rpb_ref, tab_scr, head, geo, rows):
    n_dr, n_dc = 2 * NA_KH - 1, 2 * NA_KW - 1
    shape = (GRID_W, 2 * GRID_W)
    lane = lax.broadcasted_iota(jnp.int32, shape, 1)
    cq = lax.broadcasted_iota(jnp.int32, shape, 0)
    kc = lane & (GRID_W - 1)
    dc = kc - cq + (NA_KW - 1)
    cs = jnp.clip(cq - NA_KW // 2, 0, GRID_W - NA_KW)
    col_ok = (kc >= cs) & (kc < cs + NA_KW)
    neg = jnp.full(shape, NEG, F32)
    lane_row = lax.broadcasted_iota(jnp.int32, (1, 2 * GRID_W), 1)
    dc_safe = jnp.clip(dc, 0, 2 * GRID_W - 1)
    row_bias = []
    for d in range(n_dr):
        row = jnp.zeros((1, 2 * GRID_W), F32)
        for b in range(n_dc):
            row = jnp.where(lane_row == b, rpb_ref[(head * n_dr + d) * n_dc + b] * LOG2E, row)
        t = jnp.take_along_axis(jnp.broadcast_to(row, shape), dc_safe, axis=1)
        row_bias.append(jnp.where(col_ok, t, NEG))
    first_block = {typ: blk for blk, (_, typ) in reversed(list(enumerate(geo)))}
    for typ, blk in first_block.items():
        base = geo[blk][0]
        for i in range(Q_ROWS):
            r = Q_ROWS * blk + i
            rs = _window_start(r, rows)
            for wp in range(WIN_ROWS // 2):
                halves = []
                for kr in (base + 2 * wp, base + 2 * wp + 1):
                    halves.append(row_bias[kr - r + NA_KH - 1] if rs <= kr < rs + NA_KH else neg)
                tab_scr[typ, i * GRID_W:(i + 1) * GRID_W, wp * 2 * GRID_W:(wp + 1) * 2 * GRID_W] = (
                    jnp.where(lane < GRID_W, halves[0], halves[1]))


def _nbr_attn_kernel(rpb_ref, q_ref, k_ref, v_ref, gn_ref, kc_ref, vc_ref, qg_ref, kg_ref,
                     f_ref, gf_ref, cs_ref, twc_ref, tws_ref, cc_ref, sc_ref, *rest,
                     geo, rows, layer, n_cast):
    cast_in, (o_ref, fo_ref, *rest) = rest[:n_cast], rest[n_cast:]
    cast_out, (qn_scr, kn_scr, kcn_scr, vx_scr, vcx_scr, tab_ref, x_scr) = rest[:n_cast], rest[n_cast:]
    hd = HEAD_DIM
    position_dft, butterflies, channel_dft = _dft_stages(
        f_ref, gf_ref, cs_ref, twc_ref, tws_ref, cc_ref, sc_ref, fo_ref, x_scr)

    @pl.when(pl.program_id(1) == 0)
    def _():
        for hh in range(HEADS_PER_STEP):
            head = layer * NA_HEADS + pl.program_id(0) * HEADS_PER_STEP + hh
            _build_bias_tables(rpb_ref, tab_ref.at[hh], head, geo, rows)
            vx_scr[hh, :, hd:] = jnp.ones((vx_scr.shape[1], hd), BF16)
            vcx_scr[hh, :, hd:] = jnp.ones((vcx_scr.shape[1], hd), BF16)

    for src, dst in zip(cast_in, cast_out):
        dst[...] = src[...].astype(dst.dtype)

    nq = Q_ROWS * GRID_W
    nk = WIN_ROWS * GRID_W

    def prepare(hh):
        cols = slice(hh * hd, (hh + 1) * hd)
        qn_scr[hh] = _head_rms(q_ref[:, cols], qg_ref[0] * ATT_SCALE).astype(BF16)
        kn_scr[hh] = _head_rms(k_ref[:, cols], kg_ref[0]).astype(BF16)
        kcn_scr[hh] = _head_rms(kc_ref[:, cols], kg_ref[0]).astype(BF16)
        vx_scr[hh, :, :hd] = v_ref[:, cols].astype(BF16)
        vcx_scr[hh, :, :hd] = vc_ref[:, cols].astype(BF16)

    def scores(hh, blk):
        base, typ = geo[blk]
        q = qn_scr[hh, blk * nq:(blk + 1) * nq, :]
        s_loc = _dot_nt(q, kn_scr[hh, base * GRID_W:base * GRID_W + nk, :]) + tab_ref[hh, typ]
        return s_loc, _dot_nt(q, kcn_scr[hh])

    work = [(hh, blk) for hh in range(HEADS_PER_STEP) for blk in range(len(geo))]
    dft = position_dft(0)
    prepare(0)
    pending = [scores(*w) for w in work[:SCORE_LOOKAHEAD]]
    dft = butterflies(dft)
    for n, (hh, blk) in enumerate(work):
        if blk == 0 and hh + 1 < HEADS_PER_STEP:
            prepare(hh + 1)
        if n == 1:
            channel_dft(0, *dft)
        base, typ = geo[blk]
        qs = slice(blk * nq, (blk + 1) * nq)
        ks = slice(base * GRID_W, base * GRID_W + nk)
        cols = slice(hh * hd, (hh + 1) * hd)
        s_loc, s_ctx = pending.pop(0)
        if n + SCORE_LOOKAHEAD < len(work):
            pending.append(scores(*work[n + SCORE_LOOKAHEAD]))
        m = jnp.maximum(jnp.max(s_loc, axis=-1, keepdims=True), jnp.max(s_ctx, axis=-1, keepdims=True))
        p_loc = jnp.exp2(s_loc - m)
        p_ctx = jnp.exp2(s_ctx - m)
        ox = _dot(p_loc.astype(BF16), vx_scr[hh, ks, :]) + _dot(p_ctx.astype(BF16), vcx_scr[hh])
        o_ref[qs, cols] = (ox[:, :hd] / ox[:, hd:] * gn_ref[qs, cols].astype(F32)).astype(o_ref.dtype)


def _nbr_attention(z, zc, ck, cv, qg, kg, rpb_flat, dft_consts, layer, seq, cast=()):
    T = z.shape[0]
    rows = seq // GRID_W
    geo = _block_geometry(rows)
    hd = HEAD_DIM
    hps = HEADS_PER_STEP
    wide = hps * hd
    assert NA_HEADS % hps == 0 and ck % hps == 0 and cv % hps == 0
    assert NA_HEADS // hps == F_GROUPS
    const = lambda t: pl.BlockSpec(t.shape, lambda h, b: (0, 0), pipeline_mode=pl.Buffered(1))
    ntypes = 1 + max(t for _, t in geo)
    nb = T // seq
    cast2d, cast_specs = _side_cast_specs(cast, (NA_HEADS // hps) * nb, lambda h, b: h * nb + b)
    lat = lambda off: pl.BlockSpec((seq, wide), lambda h, b: (b, off // wide + h))
    n, f, *casted = pl.pallas_call(
        functools.partial(_nbr_attn_kernel, geo=geo, rows=rows, layer=layer, n_cast=len(cast)),
        grid=(NA_HEADS // hps, nb),
        in_specs=[
            pl.BlockSpec(memory_space=pltpu.SMEM),
            lat(OFF_Q), lat(OFF_K), lat(OFF_V), lat(OFF_GN),
            pl.BlockSpec((CTX_LEN, wide), lambda h, b: (b, ck // hps + h)),
            pl.BlockSpec((CTX_LEN, wide), lambda h, b: (b, cv // hps + h)),
            pl.BlockSpec((1, 1, hd), lambda h, b: (layer, 0, 0)),
            pl.BlockSpec((1, 1, hd), lambda h, b: (layer, 0, 0)),
            pl.BlockSpec((seq, F_GW), lambda h, b: (b, OFF_F // F_GW + h)),
            pl.BlockSpec((seq, F_GW), lambda h, b: (b, OFF_GF // F_GW + h)),
        ] + [const(t) for t in dft_consts] + cast_specs,
        out_specs=[pl.BlockSpec((seq, wide), lambda h, b: (b, h)),
                   pl.BlockSpec((seq, F_GW), lambda h, b: (b, h))] + cast_specs,
        out_shape=[jax.ShapeDtypeStruct((T, W_C), BF16), jax.ShapeDtypeStruct((T, W_B), BF16)]
        + [jax.ShapeDtypeStruct(t.shape, BF16) for t in cast2d],
        scratch_shapes=[pltpu.VMEM((hps, seq, hd), BF16), pltpu.VMEM((hps, seq, hd), BF16),
                        pltpu.VMEM((hps, CTX_LEN, hd), BF16),
                        pltpu.VMEM((hps, seq, 2 * hd), BF16), pltpu.VMEM((hps, CTX_LEN, 2 * hd), BF16),
                        pltpu.VMEM((hps, ntypes, Q_ROWS * GRID_W, WIN_ROWS * GRID_W), F32),
                        pltpu.VMEM((1, F_GW // LANES, seq, LANES), F32)],
        compiler_params=_params("arbitrary", "arbitrary"),
        name="nbr_attn",
    )(rpb_flat, z, z, z, z, zc, zc, qg, kg, z, z, *dft_consts, *cast2d)
    return (n, f, *(t.reshape(s.shape) for t, s in zip(casted, cast)))


def _ctx_attn_kernel(q_ref, k_ref, v_ref, gn_ref, qg_ref, kg_ref, o_ref):
    for h in range(NA_HEADS):
        cols = slice(h * HEAD_DIM, (h + 1) * HEAD_DIM)
        q = _head_rms(q_ref[:, cols], qg_ref[0] * ATT_SCALE).astype(BF16)
        k = _head_rms(k_ref[:, cols], kg_ref[0]).astype(BF16)
        s = _dot_nt(q, k)
        p = jnp.exp2(s - jnp.max(s, axis=-1, keepdims=True))
        o = _dot(p.astype(BF16), v_ref[:, cols].astype(BF16)) / jnp.sum(p, axis=-1, keepdims=True)
        o_ref[:, cols] = (o * gn_ref[:, cols].astype(F32)).astype(o_ref.dtype)


def _ctx_attention(zc, qg, kg, layer):
    T = zc.shape[0]
    blk = lambda off: pl.BlockSpec((CTX_LEN, W_C), lambda b: (b, off // W_C))
    vec = pl.BlockSpec((1, 1, HEAD_DIM), lambda b: (layer, 0, 0))
    return pl.pallas_call(
        _ctx_attn_kernel,
        grid=(T // CTX_LEN,),
        in_specs=[blk(OFF_Q), blk(OFF_K), blk(OFF_V), blk(OFF_GN), vec, vec],
        out_specs=pl.BlockSpec((CTX_LEN, W_C), lambda b: (b, 0)),
        out_shape=jax.ShapeDtypeStruct((T, W_C), BF16),
        compiler_params=_params("arbitrary"),
        name="ctx_attn",
    )(zc, zc, zc, zc, qg, kg)


def _merge_kernel(u_ref, vg_ref, ga_ref, f_ref, n_ref, ga0, ga1, gf0, gf1, gn0, gn1, x_ref, gate_ref,
                  lng_ref, lnb_ref, ws_ref, bst_ref, wpa_ref, wpf_ref, wpn_ref, wout_ref, o_ref,
                  a_scr, vn_scr, yfn_scr, y_scr):
    f = f_ref[...]
    n = n_ref[...]
    for half, (gf, gn) in enumerate(((gf0, gn0), (gf1, gn1))):
        cols = slice(half * COL, (half + 1) * COL)
        yfn_scr[:, cols] = (gf[...].astype(F32) * _dot(f, wpf_ref[0, :, cols])
                            + gn[...].astype(F32) * _dot(n, wpn_ref[0, :, cols]))
    _spatial_gating(u_ref, vg_ref, ga_ref, lng_ref, lnb_ref, ws_ref, bst_ref, a_scr, vn_scr)
    a = a_scr[...]
    for half, ga in enumerate((ga0, ga1)):
        cols = slice(half * COL, (half + 1) * COL)
        y = yfn_scr[:, cols] + ga[...].astype(F32) * _dot(a, wpa_ref[0, :, cols])
        y_scr[:, cols] = y.astype(BF16)
    o_ref[...] = x_ref[...] + gate_ref[0] * _dot(y_scr[...], wout_ref[0])


def _merge(f, n, z, x2, mod, lng, lnb, ws, bst, wpa, wpf, wpn, wout, layer, *, rows_per_mod, mod_row0):
    T, D = x2.shape
    tm = MERGE_ROWS
    mrow = lambda i: layer * MOD_ROWS + mod_row0 + i // rows_per_mod
    act = pl.BlockSpec((tm, COL), lambda i: (i, 0))
    zblk = lambda c: pl.BlockSpec((tm, COL), lambda i: (i, c))
    const = lambda *shape: pl.BlockSpec((1,) + shape, lambda i: (layer,) + (0,) * len(shape),
                                        pipeline_mode=pl.Buffered(1))
    return pl.pallas_call(
        _merge_kernel,
        grid=(T // tm,),
        in_specs=[
            zblk(OFF_U // COL), zblk(OFF_VG // COL), zblk(OFF_GA // COL),
            act, act,
        ] + [zblk(OFF_MERGE // COL + c) for c in range(3 * D // COL)] + [
            pl.BlockSpec((tm, D), lambda i: (i, 0)),
            pl.BlockSpec((1, 1, D), lambda i: (mrow(i), 0, 2)),
            const(1, W_A), const(1, W_A), const(GMLP_GROUPS, CHUNK, CHUNK), const(CHUNK, GMLP_GROUPS),
            const(W_A, D), const(W_B, D), const(W_C, D), const(D, D),
        ],
        out_specs=pl.BlockSpec((tm, D), lambda i: (i, 0)),
        out_shape=jax.ShapeDtypeStruct((T, D), F32),
        scratch_shapes=[pltpu.VMEM((tm, W_A), BF16), pltpu.VMEM((tm, W_A), BF16),
                        pltpu.VMEM((tm, D), F32), pltpu.VMEM((tm, D), BF16)],
        compiler_params=_params("arbitrary"),
        name="merge_out",
    )(z, z, z, f, n, z, z, z, z, z, z, x2, mod, lng, lnb, ws, bst, wpa, wpf, wpn, wout)


def kernel(x, c, ctx, c_ctx, norm_g, w_ada, b_ada, w_in, gmlp_ln_g, gmlp_ln_b, gmlp_ws, gmlp_bs,
           q_norm_g, k_norm_g, rpb, w_pa, w_pf, w_pn, w_out):
    B, S, D = x.shape
    C = ctx.shape[1]
    L = w_in.shape[0]
    assert D == D_MODEL and C == CTX_LEN and B < MOD_ROWS
    assert S % (GRID_W * Q_ROWS) == 0 and S // GRID_W >= WIN_ROWS

    cc = jnp.concatenate([c, c_ctx[None], jnp.zeros((MOD_ROWS - B - 1, D), F32)], axis=0)
    mod = _modulation(cc, w_ada, b_ada).reshape(L * MOD_ROWS, 1, 3 * D)

    ws = gmlp_ws.astype(BF16)
    bst = jnp.swapaxes(gmlp_bs, 1, 2)
    rpb_flat = rpb.reshape(-1)
    norm_g, gmlp_ln_g, gmlp_ln_b, q_norm_g, k_norm_g = (
        t[:, None, :] for t in (norm_g, gmlp_ln_g, gmlp_ln_b, q_norm_g, k_norm_g))
    dft_lat = _dft_consts(S)
    dft_ctx = _dft_consts(C)
    x2 = x.reshape(B * S, D)
    c2 = ctx.reshape(B * C, D)
    for l in range(L):
        last = l == L - 1
        if last:
            zc = _in_proj(c2, norm_g, mod, w_in, l, rows_per_mod=ONE_ROW, mod_row0=B,
                          col0=OFF_K // COL, ncol=2)
            ck, cv = 0, W_C // HEAD_DIM
        else:
            zc = _in_proj(c2, norm_g, mod, w_in, l, rows_per_mod=ONE_ROW, mod_row0=B, col0=0,
                          ncol=W_IN // COL)
            ck, cv = OFF_K // HEAD_DIM, OFF_V // HEAD_DIM
        z = _in_proj(x2, norm_g, mod, w_in, l, rows_per_mod=S // IN_PROJ_ROWS, mod_row0=0, col0=0,
                     ncol=W_IN // COL)

        n, f, *cast = _nbr_attention(z, zc, ck, cv, q_norm_g, k_norm_g, rpb_flat, dft_lat, l, S,
                                     cast=(w_pa, w_pf, w_pn, w_out) if l == 0 else ())
        if l == 0:
            wpa, wpf, wpn, wout = cast
        x_new = _merge(f, n, z, x2, mod, gmlp_ln_g, gmlp_ln_b, ws, bst, wpa, wpf, wpn, wout, l,
                       rows_per_mod=S // MERGE_ROWS, mod_row0=0)

        if not last:
            nc = _ctx_attention(zc, q_norm_g, k_norm_g, l)
            fc = _fourier(zc, C, dft_ctx)
            c2 = _merge(fc, nc, zc, c2, mod, gmlp_ln_g, gmlp_ln_b, ws, bst, wpa, wpf, wpn, wout, l,
                        rows_per_mod=ONE_ROW, mod_row0=B)
        x2 = x_new
    return x2.reshape(B, S, D)
```

```python
import functools

import jax
import jax.numpy as jnp
import numpy as np
from jax import lax
from jax.experimental import pallas as pl
from jax.experimental.pallas import tpu as pltpu

F32 = jnp.float32
BF16 = jnp.bfloat16

D_MODEL = 2048
CTX_LEN = 256
GRID_W = 64
EPS = 1e-6
CHUNK = 128
GMLP_GROUPS = 8
W_A = D_MODEL // 2
W_B = D_MODEL // 2
F_GROUPS = 4
F_GW = W_B // F_GROUPS
HEAD_DIM = 128
NA_HEADS = 8
W_C = NA_HEADS * HEAD_DIM
NA_KH = 8
NA_KW = 16

OFF_U = 0
OFF_VG = OFF_U + W_A
OFF_GA = OFF_VG + W_A
OFF_F = OFF_GA + W_A
OFF_GF = OFF_F + W_B
OFF_Q = OFF_GF + W_B
OFF_K = OFF_Q + W_C
OFF_V = OFF_K + W_C
OFF_GN = OFF_V + W_C
OFF_MERGE = OFF_GN + W_C
W_IN = OFF_MERGE + 3 * D_MODEL

LANES = 128
BF16_SUBLANES = 16
IN_PROJ_ROWS = 1024
MERGE_ROWS = 256
COL = 1024
MOD_ROWS = 16
ONE_ROW = 1 << 30
VMEM_LIMIT = 56 * 1024 * 1024
Z_DTYPE = jnp.bfloat16

Q_ROWS = 2
WIN_ROWS = 10
SCORE_LOOKAHEAD = 3
HEADS_PER_STEP = 2
NEG = -1e30
LOG2E = float(np.log2(np.e))
ATT_SCALE = HEAD_DIM ** -0.5 * LOG2E


GELU_C1 = float(np.sqrt(2.0 / np.pi))
GELU_C2 = GELU_C1 * 0.044715


def _sigmoid(t):
    return 0.5 * jnp.tanh(0.5 * t) + 0.5


def _silu(t):
    h = 0.5 * t
    return h * jnp.tanh(h) + h


def _gelu(t):
    h = 0.5 * t
    return h * jnp.tanh(t * (GELU_C1 + GELU_C2 * (t * t))) + h


def _dot(a, b):
    return jnp.dot(a, b, preferred_element_type=F32)


def _dot_nt(a, b):
    return lax.dot_general(a, b, (((1,), (1,)), ((), ())), preferred_element_type=F32)


def _params(*sem):
    return pltpu.CompilerParams(dimension_semantics=sem, vmem_limit_bytes=VMEM_LIMIT)


def _mod_kernel(cc_ref, w_ref, b_ref, o_ref):
    s = _silu(cc_ref[...])
    s_hi = s.astype(BF16)
    s_lo = (s - s_hi.astype(F32)).astype(BF16)
    w = w_ref[0]
    w_hi = w.astype(BF16)
    w_lo = (w - w_hi.astype(F32)).astype(BF16)
    o_ref[0] = _dot(s_hi, w_hi) + _dot(s_lo, w_hi) + _dot(s_hi, w_lo) + b_ref[0]


def _modulation(cc, w_ada, b_ada):
    L, D, N = w_ada.shape
    tn = COL
    return pl.pallas_call(
        _mod_kernel,
        grid=(L, N // tn),
        in_specs=[
            pl.BlockSpec((MOD_ROWS, D), lambda l, j: (0, 0)),
            pl.BlockSpec((1, D, tn), lambda l, j: (l, 0, j)),
            pl.BlockSpec((1, 1, tn), lambda l, j: (l, 0, j)),
        ],
        out_specs=pl.BlockSpec((1, MOD_ROWS, tn), lambda l, j: (l, 0, j)),
        out_shape=jax.ShapeDtypeStruct((L, MOD_ROWS, N), F32),
        compiler_params=_params("arbitrary", "arbitrary"),
        name="adaln_mod",
    )(cc, w_ada, b_ada.reshape(L, 1, N))


def _column_activation(c):
    if c < OFF_GA // COL:
        return _gelu
    if c in (OFF_GA // COL, OFF_GF // COL, OFF_GN // COL):
        return _silu
    if c >= OFF_MERGE // COL:
        return _sigmoid
    return lambda z: z


def _in_proj_kernel(x_ref, g_ref, sh_ref, sc_ref, w_ref, o_ref, h_scr, *, col0, ncol):
    i = pl.program_id(0)
    j = pl.program_id(1)
    slot = i % 2
    next_norm = (j == ncol - 1) & (i + 1 < pl.num_programs(0))

    def norm_into(dst):
        x = x_ref[...]
        gain = g_ref[0] * (1.0 + sc_ref[0])
        r = lax.rsqrt(jnp.mean(x * x, axis=-1, keepdims=True) + EPS)
        h_scr[dst] = (x * r * gain + sh_ref[0]).astype(BF16)

    @pl.when((i == 0) & (j == 0))
    def _():
        norm_into(0)

    c = j + col0
    plain = jnp.logical_not(next_norm)

    def project(act):
        o_ref[...] = act(_dot(h_scr[slot], w_ref[0].astype(BF16))).astype(o_ref.dtype)

    @pl.when(plain & (c < OFF_GA // COL))
    def _():
        project(_gelu)

    @pl.when(plain & ((c == OFF_GA // COL) | (c == OFF_GF // COL) | (c == OFF_GN // COL)))
    def _():
        project(_silu)

    @pl.when(plain & (c >= OFF_MERGE // COL))
    def _():
        project(_sigmoid)

    @pl.when(plain & ((c == OFF_F // COL) | ((c >= OFF_Q // COL) & (c < OFF_GN // COL))))
    def _():
        project(lambda z: z)

    @pl.when(next_norm)
    def _():
        norm_into(1 - slot)
        project(_column_activation(col0 + ncol - 1))


def _side_cast_specs(arrays, nsteps, step_of):
    flat = [t.reshape(-1, t.shape[-1]) for t in arrays]
    specs = []
    for t in flat:
        rows = BF16_SUBLANES * pl.cdiv(t.shape[0], BF16_SUBLANES * nsteps)
        while t.shape[0] % rows:
            rows += BF16_SUBLANES
        nblk = t.shape[0] // rows
        specs.append(pl.BlockSpec(
            (rows, t.shape[1]), lambda *idx, nblk=nblk: (jnp.minimum(step_of(*idx), nblk - 1), 0)))
    return flat, specs


def _in_proj(x2, norm_g, mod, w_in, layer, *, rows_per_mod, mod_row0, col0, ncol):
    T, D = x2.shape
    tm = min(IN_PROJ_ROWS, T)
    nrow = T // tm
    norm_tile = lambda i, j: jnp.minimum(i + (j == ncol - 1).astype(jnp.int32), nrow - 1)
    mrow = lambda i, j: layer * MOD_ROWS + mod_row0 + norm_tile(i, j) // rows_per_mod

    return pl.pallas_call(
        functools.partial(_in_proj_kernel, col0=col0, ncol=ncol),
        grid=(nrow, ncol),
        in_specs=[
            pl.BlockSpec((tm, D), lambda i, j: (norm_tile(i, j), 0)),
            pl.BlockSpec((1, 1, D), lambda i, j: (layer, 0, 0)),
            pl.BlockSpec((1, 1, D), lambda i, j: (mrow(i, j), 0, 0)),
            pl.BlockSpec((1, 1, D), lambda i, j: (mrow(i, j), 0, 1)),
            pl.BlockSpec((1, D, COL), lambda i, j: (layer, 0, j + col0)),
        ],
        out_specs=pl.BlockSpec((tm, COL), lambda i, j: (i, j)),
        out_shape=jax.ShapeDtypeStruct((T, ncol * COL), Z_DTYPE),
        scratch_shapes=[pltpu.VMEM((2, tm, D), BF16)],
        compiler_params=_params("arbitrary", "arbitrary"),
        name="in_proj",
    )(x2, norm_g, mod, mod, w_in)


def _spatial_gating(u_ref, vg_ref, ga_ref, lng_ref, lnb_ref, ws_ref, bst_ref, o_ref, vn_scr):
    v = vg_ref[...].astype(F32)
    mu = jnp.mean(v, axis=-1, keepdims=True)
    vc = v - mu
    vn = vc * lax.rsqrt(jnp.mean(vc * vc, axis=-1, keepdims=True) + EPS)
    vn_scr[...] = (vn * lng_ref[0] + lnb_ref[0]).astype(BF16)
    nchunk = u_ref.shape[0] // CHUNK
    for g in range(GMLP_GROUPS):
        cols = slice(g * CHUNK, (g + 1) * CHUNK)
        v_all = jnp.concatenate([vn_scr[ch * CHUNK:(ch + 1) * CHUNK, cols] for ch in range(nchunk)], axis=1)
        sv_all = _dot(ws_ref[0, g], v_all) + bst_ref[0, :, g:g + 1]
        for ch in range(nchunk):
            rows = slice(ch * CHUNK, (ch + 1) * CHUNK)
            sv = sv_all[:, ch * CHUNK:(ch + 1) * CHUNK]
            a = u_ref[rows, cols].astype(F32) * sv * ga_ref[rows, cols].astype(F32)
            o_ref[rows, cols] = a.astype(o_ref.dtype)


DFT_RADIX = 4


def _dft_consts(n):
    m = n // DFT_RADIX
    k = np.arange(m, dtype=np.int64)
    ang = 2.0 * np.pi * ((k[:, None] * k[None, :]) % m).astype(np.float64) / m
    cs = np.concatenate([np.cos(ang), np.sin(ang)], axis=0)
    th = 2.0 * np.pi * np.arange(n // 2, dtype=np.float64) / n
    twc = np.broadcast_to(np.cos(th)[:, None], (n // 2, LANES))
    tws = np.broadcast_to(np.sin(th)[:, None], (n // 2, LANES))
    c = np.arange(F_GW, dtype=np.int64)
    angc = 2.0 * np.pi * ((c[:, None] * c[None, :]) % F_GW).astype(np.float64) / F_GW
    bf = lambda t: jnp.asarray(t, dtype=F32).astype(BF16)
    return bf(cs), jnp.asarray(twc, dtype=F32), jnp.asarray(tws, dtype=F32), bf(np.cos(angc)), bf(np.sin(angc))


def _dft_stages(f_ref, gf_ref, cs_ref, twc_ref, tws_ref, cc_ref, sc_ref, o_ref, x_scr):
    n = f_ref.shape[0]
    m0 = n // DFT_RADIX
    lane_tiles = F_GW // LANES

    def position_dft(g):
        for t in range(lane_tiles):
            x_scr[g, t] = f_ref[:, g * F_GW + t * LANES:g * F_GW + (t + 1) * LANES].astype(F32)
        xcat = jnp.concatenate(
            [x_scr[g, t, pl.ds(r, m0, stride=DFT_RADIX), :].astype(BF16)
             for r in range(DFT_RADIX) for t in range(lane_tiles)], axis=1)
        return _dot(cs_ref[...], xcat)

    def butterflies(e):
        base = [(e[:m0, r * F_GW:(r + 1) * F_GW], e[m0:, r * F_GW:(r + 1) * F_GW])
                for r in range(DFT_RADIX)]

        def sub_dft(stride, offset):
            if stride == DFT_RADIX:
                return base[offset]
            ec0, es0 = sub_dft(2 * stride, offset)
            ec1, es1 = sub_dft(2 * stride, offset + stride)
            m = n // (2 * stride)
            c = jnp.concatenate([twc_ref[pl.ds(0, m, stride=stride), :]] * lane_tiles, axis=1)
            s = jnp.concatenate([tws_ref[pl.ds(0, m, stride=stride), :]] * lane_tiles, axis=1)
            a = c * ec1 - s * es1
            b = c * es1 + s * ec1
            return (jnp.concatenate([ec0 + a, ec0 - a], axis=0),
                    jnp.concatenate([es0 + b, es0 - b], axis=0))

        return sub_dft(1, 0)

    def channel_dft(g, tc, ts):
        cols = slice(g * F_GW, (g + 1) * F_GW)
        y = _dot(tc.astype(BF16), cc_ref[...]) - _dot(ts.astype(BF16), sc_ref[...])
        o_ref[:, cols] = (y * ((n * F_GW) ** -0.5) * gf_ref[:, cols].astype(F32)).astype(o_ref.dtype)

    return position_dft, butterflies, channel_dft


def _fourier_kernel(*refs):
    position_dft, butterflies, channel_dft = _dft_stages(*refs)
    nxt = position_dft(0)
    for g in range(F_GROUPS):
        e = nxt
        if g + 1 < F_GROUPS:
            nxt = position_dft(g + 1)
        channel_dft(g, *butterflies(e))


def _fourier(z, n, consts):
    T = z.shape[0]
    const = lambda t: pl.BlockSpec(t.shape, lambda b: (0, 0), pipeline_mode=pl.Buffered(1))
    return pl.pallas_call(
        _fourier_kernel,
        grid=(T // n,),
        in_specs=[
            pl.BlockSpec((n, W_B), lambda b: (b, OFF_F // W_B)),
            pl.BlockSpec((n, W_B), lambda b: (b, OFF_GF // W_B)),
        ] + [const(t) for t in consts],
        out_specs=pl.BlockSpec((n, W_B), lambda b: (b, 0)),
        out_shape=jax.ShapeDtypeStruct((T, W_B), BF16),
        scratch_shapes=[pltpu.VMEM((F_GROUPS, F_GW // LANES, n, LANES), F32)],
        compiler_params=_params("arbitrary"),
        name="fourier",
    )(z, z, *consts)


def _head_rms(t, g):
    t = t.astype(F32)
    return t * lax.rsqrt(jnp.mean(t * t, axis=-1, keepdims=True) + EPS) * g


def _block_geometry(rows):
    assert rows % Q_ROWS == 0 and WIN_ROWS % 2 == 0 and Q_ROWS + NA_KH - 1 <= WIN_ROWS <= rows
    geo, types = [], {}
    for blk in range(rows // Q_ROWS):
        base = min(max(Q_ROWS * blk - NA_KH // 2, 0), rows - WIN_ROWS)
        starts = tuple(_window_start(Q_ROWS * blk + i, rows) - base for i in range(Q_ROWS))
        signature = (base - Q_ROWS * blk, starts)
        geo.append((base, types.setdefault(signature, len(types))))
    return geo


def _window_start(r, rows):
    return min(max(r - NA_KH // 2, 0), rows - NA_KH)


def _build_bias_tables(rpb_ref, tab_scr, head, geo, rows):
    n_dr, n_dc = 2 * NA_KH - 1, 2 * NA_KW - 1
    shape = (GRID_W, 2 * GRID_W)
    lane = lax.broadcasted_iota(jnp.int32, shape, 1)
    cq = lax.broadcasted_iota(jnp.int32, shape, 0)
    kc = lane & (GRID_W - 1)
    dc = kc - cq + (NA_KW - 1)
    cs = jnp.clip(cq - NA_KW // 2, 0, GRID_W - NA_KW)
    col_ok = (kc >= cs) & (kc < cs + NA_KW)
    neg = jnp.full(shape, NEG, F32)
    lane_row = lax.broadcasted_iota(jnp.int32, (1, 2 * GRID_W), 1)
    dc_safe = jnp.clip(dc, 0, 2 * GRID_W - 1)
    row_bias = []
    for d in range(n_dr):
        row = jnp.zeros((1, 2 * GRID_W), F32)
        for b in range(n_dc):
            row = jnp.where(lane_row == b, rpb_ref[(head * n_dr + d) * n_dc + b] * LOG2E, row)
        t = jnp.take_along_axis(jnp.broadcast_to(row, shape), dc_safe, axis=1)
        row_bias.append(jnp.where(col_ok, t, NEG))
    first_block = {typ: blk for blk, (_, typ) in reversed(list(enumerate(geo)))}
    for typ, blk in first_block.items():
        base = geo[blk][0]
        for i in range(Q_ROWS):
            r = Q_ROWS * blk + i
            rs = _window_start(r, rows)
            for wp in range(WIN_ROWS // 2):
                halves = []
                for kr in (base + 2 * wp, base + 2 * wp + 1):
                    halves.append(row_bias[kr - r + NA_KH - 1] if rs <= kr < rs + NA_KH else neg)
                tab_scr[typ, i * GRID_W:(i + 1) * GRID_W, wp * 2 * GRID_W:(wp + 1) * 2 * GRID_W] = (
                    jnp.where(lane < GRID_W, halves[0], halves[1]))


def _nbr_attn_kernel(rpb_ref, q_ref, k_ref, v_ref, gn_ref, kc_ref, vc_ref, qg_ref, kg_ref,
                     f_ref, gf_ref, cs_ref, twc_ref, tws_ref, cc_ref, sc_ref, *rest,
                     geo, rows, layer, n_cast):
    cast_in, (o_ref, fo_ref, *rest) = rest[:n_cast], rest[n_cast:]
    cast_out, (qn_scr, kn_scr, kcn_scr, vx_scr, vcx_scr, tab_ref, x_scr) = rest[:n_cast], rest[n_cast:]
    hd = HEAD_DIM
    position_dft, butterflies, channel_dft = _dft_stages(
        f_ref, gf_ref, cs_ref, twc_ref, tws_ref, cc_ref, sc_ref, fo_ref, x_scr)

    @pl.when(pl.program_id(1) == 0)
    def _():
        for hh in range(HEADS_PER_STEP):
            head = layer * NA_HEADS + pl.program_id(0) * HEADS_PER_STEP + hh
            _build_bias_tables(rpb_ref, tab_ref.at[hh], head, geo, rows)
            vx_scr[hh, :, hd:] = jnp.ones((vx_scr.shape[1], hd), BF16)
            vcx_scr[hh, :, hd:] = jnp.ones((vcx_scr.shape[1], hd), BF16)

    for src, dst in zip(cast_in, cast_out):
        dst[...] = src[...].astype(dst.dtype)

    nq = Q_ROWS * GRID_W
    nk = WIN_ROWS * GRID_W

    def prepare(hh):
        cols = slice(hh * hd, (hh + 1) * hd)
        qn_scr[hh] = _head_rms(q_ref[:, cols], qg_ref[0] * ATT_SCALE).astype(BF16)
        kn_scr[hh] = _head_rms(k_ref[:, cols], kg_ref[0]).astype(BF16)
        kcn_scr[hh] = _head_rms(kc_ref[:, cols], kg_ref[0]).astype(BF16)
        vx_scr[hh, :, :hd] = v_ref[:, cols].astype(BF16)
        vcx_scr[hh, :, :hd] = vc_ref[:, cols].astype(BF16)

    def scores(hh, blk):
        base, typ = geo[blk]
        q = qn_scr[hh, blk * nq:(blk + 1) * nq, :]
        s_loc = _dot_nt(q, kn_scr[hh, base * GRID_W:base * GRID_W + nk, :]) + tab_ref[hh, typ]
        return s_loc, _dot_nt(q, kcn_scr[hh])

    work = [(hh, blk) for hh in range(HEADS_PER_STEP) for blk in range(len(geo))]
    dft = position_dft(0)
    prepare(0)
    pending = [scores(*w) for w in work[:SCORE_LOOKAHEAD]]
    dft = butterflies(dft)
    for n, (hh, blk) in enumerate(work):
        if blk == 0 and hh + 1 < HEADS_PER_STEP:
            prepare(hh + 1)
        if n == 1:
            channel_dft(0, *dft)
        base, typ = geo[blk]
        qs = slice(blk * nq, (blk + 1) * nq)
        ks = slice(base * GRID_W, base * GRID_W + nk)
        cols = slice(hh * hd, (hh + 1) * hd)
        s_loc, s_ctx = pending.pop(0)
        if n + SCORE_LOOKAHEAD < len(work):
            pending.append(scores(*work[n + SCORE_LOOKAHEAD]))
        m = jnp.maximum(jnp.max(s_loc, axis=-1, keepdims=True), jnp.max(s_ctx, axis=-1, keepdims=True))
        p_loc = jnp.exp2(s_loc - m)
        p_ctx = jnp.exp2(s_ctx - m)
        ox = _dot(p_loc.astype(BF16), vx_scr[hh, ks, :]) + _dot(p_ctx.astype(BF16), vcx_scr[hh])
        o_ref[qs, cols] = (ox[:, :hd] / ox[:, hd:] * gn_ref[qs, cols].astype(F32)).astype(o_ref.dtype)


def _nbr_attention(z, zc, ck, cv, qg, kg, rpb_flat, dft_consts, layer, seq, cast=()):
    T = z.shape[0]
    rows = seq // GRID_W
    geo = _block_geometry(rows)
    hd = HEAD_DIM
    hps = HEADS_PER_STEP
    wide = hps * hd
    assert NA_HEADS % hps == 0 and ck % hps == 0 and cv % hps == 0
    assert NA_HEADS // hps == F_GROUPS
    const = lambda t: pl.BlockSpec(t.shape, lambda h, b: (0, 0), pipeline_mode=pl.Buffered(1))
    ntypes = 1 + max(t for _, t in geo)
    nb = T // seq
    cast2d, cast_specs = _side_cast_specs(cast, (NA_HEADS // hps) * nb, lambda h, b: h * nb + b)
    lat = lambda off: pl.BlockSpec((seq, wide), lambda h, b: (b, off // wide + h))
    n, f, *casted = pl.pallas_call(
        functools.partial(_nbr_attn_kernel, geo=geo, rows=rows, layer=layer, n_cast=len(cast)),
        grid=(NA_HEADS // hps, nb),
        in_specs=[
            pl.BlockSpec(memory_space=pltpu.SMEM),
            lat(OFF_Q), lat(OFF_K), lat(OFF_V), lat(OFF_GN),
            pl.BlockSpec((CTX_LEN, wide), lambda h, b: (b, ck // hps + h)),
            pl.BlockSpec((CTX_LEN, wide), lambda h, b: (b, cv // hps + h)),
            pl.BlockSpec((1, 1, hd), lambda h, b: (layer, 0, 0)),
            pl.BlockSpec((1, 1, hd), lambda h, b: (layer, 0, 0)),
            pl.BlockSpec((seq, F_GW), lambda h, b: (b, OFF_F // F_GW + h)),
            pl.BlockSpec((seq, F_GW), lambda h, b: (b, OFF_GF // F_GW + h)),
        ] + [const(t) for t in dft_consts] + cast_specs,
        out_specs=[pl.BlockSpec((seq, wide), lambda h, b: (b, h)),
                   pl.BlockSpec((seq, F_GW), lambda h, b: (b, h))] + cast_specs,
        out_shape=[jax.ShapeDtypeStruct((T, W_C), BF16), jax.ShapeDtypeStruct((T, W_B), BF16)]
        + [jax.ShapeDtypeStruct(t.shape, BF16) for t in cast2d],
        scratch_shapes=[pltpu.VMEM((hps, seq, hd), BF16), pltpu.VMEM((hps, seq, hd), BF16),
                        pltpu.VMEM((hps, CTX_LEN, hd), BF16),
                        pltpu.VMEM((hps, seq, 2 * hd), BF16), pltpu.VMEM((hps, CTX_LEN, 2 * hd), BF16),
                        pltpu.VMEM((hps, ntypes, Q_ROWS * GRID_W, WIN_ROWS * GRID_W), F32),
                        pltpu.VMEM((1, F_GW // LANES, seq, LANES), F32)],
        compiler_params=_params("arbitrary", "arbitrary"),
        name="nbr_attn",
    )(rpb_flat, z, z, z, z, zc, zc, qg, kg, z, z, *dft_consts, *cast2d)
    return (n, f, *(t.reshape(s.shape) for t, s in zip(casted, cast)))


def _ctx_attn_kernel(q_ref, k_ref, v_ref, gn_ref, qg_ref, kg_ref, o_ref):
    for h in range(NA_HEADS):
        cols = slice(h * HEAD_DIM, (h + 1) * HEAD_DIM)
        q = _head_rms(q_ref[:, cols], qg_ref[0] * ATT_SCALE).astype(BF16)
        k = _head_rms(k_ref[:, cols], kg_ref[0]).astype(BF16)
        s = _dot_nt(q, k)
        p = jnp.exp2(s - jnp.max(s, axis=-1, keepdims=True))
        o = _dot(p.astype(BF16), v_ref[:, cols].astype(BF16)) / jnp.sum(p, axis=-1, keepdims=True)
        o_ref[:, cols] = (o * gn_ref[:, cols].astype(F32)).astype(o_ref.dtype)


def _ctx_attention(zc, qg, kg, layer):
    T = zc.shape[0]
    blk = lambda off: pl.BlockSpec((CTX_LEN, W_C), lambda b: (b, off // W_C))
    vec = pl.BlockSpec((1, 1, HEAD_DIM), lambda b: (layer, 0, 0))
    return pl.pallas_call(
        _ctx_attn_kernel,
        grid=(T // CTX_LEN,),
        in_specs=[blk(OFF_Q), blk(OFF_K), blk(OFF_V), blk(OFF_GN), vec, vec],
        out_specs=pl.BlockSpec((CTX_LEN, W_C), lambda b: (b, 0)),
        out_shape=jax.ShapeDtypeStruct((T, W_C), BF16),
        compiler_params=_params("arbitrary"),
        name="ctx_attn",
    )(zc, zc, zc, zc, qg, kg)


def _merge_kernel(u_ref, vg_ref, ga_ref, f_ref, n_ref, ga0, ga1, gf0, gf1, gn0, gn1, x_ref, gate_ref,
                  lng_ref, lnb_ref, ws_ref, bst_ref, wpa_ref, wpf_ref, wpn_ref, wout_ref, o_ref,
                  a_scr, vn_scr, yfn_scr, y_scr):
    f = f_ref[...]
    n = n_ref[...]
    for half, (gf, gn) in enumerate(((gf0, gn0), (gf1, gn1))):
        cols = slice(half * COL, (half + 1) * COL)
        yfn_scr[:, cols] = (gf[...].astype(F32) * _dot(f, wpf_ref[0, :, cols])
                            + gn[...].astype(F32) * _dot(n, wpn_ref[0, :, cols]))
    _spatial_gating(u_ref, vg_ref, ga_ref, lng_ref, lnb_ref, ws_ref, bst_ref, a_scr, vn_scr)
    a = a_scr[...]
    for half, ga in enumerate((ga0, ga1)):
        cols = slice(half * COL, (half + 1) * COL)
        y = yfn_scr[:, cols] + ga[...].astype(F32) * _dot(a, wpa_ref[0, :, cols])
        y_scr[:, cols] = y.astype(BF16)
    o_ref[...] = x_ref[...] + gate_ref[0] * _dot(y_scr[...], wout_ref[0])


def _merge(f, n, z, x2, mod, lng, lnb, ws, bst, wpa, wpf, wpn, wout, layer, *, rows_per_mod, mod_row0):
    T, D = x2.shape
    tm = MERGE_ROWS
    mrow = lambda i: layer * MOD_ROWS + mod_row0 + i // rows_per_mod
    act = pl.BlockSpec((tm, COL), lambda i: (i, 0))
    zblk = lambda c: pl.BlockSpec((tm, COL), lambda i: (i, c))
    const = lambda *shape: pl.BlockSpec((1,) + shape, lambda i: (layer,) + (0,) * len(shape),
                                        pipeline_mode=pl.Buffered(1))
    return pl.pallas_call(
        _merge_kernel,
        grid=(T // tm,),
        in_specs=[
            zblk(OFF_U // COL), zblk(OFF_VG // COL), zblk(OFF_GA // COL),
            act, act,
        ] + [zblk(OFF_MERGE // COL + c) for c in range(3 * D // COL)] + [
            pl.BlockSpec((tm, D), lambda i: (i, 0)),
            pl.BlockSpec((1, 1, D), lambda i: (mrow(i), 0, 2)),
            const(1, W_A), const(1, W_A), const(GMLP_GROUPS, CHUNK, CHUNK), const(CHUNK, GMLP_GROUPS),
            const(W_A, D), const(W_B, D), const(W_C, D), const(D, D),
        ],
        out_specs=pl.BlockSpec((tm, D), lambda i: (i, 0)),
        out_shape=jax.ShapeDtypeStruct((T, D), F32),
        scratch_shapes=[pltpu.VMEM((tm, W_A), BF16), pltpu.VMEM((tm, W_A), BF16),
                        pltpu.VMEM((tm, D), F32), pltpu.VMEM((tm, D), BF16)],
        compiler_params=_params("arbitrary"),
        name="merge_out",
    )(z, z, z, f, n, z, z, z, z, z, z, x2, mod, lng, lnb, ws, bst, wpa, wpf, wpn, wout)


def kernel(x, c, ctx, c_ctx, norm_g, w_ada, b_ada, w_in, gmlp_ln_g, gmlp_ln_b, gmlp_ws, gmlp_bs,
           q_norm_g, k_norm_g, rpb, w_pa, w_pf, w_pn, w_out):
    B, S, D = x.shape
    C = ctx.shape[1]
    L = w_in.shape[0]
    assert D == D_MODEL and C == CTX_LEN and B < MOD_ROWS
    assert S % (GRID_W * Q_ROWS) == 0 and S // GRID_W >= WIN_ROWS

    cc = jnp.concatenate([c, c_ctx[None], jnp.zeros((MOD_ROWS - B - 1, D), F32)], axis=0)
    mod = _modulation(cc, w_ada, b_ada).reshape(L * MOD_ROWS, 1, 3 * D)

    ws = gmlp_ws.astype(BF16)
    bst = jnp.swapaxes(gmlp_bs, 1, 2)
    rpb_flat = rpb.reshape(-1)
    norm_g, gmlp_ln_g, gmlp_ln_b, q_norm_g, k_norm_g = (
        t[:, None, :] for t in (norm_g, gmlp_ln_g, gmlp_ln_b, q_norm_g, k_norm_g))
    dft_lat = _dft_consts(S)
    dft_ctx = _dft_consts(C)
    x2 = x.reshape(B * S, D)
    c2 = ctx.reshape(B * C, D)
    for l in range(L):
        last = l == L - 1
        z = _in_proj(x2, norm_g, mod, w_in, l, rows_per_mod=S // IN_PROJ_ROWS, mod_row0=0, col0=0,
                     ncol=W_IN // COL)
        if last:
            zc = _in_proj(c2, norm_g, mod, w_in, l, rows_per_mod=ONE_ROW, mod_row0=B,
                          col0=OFF_K // COL, ncol=2)
            ck, cv = 0, W_C // HEAD_DIM
        else:
            zc = _in_proj(c2, norm_g, mod, w_in, l, rows_per_mod=ONE_ROW, mod_row0=B, col0=0,
                          ncol=W_IN // COL)
            ck, cv = OFF_K // HEAD_DIM, OFF_V // HEAD_DIM

        n, f, *cast = _nbr_attention(z, zc, ck, cv, q_norm_g, k_norm_g, rpb_flat, dft_lat, l, S,
                                     cast=(w_pa, w_pf, w_pn, w_out) if l == 0 else ())
        if l == 0:
            wpa, wpf, wpn, wout = cast
        x_new = _merge(f, n, z, x2, mod, gmlp_ln_g, gmlp_ln_b, ws, bst, wpa, wpf, wpn, wout, l,
                       rows_per_mod=S // MERGE_ROWS, mod_row0=0)

        if not last:
            nc = _ctx_attention(zc, q_norm_g, k_norm_g, l)
            fc = _fourier(zc, C, dft_ctx)
            c2 = _merge(fc, nc, zc, c2, mod, gmlp_ln_g, gmlp_ln_b, ws, bst, wpa, wpf, wpn, wout, l,
                        rows_per_mod=ONE_ROW, mod_row0=B)
        x2 = x_new
    return x2.reshape(B, S, D)
```

```python
import functools

import jax
import jax.numpy as jnp
import numpy as np
from jax import lax
from jax.experimental import pallas as pl
from jax.experimental.pallas import tpu as pltpu

F32 = jnp.float32
BF16 = jnp.bfloat16

D_MODEL = 2048
CTX_LEN = 256
GRID_W = 64
EPS = 1e-6
CHUNK = 128
GMLP_GROUPS = 8
W_A = D_MODEL // 2
W_B = D_MODEL // 2
F_GROUPS = 4
F_GW = W_B // F_GROUPS
HEAD_DIM = 128
NA_HEADS = 8
W_C = NA_HEADS * HEAD_DIM
NA_KH = 8
NA_KW = 16

OFF_U = 0
OFF_VG = OFF_U + W_A
OFF_GA = OFF_VG + W_A
OFF_F = OFF_GA + W_A
OFF_GF = OFF_F + W_B
OFF_Q = OFF_GF + W_B
OFF_K = OFF_Q + W_C
OFF_V = OFF_K + W_C
OFF_GN = OFF_V + W_C
OFF_MERGE = OFF_GN + W_C
W_IN = OFF_MERGE + 3 * D_MODEL

LANES = 128
BF16_SUBLANES = 16
IN_PROJ_ROWS = 1024
MERGE_ROWS = 256
COL = 1024
MOD_ROWS = 16
ONE_ROW = 1 << 30
VMEM_LIMIT = 56 * 1024 * 1024
Z_DTYPE = jnp.bfloat16

Q_ROWS = 2
WIN_ROWS = 10
SCORE_LOOKAHEAD = 3
HEADS_PER_STEP = 2
NEG = -1e30
LOG2E = float(np.log2(np.e))
ATT_SCALE = HEAD_DIM ** -0.5 * LOG2E


GELU_C1 = float(np.sqrt(2.0 / np.pi))
GELU_C2 = GELU_C1 * 0.044715


def _sigmoid(t):
    return 0.5 * jnp.tanh(0.5 * t) + 0.5


def _silu(t):
    h = 0.5 * t
    return h * jnp.tanh(h) + h


def _gelu(t):
    h = 0.5 * t
    return h * jnp.tanh(t * (GELU_C1 + GELU_C2 * (t * t))) + h


def _dot(a, b):
    return jnp.dot(a, b, preferred_element_type=F32)


def _dot_nt(a, b):
    return lax.dot_general(a, b, (((1,), (1,)), ((), ())), preferred_element_type=F32)


def _params(*sem):
    return pltpu.CompilerParams(dimension_semantics=sem, vmem_limit_bytes=VMEM_LIMIT)


def _mod_kernel(cc_ref, w_ref, b_ref, o_ref):
    s = _silu(cc_ref[...])
    s_hi = s.astype(BF16)
    s_lo = (s - s_hi.astype(F32)).astype(BF16)
    w = w_ref[0]
    w_hi = w.astype(BF16)
    w_lo = (w - w_hi.astype(F32)).astype(BF16)
    o_ref[0] = _dot(s_hi, w_hi) + _dot(s_lo, w_hi) + _dot(s_hi, w_lo) + b_ref[0]


def _modulation(cc, w_ada, b_ada):
    L, D, N = w_ada.shape
    tn = COL
    return pl.pallas_call(
        _mod_kernel,
        grid=(L, N // tn),
        in_specs=[
            pl.BlockSpec((MOD_ROWS, D), lambda l, j: (0, 0)),
            pl.BlockSpec((1, D, tn), lambda l, j: (l, 0, j)),
            pl.BlockSpec((1, 1, tn), lambda l, j: (l, 0, j)),
        ],
        out_specs=pl.BlockSpec((1, MOD_ROWS, tn), lambda l, j: (l, 0, j)),
        out_shape=jax.ShapeDtypeStruct((L, MOD_ROWS, N), F32),
        compiler_params=_params("arbitrary", "arbitrary"),
        name="adaln_mod",
    )(cc, w_ada, b_ada.reshape(L, 1, N))


def _column_activation(c):
    if c < OFF_GA // COL:
        return _gelu
    if c in (OFF_GA // COL, OFF_GF // COL, OFF_GN // COL):
        return _silu
    if c >= OFF_MERGE // COL:
        return _sigmoid
    return lambda z: z


def _in_proj_kernel(x_ref, g_ref, sh_ref, sc_ref, w_ref, o_ref, h_scr, *, col0, ncol):
    i = pl.program_id(0)
    j = pl.program_id(1)
    slot = i % 2
    next_norm = (j == ncol - 1) & (i + 1 < pl.num_programs(0))

    def norm_into(dst):
        x = x_ref[...]
        gain = g_ref[0] * (1.0 + sc_ref[0])
        r = lax.rsqrt(jnp.mean(x * x, axis=-1, keepdims=True) + EPS)
        h_scr[dst] = (x * r * gain + sh_ref[0]).astype(BF16)

    @pl.when((i == 0) & (j == 0))
    def _():
        norm_into(0)

    c = j + col0
    plain = jnp.logical_not(next_norm)

    def project(act):
        o_ref[...] = act(_dot(h_scr[slot], w_ref[0].astype(BF16))).astype(o_ref.dtype)

    @pl.when(plain & (c < OFF_GA // COL))
    def _():
        project(_gelu)

    @pl.when(plain & ((c == OFF_GA // COL) | (c == OFF_GF // COL) | (c == OFF_GN // COL)))
    def _():
        project(_silu)

    @pl.when(plain & (c >= OFF_MERGE // COL))
    def _():
        project(_sigmoid)

    @pl.when(plain & ((c == OFF_F // COL) | ((c >= OFF_Q // COL) & (c < OFF_GN // COL))))
    def _():
        project(lambda z: z)

    @pl.when(next_norm)
    def _():
        norm_into(1 - slot)
        project(_column_activation(col0 + ncol - 1))


def _side_cast_specs(arrays, nsteps, step_of):
    flat = [t.reshape(-1, t.shape[-1]) for t in arrays]
    specs = []
    for t in flat:
        rows = BF16_SUBLANES * pl.cdiv(t.shape[0], BF16_SUBLANES * nsteps)
        while t.shape[0] % rows:
            rows += BF16_SUBLANES
        nblk = t.shape[0] // rows
        specs.append(pl.BlockSpec(
            (rows, t.shape[1]), lambda *idx, nblk=nblk: (jnp.minimum(step_of(*idx), nblk - 1), 0)))
    return flat, specs


def _in_proj(x2, norm_g, mod, w_in, layer, *, rows_per_mod, mod_row0, col0, ncol):
    T, D = x2.shape
    tm = min(IN_PROJ_ROWS, T)
    nrow = T // tm
    norm_tile = lambda i, j: jnp.minimum(i + (j == ncol - 1).astype(jnp.int32), nrow - 1)
    mrow = lambda i, j: layer * MOD_ROWS + mod_row0 + norm_tile(i, j) // rows_per_mod

    return pl.pallas_call(
        functools.partial(_in_proj_kernel, col0=col0, ncol=ncol),
        grid=(nrow, ncol),
        in_specs=[
            pl.BlockSpec((tm, D), lambda i, j: (norm_tile(i, j), 0)),
            pl.BlockSpec((1, 1, D), lambda i, j: (layer, 0, 0)),
            pl.BlockSpec((1, 1, D), lambda i, j: (mrow(i, j), 0, 0)),
            pl.BlockSpec((1, 1, D), lambda i, j: (mrow(i, j), 0, 1)),
            pl.BlockSpec((1, D, COL), lambda i, j: (layer, 0, j + col0)),
        ],
        out_specs=pl.BlockSpec((tm, COL), lambda i, j: (i, j)),
        out_shape=jax.ShapeDtypeStruct((T, ncol * COL), Z_DTYPE),
        scratch_shapes=[pltpu.VMEM((2, tm, D), BF16)],
        compiler_params=_params("arbitrary", "arbitrary"),
        name="in_proj",
    )(x2, norm_g, mod, mod, w_in)


def _spatial_gating(u_ref, vg_ref, ga_ref, lng_ref, lnb_ref, ws_ref, bst_ref, o_ref, vn_scr):
    v = vg_ref[...].astype(F32)
    mu = jnp.mean(v, axis=-1, keepdims=True)
    vc = v - mu
    vn = vc * lax.rsqrt(jnp.mean(vc * vc, axis=-1, keepdims=True) + EPS)
    vn_scr[...] = (vn * lng_ref[0] + lnb_ref[0]).astype(BF16)
    nchunk = u_ref.shape[0] // CHUNK
    for g in range(GMLP_GROUPS):
        cols = slice(g * CHUNK, (g + 1) * CHUNK)
        v_all = jnp.concatenate([vn_scr[ch * CHUNK:(ch + 1) * CHUNK, cols] for ch in range(nchunk)], axis=1)
        sv_all = _dot(ws_ref[0, g], v_all) + bst_ref[0, :, g:g + 1]
        for ch in range(nchunk):
            rows = slice(ch * CHUNK, (ch + 1) * CHUNK)
            sv = sv_all[:, ch * CHUNK:(ch + 1) * CHUNK]
            a = u_ref[rows, cols].astype(F32) * sv * ga_ref[rows, cols].astype(F32)
            o_ref[rows, cols] = a.astype(o_ref.dtype)


DFT_RADIX = 4


def _dft_consts(n):
    m = n // DFT_RADIX
    k = np.arange(m, dtype=np.int64)
    ang = 2.0 * np.pi * ((k[:, None] * k[None, :]) % m).astype(np.float64) / m
    cs = np.concatenate([np.cos(ang), np.sin(ang)], axis=0)
    th = 2.0 * np.pi * np.arange(n // 2, dtype=np.float64) / n
    twc = np.broadcast_to(np.cos(th)[:, None], (n // 2, LANES))
    tws = np.broadcast_to(np.sin(th)[:, None], (n // 2, LANES))
    c = np.arange(F_GW, dtype=np.int64)
    angc = 2.0 * np.pi * ((c[:, None] * c[None, :]) % F_GW).astype(np.float64) / F_GW
    bf = lambda t: jnp.asarray(t, dtype=F32).astype(BF16)
    return bf(cs), jnp.asarray(twc, dtype=F32), jnp.asarray(tws, dtype=F32), bf(np.cos(angc)), bf(np.sin(angc))


def _dft_stages(f_ref, gf_ref, cs_ref, twc_ref, tws_ref, cc_ref, sc_ref, o_ref, x_scr):
    n = f_ref.shape[0]
    m0 = n // DFT_RADIX
    lane_tiles = F_GW // LANES

    def position_dft(g):
        for t in range(lane_tiles):
            x_scr[g, t] = f_ref[:, g * F_GW + t * LANES:g * F_GW + (t + 1) * LANES].astype(F32)
        xcat = jnp.concatenate(
            [x_scr[g, t, pl.ds(r, m0, stride=DFT_RADIX), :].astype(BF16)
             for r in range(DFT_RADIX) for t in range(lane_tiles)], axis=1)
        return _dot(cs_ref[...], xcat)

    def butterflies(e):
        base = [(e[:m0, r * F_GW:(r + 1) * F_GW], e[m0:, r * F_GW:(r + 1) * F_GW])
                for r in range(DFT_RADIX)]

        def sub_dft(stride, offset):
            if stride == DFT_RADIX:
                return base[offset]
            ec0, es0 = sub_dft(2 * stride, offset)
            ec1, es1 = sub_dft(2 * stride, offset + stride)
            m = n // (2 * stride)
            c = jnp.concatenate([twc_ref[pl.ds(0, m, stride=stride), :]] * lane_tiles, axis=1)
            s = jnp.concatenate([tws_ref[pl.ds(0, m, stride=stride), :]] * lane_tiles, axis=1)
            a = c * ec1 - s * es1
            b = c * es1 + s * ec1
            return (jnp.concatenate([ec0 + a, ec0 - a], axis=0),
                    jnp.concatenate([es0 + b, es0 - b], axis=0))

        return sub_dft(1, 0)

    def channel_dft(g, tc, ts):
        cols = slice(g * F_GW, (g + 1) * F_GW)
        y = _dot(tc.astype(BF16), cc_ref[...]) - _dot(ts.astype(BF16), sc_ref[...])
        o_ref[:, cols] = (y * ((n * F_GW) ** -0.5) * gf_ref[:, cols].astype(F32)).astype(o_ref.dtype)

    return position_dft, butterflies, channel_dft


def _fourier_kernel(*refs):
    position_dft, butterflies, channel_dft = _dft_stages(*refs)
    nxt = position_dft(0)
    for g in range(F_GROUPS):
        e = nxt
        if g + 1 < F_GROUPS:
            nxt = position_dft(g + 1)
        channel_dft(g, *butterflies(e))


def _head_rms(t, g):
    t = t.astype(F32)
    return t * lax.rsqrt(jnp.mean(t * t, axis=-1, keepdims=True) + EPS) * g


def _block_geometry(rows):
    assert rows % Q_ROWS == 0 and WIN_ROWS % 2 == 0 and Q_ROWS + NA_KH - 1 <= WIN_ROWS <= rows
    geo, types = [], {}
    for blk in range(rows // Q_ROWS):
        base = min(max(Q_ROWS * blk - NA_KH // 2, 0), rows - WIN_ROWS)
        starts = tuple(_window_start(Q_ROWS * blk + i, rows) - base for i in range(Q_ROWS))
        signature = (base - Q_ROWS * blk, starts)
        geo.append((base, types.setdefault(signature, len(types))))
    return geo


def _window_start(r, rows):
    return min(max(r - NA_KH // 2, 0), rows - NA_KH)


def _build_bias_tables(rpb_ref, tab_scr, head, geo, rows):
    n_dr, n_dc = 2 * NA_KH - 1, 2 * NA_KW - 1
    shape = (GRID_W, 2 * GRID_W)
    lane = lax.broadcasted_iota(jnp.int32, shape, 1)
    cq = lax.broadcasted_iota(jnp.int32, shape, 0)
    kc = lane & (GRID_W - 1)
    dc = kc - cq + (NA_KW - 1)
    cs = jnp.clip(cq - NA_KW // 2, 0, GRID_W - NA_KW)
    col_ok = (kc >= cs) & (kc < cs + NA_KW)
    neg = jnp.full(shape, NEG, F32)
    lane_row = lax.broadcasted_iota(jnp.int32, (1, 2 * GRID_W), 1)
    dc_safe = jnp.clip(dc, 0, 2 * GRID_W - 1)
    row_bias = []
    for d in range(n_dr):
        row = jnp.zeros((1, 2 * GRID_W), F32)
        for b in range(n_dc):
            row = jnp.where(lane_row == b, rpb_ref[(head * n_dr + d) * n_dc + b] * LOG2E, row)
        t = jnp.take_along_axis(jnp.broadcast_to(row, shape), dc_safe, axis=1)
        row_bias.append(jnp.where(col_ok, t, NEG))
    first_block = {typ: blk for blk, (_, typ) in reversed(list(enumerate(geo)))}
    for typ, blk in first_block.items():
        base = geo[blk][0]
        for i in range(Q_ROWS):
            r = Q_ROWS * blk + i
            rs = _window_start(r, rows)
            for wp in range(WIN_ROWS // 2):
                halves = []
                for kr in (base + 2 * wp, base + 2 * wp + 1):
                    halves.append(row_bias[kr - r + NA_KH - 1] if rs <= kr < rs + NA_KH else neg)
                tab_scr[typ, i * GRID_W:(i + 1) * GRID_W, wp * 2 * GRID_W:(wp + 1) * 2 * GRID_W] = (
                    jnp.where(lane < GRID_W, halves[0], halves[1]))


def _nbr_attn_kernel(rpb_ref, q_ref, k_ref, v_ref, gn_ref, kc_ref, vc_ref, qg_ref, kg_ref,
                     f_ref, gf_ref, cs_ref, twc_ref, tws_ref, cc_ref, sc_ref, *rest,
                     geo, rows, layer, n_cast):
    cast_in, (o_ref, fo_ref, *rest) = rest[:n_cast], rest[n_cast:]
    cast_out, (qn_scr, kn_scr, kcn_scr, vx_scr, vcx_scr, tab_ref, x_scr) = rest[:n_cast], rest[n_cast:]
    hd = HEAD_DIM
    position_dft, butterflies, channel_dft = _dft_stages(
        f_ref, gf_ref, cs_ref, twc_ref, tws_ref, cc_ref, sc_ref, fo_ref, x_scr)

    @pl.when(pl.program_id(1) == 0)
    def _():
        for hh in range(HEADS_PER_STEP):
            head = layer * NA_HEADS + pl.program_id(0) * HEADS_PER_STEP + hh
            _build_bias_tables(rpb_ref, tab_ref.at[hh], head, geo, rows)
            vx_scr[hh, :, hd:] = jnp.ones((vx_scr.shape[1], hd), BF16)
            vcx_scr[hh, :, hd:] = jnp.ones((vcx_scr.shape[1], hd), BF16)

    for src, dst in zip(cast_in, cast_out):
        dst[...] = src[...].astype(dst.dtype)

    nq = Q_ROWS * GRID_W
    nk = WIN_ROWS * GRID_W

    def prepare(hh):
        cols = slice(hh * hd, (hh + 1) * hd)
        qn_scr[hh] = _head_rms(q_ref[:, cols], qg_ref[0] * ATT_SCALE).astype(BF16)
        kn_scr[hh] = _head_rms(k_ref[:, cols], kg_ref[0]).astype(BF16)
        kcn_scr[hh] = _head_rms(kc_ref[:, cols], kg_ref[0]).astype(BF16)
        vx_scr[hh, :, :hd] = v_ref[:, cols].astype(BF16)
        vcx_scr[hh, :, :hd] = vc_ref[:, cols].astype(BF16)

    def scores(hh, blk):
        base, typ = geo[blk]
        q = qn_scr[hh, blk * nq:(blk + 1) * nq, :]
        s_loc = _dot_nt(q, kn_scr[hh, base * GRID_W:base * GRID_W + nk, :]) + tab_ref[hh, typ]
        return s_loc, _dot_nt(q, kcn_scr[hh])

    work = [(hh, blk) for hh in range(HEADS_PER_STEP) for blk in range(len(geo))]
    dft = position_dft(0)
    prepare(0)
    pending = [scores(*w) for w in work[:SCORE_LOOKAHEAD]]
    dft = butterflies(dft)
    for n, (hh, blk) in enumerate(work):
        if blk == 0 and hh + 1 < HEADS_PER_STEP:
            prepare(hh + 1)
        if n == 1:
            channel_dft(0, *dft)
        base, typ = geo[blk]
        qs = slice(blk * nq, (blk + 1) * nq)
        ks = slice(base * GRID_W, base * GRID_W + nk)
        cols = slice(hh * hd, (hh + 1) * hd)
        s_loc, s_ctx = pending.pop(0)
        if n + SCORE_LOOKAHEAD < len(work):
            pending.append(scores(*work[n + SCORE_LOOKAHEAD]))
        m = jnp.maximum(jnp.max(s_loc, axis=-1, keepdims=True), jnp.max(s_ctx, axis=-1, keepdims=True))
        p_loc = jnp.exp2(s_loc - m)
        p_ctx = jnp.exp2(s_ctx - m)
        ox = _dot(p_loc.astype(BF16), vx_scr[hh, ks, :]) + _dot(p_ctx.astype(BF16), vcx_scr[hh])
        o_ref[qs, cols] = (ox[:, :hd] / ox[:, hd:] * gn_ref[qs, cols].astype(F32)).astype(o_ref.dtype)


def _nbr_attention(z, zc, ck, cv, qg, kg, rpb_flat, dft_consts, layer, seq, cast=()):
    T = z.shape[0]
    rows = seq // GRID_W
    geo = _block_geometry(rows)
    hd = HEAD_DIM
    hps = HEADS_PER_STEP
    wide = hps * hd
    assert NA_HEADS % hps == 0 and ck % hps == 0 and cv % hps == 0
    assert NA_HEADS // hps == F_GROUPS
    const = lambda t: pl.BlockSpec(t.shape, lambda h, b: (0, 0), pipeline_mode=pl.Buffered(1))
    ntypes = 1 + max(t for _, t in geo)
    nb = T // seq
    cast2d, cast_specs = _side_cast_specs(cast, (NA_HEADS // hps) * nb, lambda h, b: h * nb + b)
    lat = lambda off: pl.BlockSpec((seq, wide), lambda h, b: (b, off // wide + h))
    n, f, *casted = pl.pallas_call(
        functools.partial(_nbr_attn_kernel, geo=geo, rows=rows, layer=layer, n_cast=len(cast)),
        grid=(NA_HEADS // hps, nb),
        in_specs=[
            pl.BlockSpec(memory_space=pltpu.SMEM),
            lat(OFF_Q), lat(OFF_K), lat(OFF_V), lat(OFF_GN),
            pl.BlockSpec((CTX_LEN, wide), lambda h, b: (b, ck // hps + h)),
            pl.BlockSpec((CTX_LEN, wide), lambda h, b: (b, cv // hps + h)),
            pl.BlockSpec((1, 1, hd), lambda h, b: (layer, 0, 0)),
            pl.BlockSpec((1, 1, hd), lambda h, b: (layer, 0, 0)),
            pl.BlockSpec((seq, F_GW), lambda h, b: (b, OFF_F // F_GW + h)),
            pl.BlockSpec((seq, F_GW), lambda h, b: (b, OFF_GF // F_GW + h)),
        ] + [const(t) for t in dft_consts] + cast_specs,
        out_specs=[pl.BlockSpec((seq, wide), lambda h, b: (b, h)),
                   pl.BlockSpec((seq, F_GW), lambda h, b: (b, h))] + cast_specs,
        out_shape=[jax.ShapeDtypeStruct((T, W_C), BF16), jax.ShapeDtypeStruct((T, W_B), BF16)]
        + [jax.ShapeDtypeStruct(t.shape, BF16) for t in cast2d],
        scratch_shapes=[pltpu.VMEM((hps, seq, hd), BF16), pltpu.VMEM((hps, seq, hd), BF16),
                        pltpu.VMEM((hps, CTX_LEN, hd), BF16),
                        pltpu.VMEM((hps, seq, 2 * hd), BF16), pltpu.VMEM((hps, CTX_LEN, 2 * hd), BF16),
                        pltpu.VMEM((hps, ntypes, Q_ROWS * GRID_W, WIN_ROWS * GRID_W), F32),
                        pltpu.VMEM((1, F_GW // LANES, seq, LANES), F32)],
        compiler_params=_params("arbitrary", "arbitrary"),
        name="nbr_attn",
    )(rpb_flat, z, z, z, z, zc, zc, qg, kg, z, z, *dft_consts, *cast2d)
    return (n, f, *(t.reshape(s.shape) for t, s in zip(casted, cast)))


def _ctx_attn_kernel(q_ref, k_ref, v_ref, gn_ref, qg_ref, kg_ref, o_ref):
    for h in range(NA_HEADS):
        cols = slice(h * HEAD_DIM, (h + 1) * HEAD_DIM)
        q = _head_rms(q_ref[:, cols], qg_ref[0] * ATT_SCALE).astype(BF16)
        k = _head_rms(k_ref[:, cols], kg_ref[0]).astype(BF16)
        s = _dot_nt(q, k)
        p = jnp.exp2(s - jnp.max(s, axis=-1, keepdims=True))
        o = _dot(p.astype(BF16), v_ref[:, cols].astype(BF16)) / jnp.sum(p, axis=-1, keepdims=True)
        o_ref[:, cols] = (o * gn_ref[:, cols].astype(F32)).astype(o_ref.dtype)


def _ctx_mix_kernel(q_ref, k_ref, v_ref, gn_ref, qg_ref, kg_ref, f_ref, gf_ref, cs_ref, twc_ref, tws_ref,
                    cc_ref, sc_ref, o_ref, fo_ref, x_scr):
    _fourier_kernel(f_ref, gf_ref, cs_ref, twc_ref, tws_ref, cc_ref, sc_ref, fo_ref, x_scr)
    _ctx_attn_kernel(q_ref, k_ref, v_ref, gn_ref, qg_ref, kg_ref, o_ref)


def _ctx_mix(zc, qg, kg, layer, dft_consts):
    T = zc.shape[0]
    blk = lambda off, width: pl.BlockSpec((CTX_LEN, width), lambda b: (b, off // width))
    vec = pl.BlockSpec((1, 1, HEAD_DIM), lambda b: (layer, 0, 0))
    const = lambda t: pl.BlockSpec(t.shape, lambda b: (0, 0), pipeline_mode=pl.Buffered(1))
    return pl.pallas_call(
        _ctx_mix_kernel,
        grid=(T // CTX_LEN,),
        in_specs=[blk(OFF_Q, W_C), blk(OFF_K, W_C), blk(OFF_V, W_C), blk(OFF_GN, W_C), vec, vec,
                  blk(OFF_F, W_B), blk(OFF_GF, W_B)] + [const(t) for t in dft_consts],
        out_specs=[pl.BlockSpec((CTX_LEN, W_C), lambda b: (b, 0)),
                   pl.BlockSpec((CTX_LEN, W_B), lambda b: (b, 0))],
        out_shape=[jax.ShapeDtypeStruct((T, W_C), BF16), jax.ShapeDtypeStruct((T, W_B), BF16)],
        scratch_shapes=[pltpu.VMEM((F_GROUPS, F_GW // LANES, CTX_LEN, LANES), F32)],
        compiler_params=_params("arbitrary"),
        name="ctx_mix",
    )(zc, zc, zc, zc, qg, kg, zc, zc, *dft_consts)


def _merge_kernel(u_ref, vg_ref, ga_ref, f_ref, n_ref, ga0, ga1, gf0, gf1, gn0, gn1, x_ref, gate_ref,
                  lng_ref, lnb_ref, ws_ref, bst_ref, wpa_ref, wpf_ref, wpn_ref, wout_ref, o_ref,
                  a_scr, vn_scr, yfn_scr, y_scr):
    f = f_ref[...]
    n = n_ref[...]
    for half, (gf, gn) in enumerate(((gf0, gn0), (gf1, gn1))):
        cols = slice(half * COL, (half + 1) * COL)
        yfn_scr[:, cols] = (gf[...].astype(F32) * _dot(f, wpf_ref[0, :, cols])
                            + gn[...].astype(F32) * _dot(n, wpn_ref[0, :, cols]))
    _spatial_gating(u_ref, vg_ref, ga_ref, lng_ref, lnb_ref, ws_ref, bst_ref, a_scr, vn_scr)
    a = a_scr[...]
    for half, ga in enumerate((ga0, ga1)):
        cols = slice(half * COL, (half + 1) * COL)
        y = yfn_scr[:, cols] + ga[...].astype(F32) * _dot(a, wpa_ref[0, :, cols])
        y_scr[:, cols] = y.astype(BF16)
    o_ref[...] = x_ref[...] + gate_ref[0] * _dot(y_scr[...], wout_ref[0])


def _merge(f, n, z, x2, mod, lng, lnb, ws, bst, wpa, wpf, wpn, wout, layer, *, rows_per_mod, mod_row0):
    T, D = x2.shape
    tm = MERGE_ROWS
    mrow = lambda i: layer * MOD_ROWS + mod_row0 + i // rows_per_mod
    act = pl.BlockSpec((tm, COL), lambda i: (i, 0))
    zblk = lambda c: pl.BlockSpec((tm, COL), lambda i: (i, c))
    const = lambda *shape: pl.BlockSpec((1,) + shape, lambda i: (layer,) + (0,) * len(shape),
                                        pipeline_mode=pl.Buffered(1))
    return pl.pallas_call(
        _merge_kernel,
        grid=(T // tm,),
        in_specs=[
            zblk(OFF_U // COL), zblk(OFF_VG // COL), zblk(OFF_GA // COL),
            act, act,
        ] + [zblk(OFF_MERGE // COL + c) for c in range(3 * D // COL)] + [
            pl.BlockSpec((tm, D), lambda i: (i, 0)),
            pl.BlockSpec((1, 1, D), lambda i: (mrow(i), 0, 2)),
            const(1, W_A), const(1, W_A), const(GMLP_GROUPS, CHUNK, CHUNK), const(CHUNK, GMLP_GROUPS),
            const(W_A, D), const(W_B, D), const(W_C, D), const(D, D),
        ],
        out_specs=pl.BlockSpec((tm, D), lambda i: (i, 0)),
        out_shape=jax.ShapeDtypeStruct((T, D), F32),
        scratch_shapes=[pltpu.VMEM((tm, W_A), BF16), pltpu.VMEM((tm, W_A), BF16),
                        pltpu.VMEM((tm, D), F32), pltpu.VMEM((tm, D), BF16)],
        compiler_params=_params("arbitrary"),
        name="merge_out",
    )(z, z, z, f, n, z, z, z, z, z, z, x2, mod, lng, lnb, ws, bst, wpa, wpf, wpn, wout)


def kernel(x, c, ctx, c_ctx, norm_g, w_ada, b_ada, w_in, gmlp_ln_g, gmlp_ln_b, gmlp_ws, gmlp_bs,
           q_norm_g, k_norm_g, rpb, w_pa, w_pf, w_pn, w_out):
    B, S, D = x.shape
    C = ctx.shape[1]
    L = w_in.shape[0]
    assert D == D_MODEL and C == CTX_LEN and B < MOD_ROWS
    assert S % (GRID_W * Q_ROWS) == 0 and S // GRID_W >= WIN_ROWS

    cc = jnp.concatenate([c, c_ctx[None], jnp.zeros((MOD_ROWS - B - 1, D), F32)], axis=0)
    mod = _modulation(cc, w_ada, b_ada).reshape(L * MOD_ROWS, 1, 3 * D)

    ws = gmlp_ws.astype(BF16)
    bst = jnp.swapaxes(gmlp_bs, 1, 2)
    rpb_flat = rpb.reshape(-1)
    norm_g, gmlp_ln_g, gmlp_ln_b, q_norm_g, k_norm_g = (
        t[:, None, :] for t in (norm_g, gmlp_ln_g, gmlp_ln_b, q_norm_g, k_norm_g))
    dft_lat = _dft_consts(S)
    dft_ctx = _dft_consts(C)
    x2 = x.reshape(B * S, D)
    c2 = ctx.reshape(B * C, D)
    for l in range(L):
        last = l == L - 1
        z = _in_proj(x2, norm_g, mod, w_in, l, rows_per_mod=S // IN_PROJ_ROWS, mod_row0=0, col0=0,
                     ncol=W_IN // COL)
        if last:
            zc = _in_proj(c2, norm_g, mod, w_in, l, rows_per_mod=ONE_ROW, mod_row0=B,
                          col0=OFF_K // COL, ncol=2)
            ck, cv = 0, W_C // HEAD_DIM
        else:
            zc = _in_proj(c2, norm_g, mod, w_in, l, rows_per_mod=ONE_ROW, mod_row0=B, col0=0,
                          ncol=W_IN // COL)
            ck, cv = OFF_K // HEAD_DIM, OFF_V // HEAD_DIM

        n, f, *cast = _nbr_attention(z, zc, ck, cv, q_norm_g, k_norm_g, rpb_flat, dft_lat, l, S,
                                     cast=(w_pa, w_pf, w_pn, w_out) if l == 0 else ())
        if l == 0:
            wpa, wpf, wpn, wout = cast
        x_new = _merge(f, n, z, x2, mod, gmlp_ln_g, gmlp_ln_b, ws, bst, wpa, wpf, wpn, wout, l,
                       rows_per_mod=S // MERGE_ROWS, mod_row0=0)

        if not last:
            nc, fc = _ctx_mix(zc, q_norm_g, k_norm_g, l, dft_ctx)
            c2 = _merge(fc, nc, zc, c2, mod, gmlp_ln_g, gmlp_ln_b, ws, bst, wpa, wpf, wpn, wout, l,
                        rows_per_mod=ONE_ROW, mod_row0=B)
        x2 = x_new
    return x2.reshape(B, S, D)
```
